```python
import jax, jax.numpy as jnp
from jax import lax
import numpy as np

D_MODEL = 1024
BATCH = 32
SEQ = 2048
DEPTH = 2

N_MIXERS = 2
N_A_LAYERS = (DEPTH + 1) // 2
N_B_LAYERS = DEPTH // 2
RMS_EPS = 1e-6
LRU_WIDTH = D_MODEL
LRU_HEADS = 4
LRU_BLOCK = LRU_WIDTH // LRU_HEADS
CONV_WIDTH = 4
LRU_C = 8.0
RWKV_HEAD = 64
RWKV_HEADS = D_MODEL // RWKV_HEAD
DECAY_LORA = 64
AAA_LORA = 64
GATE_LORA = 128
RWKV_GN_EPS = 64e-5
MEM_LEN = 256
MEM_HEADS = 4
MEM_HEAD_DIM = D_MODEL // MEM_HEADS
D_FF = 4 * D_MODEL

kernel_name = "hybrid_rglru_rwkv7_memxattn"


def rms_norm(x, g):
    xf = x.astype(jnp.float32)
    y = xf * lax.rsqrt(jnp.mean(xf * xf, axis=-1, keepdims=True) + RMS_EPS)
    return (y * g.astype(jnp.float32)).astype(x.dtype)


def _lru_combine(c1, c2):
    a1, b1 = c1
    a2, b2 = c2
    return a1 * a2, a2 * b1 + b2


def rglru_mixer(x, conv_w, conv_b, w_in, b_in, gate_w, gate_b, lam, w_out, b_out):
    B, S, _ = x.shape
    proj = x @ w_in + b_in
    y_branch, u = jnp.split(proj, 2, axis=-1)
    y_branch = jax.nn.gelu(y_branch, approximate=True)
    u_pad = jnp.pad(u, ((0, 0), (CONV_WIDTH - 1, 0), (0, 0)))
    conv = conv_b + u_pad[:, 0:S] * conv_w[0]
    for tap in range(1, CONV_WIDTH):
        conv = conv + u_pad[:, tap:tap + S] * conv_w[tap]
    ub = conv.reshape(B, S, LRU_HEADS, LRU_BLOCK)
    gates = jax.nn.sigmoid(jnp.einsum('bshi,ghij->gbshj', ub, gate_w) + gate_b[:, None, None])
    r_gate = gates[0].reshape(B, S, LRU_WIDTH).astype(jnp.float32)
    i_gate = gates[1].reshape(B, S, LRU_WIDTH).astype(jnp.float32)
    log_a = -LRU_C * r_gate * jax.nn.softplus(-lam.astype(jnp.float32))
    a = jnp.exp(log_a)
    mult = jnp.sqrt(-jnp.expm1(2.0 * log_a))
    b = mult * i_gate * conv.astype(jnp.float32)
    _, h = lax.associative_scan(_lru_combine, (a, b), axis=1)
    return (h.astype(x.dtype) * y_branch) @ w_out + b_out


def rwkv7_mixer(x, mu, w_rkv, w0, w1, w2, a0, a1, a2, g1, g2, k_k, k_a, r_k, gn_g, gn_b, w_o):
    B, S, D = x.shape
    H, N = RWKV_HEADS, RWKV_HEAD
    x_prev = jnp.pad(x, ((0, 0), (1, 0), (0, 0)))[:, :S]
    xx = x_prev - x
    r = (x + xx * mu[0]) @ w_rkv[0]
    xw = x + xx * mu[1]
    k = (x + xx * mu[2]) @ w_rkv[1]
    v = (x + xx * mu[3]) @ w_rkv[2]
    xa = x + xx * mu[4]
    xg = x + xx * mu[5]
    w_log = -jax.nn.softplus(-(w0 + jnp.tanh(xw @ w1) @ w2).astype(jnp.float32)) - 0.5
    decay = jnp.exp(-jnp.exp(w_log))
    a = jax.nn.sigmoid(a0 + (xa @ a1) @ a2)
    g = jax.nn.sigmoid(xg @ g1) @ g2
    kk = (k * k_k).reshape(B, S, H, N).astype(jnp.float32)
    kk = kk / jnp.maximum(jnp.linalg.norm(kk, axis=-1, keepdims=True), 1e-12)
    k = k * (1.0 + (a - 1.0) * k_a)

    rh = r.reshape(B, S, H, N).astype(jnp.float32)
    kh = k.reshape(B, S, H, N).astype(jnp.float32)
    vh = v.reshape(B, S, H, N).astype(jnp.float32)
    wh = decay.reshape(B, S, H, N)
    ah = a.reshape(B, S, H, N).astype(jnp.float32)
    rem_a = -kk
    rem_b = kk * ah

    def step(state, inp):
        r_t, w_t, k_t, v_t, a_t, b_t = inp
        sa = jnp.einsum('bhij,bhj->bhi', state, a_t)
        state = (state * w_t[:, :, None, :] + sa[..., None] * b_t[:, :, None, :]
                 + v_t[..., None] * k_t[:, :, None, :])
        y_t = jnp.einsum('bhij,bhj->bhi', state, r_t)
        return state, y_t

    seq_inputs = tuple(jnp.moveaxis(t, 1, 0) for t in (rh, wh, kh, vh, rem_a, rem_b))
    state0 = jnp.zeros((B, H, N, N), jnp.float32)
    _, ys = lax.scan(step, state0, seq_inputs)
    y = jnp.moveaxis(ys, 0, 1)
    mean = jnp.mean(y, axis=-1, keepdims=True)
    var = jnp.mean(jnp.square(y - mean), axis=-1, keepdims=True)
    yn = ((y - mean) * lax.rsqrt(var + RWKV_GN_EPS)).reshape(B, S, D)
    yn = yn * gn_g.astype(jnp.float32) + gn_b.astype(jnp.float32)
    bonus = jnp.sum(rh * kh * r_k.astype(jnp.float32), axis=-1, keepdims=True) * vh
    out = (yn + bonus.reshape(B, S, D)).astype(x.dtype)
    return (out * g) @ w_o


def mem_cross_attention(h, mem_n, w_q, w_kv, w_o):
    B, S, D = h.shape
    M = mem_n.shape[1]
    q = (h @ w_q).reshape(B, S, MEM_HEADS, MEM_HEAD_DIM)
    kv = (mem_n @ w_kv).reshape(B, M, 2, MEM_HEADS, MEM_HEAD_DIM)
    k, v = kv[:, :, 0], kv[:, :, 1]
    s = jnp.einsum('bqhd,bkhd->bhqk', q, k).astype(jnp.float32) * (MEM_HEAD_DIM ** -0.5)
    p = jax.nn.softmax(s, axis=-1).astype(h.dtype)
    o = jnp.einsum('bhqk,bkhd->bqhd', p, v).reshape(B, S, D)
    return o @ w_o


def sqrelu_mlp(h, w_up, w_down):
    return jnp.square(jax.nn.relu(h @ w_up)) @ w_down


def setup_inputs(seed: int = 0) -> dict:
    key = jax.random.key(seed)
    ks = iter(jax.random.split(key, 48))
    f32 = jnp.float32

    def nrm(shape, scale):
        return jax.random.normal(next(ks), shape, f32) * scale

    def unif(shape, lo, hi):
        return jax.random.uniform(next(ks), shape, f32, lo, hi)

    D, NA, NB = D_MODEL, N_A_LAYERS, N_B_LAYERS
    x = nrm((BATCH, SEQ, D), 1.0)
    mem = nrm((BATCH, MEM_LEN, D), 1.0)
    ln_gains = 1.0 + nrm((DEPTH, 6, D), 0.05)
    mem_norm = 1.0 + nrm((D,), 0.05)
    a_conv_w = nrm((NA, CONV_WIDTH, LRU_WIDTH), CONV_WIDTH ** -0.5)
    a_conv_b = nrm((NA, LRU_WIDTH), 0.01)
    a_w_in = nrm((NA, D, 2 * LRU_WIDTH), D ** -0.5)
    a_b_in = nrm((NA, 2 * LRU_WIDTH), 0.01)
    a_gate_w = nrm((NA, 2, LRU_HEADS, LRU_BLOCK, LRU_BLOCK), LRU_BLOCK ** -0.5)
    a_gate_b = nrm((NA, 2, LRU_HEADS, LRU_BLOCK), 0.01)
    u = unif((NA, LRU_WIDTH), 0.81, 0.998)
    sp = -0.5 * jnp.log(u)
    a_lambda = -jnp.log(jnp.expm1(sp))
    a_w_out = nrm((NA, LRU_WIDTH, D), LRU_WIDTH ** -0.5)
    a_b_out = nrm((NA, D), 0.01)
    b_mu = unif((NB, 6, D), 0.0, 1.0)
    b_w_rkv = nrm((NB, 3, D, D), D ** -0.5)
    b_w0 = unif((NB, D), -6.0, -1.0)
    b_w1 = nrm((NB, D, DECAY_LORA), D ** -0.5)
    b_w2 = nrm((NB, DECAY_LORA, D), 0.1 * DECAY_LORA ** -0.5)
    b_a0 = nrm((NB, D), 0.1)
    b_a1 = nrm((NB, D, AAA_LORA), D ** -0.5)
    b_a2 = nrm((NB, AAA_LORA, D), 0.1 * AAA_LORA ** -0.5)
    b_g1 = nrm((NB, D, GATE_LORA), D ** -0.5)
    b_g2 = nrm((NB, GATE_LORA, D), GATE_LORA ** -0.5)
    b_k_k = 0.85 + nrm((NB, D), 0.05)
    b_k_a = 1.0 + nrm((NB, D), 0.05)
    b_r_k = nrm((NB, RWKV_HEADS, RWKV_HEAD), 0.1)
    b_gn_g = 1.0 + nrm((NB, D), 0.05)
    b_gn_b = nrm((NB, D), 0.01)
    b_w_o = nrm((NB, D, D), D ** -0.5)
    c_w_q = nrm((DEPTH, D, D), D ** -0.5)
    c_w_kv = nrm((DEPTH, D, 2 * D), D ** -0.5)
    c_w_o = nrm((DEPTH, D, D), D ** -0.5)
    m_w_up = nrm((DEPTH, D, D_FF), D ** -0.5)
    m_w_down = nrm((DEPTH, D_FF, D), D_FF ** -0.5)
    return {"x": x, "mem": mem, "ln_gains": ln_gains, "mem_norm": mem_norm,
            "a_conv_w": a_conv_w, "a_conv_b": a_conv_b, "a_w_in": a_w_in, "a_b_in": a_b_in,
            "a_gate_w": a_gate_w, "a_gate_b": a_gate_b, "a_lambda": a_lambda,
            "a_w_out": a_w_out, "a_b_out": a_b_out,
            "b_mu": b_mu, "b_w_rkv": b_w_rkv, "b_w0": b_w0, "b_w1": b_w1, "b_w2": b_w2,
            "b_a0": b_a0, "b_a1": b_a1, "b_a2": b_a2, "b_g1": b_g1, "b_g2": b_g2,
            "b_k_k": b_k_k, "b_k_a": b_k_a, "b_r_k": b_r_k, "b_gn_g": b_gn_g, "b_gn_b": b_gn_b,
            "b_w_o": b_w_o,
            "c_w_q": c_w_q, "c_w_kv": c_w_kv, "c_w_o": c_w_o,
            "m_w_up": m_w_up, "m_w_down": m_w_down}


def reference(x, mem, ln_gains, mem_norm,
              a_conv_w, a_conv_b, a_w_in, a_b_in, a_gate_w, a_gate_b, a_lambda, a_w_out, a_b_out,
              b_mu, b_w_rkv, b_w0, b_w1, b_w2, b_a0, b_a1, b_a2, b_g1, b_g2,
              b_k_k, b_k_a, b_r_k, b_gn_g, b_gn_b, b_w_o,
              c_w_q, c_w_kv, c_w_o, m_w_up, m_w_down):
    mem_n = rms_norm(mem, mem_norm)
    for i in range(DEPTH):
        g = ln_gains[i]
        j = i // N_MIXERS
        hn = rms_norm(x, g[0])
        if i % N_MIXERS == 0:
            t = rglru_mixer(hn, a_conv_w[j], a_conv_b[j], a_w_in[j], a_b_in[j], a_gate_w[j],
                            a_gate_b[j], a_lambda[j], a_w_out[j], a_b_out[j])
        else:
            t = rwkv7_mixer(hn, b_mu[j], b_w_rkv[j], b_w0[j], b_w1[j], b_w2[j], b_a0[j],
                            b_a1[j], b_a2[j], b_g1[j], b_g2[j], b_k_k[j], b_k_a[j], b_r_k[j],
                            b_gn_g[j], b_gn_b[j], b_w_o[j])
        x = x + rms_norm(t, g[1])
        c = mem_cross_attention(rms_norm(x, g[2]), mem_n, c_w_q[i], c_w_kv[i], c_w_o[i])
        x = x + rms_norm(c, g[3])
        m = sqrelu_mlp(rms_norm(x, g[4]), m_w_up[i], m_w_down[i])
        x = x + rms_norm(m, g[5])
    return x
```

```python
import functools

import jax
import jax.numpy as jnp
from jax import lax
from jax.experimental import pallas as pl
from jax.experimental.pallas import tpu as pltpu

F32 = jnp.float32
BF16 = jnp.bfloat16

D_MODEL = 1024
RMS_EPS = 1e-6
LRU_HEADS = 4
LRU_BLOCK = D_MODEL // LRU_HEADS
CONV_WIDTH = 4
LRU_C = 8.0
RWKV_HEAD = 64
RWKV_HEADS = D_MODEL // RWKV_HEAD
RWKV_GN_EPS = 64e-5
MEM_HEADS = 4
MEM_HEAD_DIM = D_MODEL // MEM_HEADS
D_FF = 4 * D_MODEL

LANES = 128
SUBLANES = 8
PAIR = 2 * RWKV_HEAD
N_PAIRS = D_MODEL // PAIR
SCAN_CHUNK = 64
VMEM_LIMIT = 56 * 1024 * 1024

_SCAN_PREC = lax.Precision.HIGHEST


def _cparams(*sem):
    return pltpu.CompilerParams(dimension_semantics=sem, vmem_limit_bytes=VMEM_LIMIT)


def _rms(x, g):
    return x * lax.rsqrt(jnp.mean(x * x, axis=-1, keepdims=True) + RMS_EPS) * g


def _sigmoid(x):
    return 1.0 / (1.0 + jnp.exp(-x))


def _softplus(x):
    return jnp.maximum(x, 0.0) + jnp.log(1.0 + jnp.exp(-jnp.abs(x)))


def _gelu_tanh(x):
    return 0.5 * x * (1.0 + jnp.tanh(0.7978845608028654 * (x + 0.044715 * (x * x * x))))


def _bdot(a, b):
    return jnp.dot(a.astype(BF16), b.astype(BF16), preferred_element_type=F32)


def _const_spec(shape):
    nd = len(shape)
    return pl.BlockSpec(shape, lambda *_: (0,) * nd)


def _memkv_kernel(mem_ref, gn_ref, wkv_ref, out_ref):
    mn = _rms(mem_ref[...], gn_ref[...])
    out_ref[...] = _bdot(mn, wkv_ref[...]).astype(BF16)


def _mem_kv(mem2d, mem_norm, w_kv_bf):
    rows = mem2d.shape[0]
    depth = w_kv_bf.shape[0]
    tm = min(512, rows)
    return pl.pallas_call(
        _memkv_kernel,
        grid=(depth, rows // tm),
        in_specs=[
            pl.BlockSpec((tm, D_MODEL), lambda i, r: (r, 0)),
            _const_spec((1, D_MODEL)),
            pl.BlockSpec((None, D_MODEL, 2 * D_MODEL), lambda i, r: (i, 0, 0)),
        ],
        out_specs=pl.BlockSpec((None, tm, 2 * D_MODEL), lambda i, r: (i, r, 0)),
        out_shape=jax.ShapeDtypeStruct((depth, rows, 2 * D_MODEL), BF16),
        compiler_params=_cparams("arbitrary", "arbitrary"),
        name="mem_kv",
    )(mem2d, mem_norm, w_kv_bf)


def _rglru_kernel(x_ref, g_ref, convw_ref, convb_ref, win_ref, bin_ref, gw_ref, gb_ref, lam_ref,
                  wout_ref, bout_ref, o_ref, ubuf, a_s, b_s, h_s, hcar, *, ts):
    s = pl.program_id(1)

    @pl.when(s == 0)
    def _():
        ubuf[0:SUBLANES, :] = jnp.zeros((SUBLANES, D_MODEL), F32)
        hcar[...] = jnp.zeros((1, D_MODEL), F32)

    x = x_ref[0]
    hn = _rms(x, g_ref[0:1, :])
    proj = _bdot(hn, win_ref[...]) + bin_ref[...]
    yb = _gelu_tanh(proj[:, :D_MODEL])
    ubuf[SUBLANES:SUBLANES + ts, :] = proj[:, D_MODEL:]
    conv = convb_ref[...] + ubuf[pl.ds(SUBLANES - CONV_WIDTH + 1, ts), :] * convw_ref[0:1, :]
    for tap in range(1, CONV_WIDTH):
        conv = conv + ubuf[pl.ds(SUBLANES - CONV_WIDTH + 1 + tap, ts), :] * convw_ref[tap:tap + 1, :]
    ubuf[0:SUBLANES, :] = ubuf[ts:ts + SUBLANES, :]

    conv_bf = conv.astype(BF16)
    r_parts, i_parts = [], []
    for h in range(LRU_HEADS):
        cb = conv_bf[:, h * LRU_BLOCK:(h + 1) * LRU_BLOCK]
        r_parts.append(jnp.dot(cb, gw_ref[0, h], preferred_element_type=F32))
        i_parts.append(jnp.dot(cb, gw_ref[1, h], preferred_element_type=F32))
    r_gate = _sigmoid(jnp.concatenate(r_parts, axis=1) + gb_ref[0:1, :])
    i_gate = _sigmoid(jnp.concatenate(i_parts, axis=1) + gb_ref[1:2, :])
    a = jnp.exp((-LRU_C) * r_gate * _softplus(-lam_ref[...]))
    a_s[...] = a
    b_s[...] = jnp.sqrt(1.0 - a * a) * i_gate * conv

    def body(t, h):
        h = a_s[pl.ds(t, 1), :] * h + b_s[pl.ds(t, 1), :]
        h_s[pl.ds(t, 1), :] = h
        return h

    hcar[...] = lax.fori_loop(0, ts, body, hcar[...], unroll=8)
    out = _bdot(h_s[...] * yb, wout_ref[...]) + bout_ref[...]
    o_ref[0] = x + _rms(out, g_ref[1:2, :])


def _rglru_layer(x, g01, conv_w, conv_b, w_in_bf, b_in, gate_w_bf, gate_b, lam, w_out_bf, b_out):
    B, S, D = x.shape
    ts = min(256, S)
    tile = pl.BlockSpec((1, ts, D), lambda b, s: (b, s, 0))
    return pl.pallas_call(
        functools.partial(_rglru_kernel, ts=ts),
        grid=(B, S // ts),
        in_specs=[
            tile,
            _const_spec((2, D)),
            _const_spec((CONV_WIDTH, D)),
            _const_spec((1, D)),
            _const_spec((D, 2 * D)),
            _const_spec((1, 2 * D)),
            _const_spec((2, LRU_HEADS, LRU_BLOCK, LRU_BLOCK)),
            _const_spec((2, D)),
            _const_spec((1, D)),
            _const_spec((D, D)),
            _const_spec((1, D)),
        ],
        out_specs=tile,
        out_shape=jax.ShapeDtypeStruct((B, S, D), F32),
        scratch_shapes=[
            pltpu.VMEM((ts + SUBLANES, D), F32),
            pltpu.VMEM((ts, D), F32),
            pltpu.VMEM((ts, D), F32),
            pltpu.VMEM((ts, D), F32),
            pltpu.VMEM((1, D), F32),
        ],
        compiler_params=_cparams("arbitrary", "arbitrary"),
        name="rglru_layer",
    )(x, g01, conv_w, conv_b, w_in_bf, b_in, gate_w_bf, gate_b, lam, w_out_bf, b_out)


def _xattn_kernel(x_ref, k_ref, v_ref, g_ref, wq_ref, wo_ref, o_ref):
    x = x_ref[0]
    hn = _rms(x, g_ref[0:1, :])
    q = (_bdot(hn, wq_ref[...]) * (MEM_HEAD_DIM ** -0.5)).astype(BF16)
    outs = []
    for h in range(MEM_HEADS):
        sl = slice(h * MEM_HEAD_DIM, (h + 1) * MEM_HEAD_DIM)
        sc = lax.dot_general(q[:, sl], k_ref[:, sl], (((1,), (1,)), ((), ())),
                             preferred_element_type=F32)
        p = jnp.exp(sc - jnp.max(sc, axis=-1, keepdims=True))
        l = jnp.sum(p, axis=-1, keepdims=True)
        outs.append(jnp.dot(p.astype(BF16), v_ref[:, sl], preferred_element_type=F32) / l)
    c = _bdot(jnp.concatenate(outs, axis=1), wo_ref[...])
    o_ref[0] = x + _rms(c, g_ref[1:2, :])


def _xattn_layer(x, kv, layer, g23, w_q_bf, w_o_bf):
    B, S, D = x.shape
    M = kv.shape[2]
    tq = min(512, S)
    tile = pl.BlockSpec((1, tq, D), lambda b, s: (b, s, 0))
    return pl.pallas_call(
        _xattn_kernel,
        grid=(B, S // tq),
        in_specs=[
            tile,
            pl.BlockSpec((None, None, M, D), lambda b, s: (layer, b, 0, 0)),
            pl.BlockSpec((None, None, M, D), lambda b, s: (layer, b, 0, 1)),
            _const_spec((2, D)),
            _const_spec((D, D)),
            _const_spec((D, D)),
        ],
        out_specs=tile,
        out_shape=jax.ShapeDtypeStruct((B, S, D), F32),
        compiler_params=_cparams("arbitrary", "arbitrary"),
        name="xattn_layer",
    )(x, kv, kv, g23, w_q_bf, w_o_bf)


def _mlp_kernel(x_ref, g_ref, wup_ref, wdn_ref, o_ref):
    x = x_ref[...]
    hn = _rms(x, g_ref[0:1, :]).astype(BF16)
    acc = None
    for c in range(D_FF // D_MODEL):
        sl = slice(c * D_MODEL, (c + 1) * D_MODEL)
        h = jnp.maximum(jnp.dot(hn, wup_ref[:, sl], preferred_element_type=F32), 0.0)
        part = jnp.dot((h * h).astype(BF16), wdn_ref[sl, :], preferred_element_type=F32)
        acc = part if acc is None else acc + part
    o_ref[...] = x + _rms(acc, g_ref[1:2, :])


def _mlp_layer(x2d, g45, w_up_bf, w_down_bf):
    T, D = x2d.shape
    tm = min(512, T)
    tile = pl.BlockSpec((tm, D), lambda r: (r, 0))
    return pl.pallas_call(
        _mlp_kernel,
        grid=(T // tm,),
        in_specs=[
            tile,
            _const_spec((2, D)),
            pl.BlockSpec((D, D_FF), lambda r: (0, 0), pipeline_mode=pl.Buffered(1)),
            pl.BlockSpec((D_FF, D), lambda r: (0, 0), pipeline_mode=pl.Buffered(1)),
        ],
        out_specs=tile,
        out_shape=jax.ShapeDtypeStruct((T, D), F32),
        compiler_params=_cparams("arbitrary"),
        name="mlp_layer",
    )(x2d, g45, w_up_bf, w_down_bf)


def _head_sum(x, e_ref, et_ref):
    hi = x.astype(BF16)
    lo = (x - hi.astype(F32)).astype(BF16)
    s = (jnp.dot(hi, e_ref[...], preferred_element_type=F32)
         + jnp.dot(lo, e_ref[...], preferred_element_type=F32))
    shi = s.astype(BF16)
    slo = (s - shi.astype(F32)).astype(BF16)
    return (jnp.dot(shi, et_ref[...], preferred_element_type=F32)
            + jnp.dot(slo, et_ref[...], preferred_element_type=F32))


def _rwkv_pre_kernel(x_ref, g_ref, mu_ref, wrkv_ref, w0_ref, w1_ref, w2_ref, a0_ref, a1_ref, a2_ref,
                     g1_ref, g2_ref, kk_ref, ka_ref, rk_ref, e_ref, et_ref,
                     r_out, lw_out, k_out, v_out, a_out, b_out, g_out, bonus_out, hbuf, *, ts):
    s = pl.program_id(1)

    @pl.when(s == 0)
    def _():
        hbuf[0:SUBLANES, :] = jnp.zeros((SUBLANES, D_MODEL), F32)

    hn = _rms(x_ref[0], g_ref[...])
    hbuf[SUBLANES:SUBLANES + ts, :] = hn
    xx = hbuf[pl.ds(SUBLANES - 1, ts), :] - hn
    hbuf[0:SUBLANES, :] = hbuf[ts:ts + SUBLANES, :]

    def mix(i):
        return hn + xx * mu_ref[i:i + 1, :]

    r = _bdot(mix(0), wrkv_ref[0])
    k = _bdot(mix(2), wrkv_ref[1])
    v = _bdot(mix(3), wrkv_ref[2])
    z = w0_ref[...] + _bdot(jnp.tanh(_bdot(mix(1), w1_ref[...])), w2_ref[...])
    lw = -jnp.exp(-_softplus(-z) - 0.5)
    alr = _sigmoid(a0_ref[...] + _bdot(_bdot(mix(4), a1_ref[...]), a2_ref[...]))
    g = _bdot(_sigmoid(_bdot(mix(5), g1_ref[...])), g2_ref[...])

    kk = k * kk_ref[...]
    kk = kk / jnp.maximum(jnp.sqrt(_head_sum(kk * kk, e_ref, et_ref)), 1e-12)
    k = k * (1.0 + (alr - 1.0) * ka_ref[...])
    r_out[0] = r
    lw_out[0] = lw
    k_out[0] = k
    v_out[0] = v
    a_out[0] = -kk
    b_out[0] = kk * alr
    g_out[0] = g
    bonus_out[0] = _head_sum(r * k * rk_ref[...], e_ref, et_ref) * v


def _rwkv_pre(x, g0, mu, w_rkv_bf, w0, w1_bf, w2_bf, a0, a1_bf, a2_bf, g1_bf, g2_bf, k_k, k_a, r_k, e, et):
    B, S, D = x.shape
    ts = min(256, S)
    tile = pl.BlockSpec((1, ts, D), lambda b, s: (b, s, 0))
    consts = [g0, mu, w_rkv_bf, w0, w1_bf, w2_bf, a0, a1_bf, a2_bf, g1_bf, g2_bf, k_k, k_a, r_k, e, et]
    return pl.pallas_call(
        functools.partial(_rwkv_pre_kernel, ts=ts),
        grid=(B, S // ts),
        in_specs=[tile] + [_const_spec(c.shape) for c in consts],
        out_specs=[tile] * 8,
        out_shape=[jax.ShapeDtypeStruct((B, S, D), F32)] * 8,
        scratch_shapes=[pltpu.VMEM((ts + SUBLANES, D), F32)],
        compiler_params=_cparams("arbitrary", "arbitrary"),
        name="rwkv_pre",
    )(x, *consts)


def _sdot(a, b):
    return jnp.dot(a, b, precision=_SCAN_PREC, preferred_element_type=F32)


def _sdot_nt(a, b):
    return lax.dot_general(a, b, (((1,), (1,)), ((), ())), precision=_SCAN_PREC,
                           preferred_element_type=F32)


def _rwkv_scan_kernel(r_ref, lw_ref, k_ref, v_ref, a_ref, b_ref, y_ref, state, *, L):
    @pl.when(pl.program_id(1) == 0)
    def _():
        state[...] = jnp.zeros(state.shape, F32)

    row_l = lax.broadcasted_iota(jnp.int32, (L, L), 0)
    col_l = lax.broadcasted_iota(jnp.int32, (L, L), 1)
    tri_incl = (row_l >= col_l).astype(F32)
    lw = lw_ref[0]
    c = jnp.dot(tri_incl, lw, precision=lax.Precision.HIGHEST, preferred_element_type=F32)
    c_end = c[L - 1:L, :]
    gam = jnp.exp(c)
    gam_inv = jnp.exp(-c)
    gam_end = jnp.exp(c_end - c)
    r_t = r_ref[0] * gam
    a_t = a_ref[0] * jnp.exp(c - lw)
    b_t = b_ref[0] * gam_inv
    k_t = k_ref[0] * gam_inv
    b_h = b_ref[0] * gam_end
    k_h = k_ref[0] * gam_end
    gam_last = jnp.exp(c_end)
    v_all = v_ref[0]

    row = lax.broadcasted_iota(jnp.int32, (2 * L, 2 * L), 0)
    col = lax.broadcasted_iota(jnp.int32, (2 * L, 2 * L), 1)
    same_head = (row >= L) == (col >= L)
    m_strict = same_head & (row > col)
    m_incl = same_head & (row >= col)
    eye = (row == col).astype(F32)
    prow = lax.broadcasted_iota(jnp.int32, (PAIR, PAIR), 0)
    pcol = lax.broadcasted_iota(jnp.int32, (PAIR, PAIR), 1)
    m_state = (prow >= RWKV_HEAD) == (pcol >= RWKV_HEAD)
    first = lax.broadcasted_iota(jnp.int32, (L, PAIR), 1) < RWKV_HEAD

    def stack(z):
        return jnp.concatenate([jnp.where(first, z, 0.0), jnp.where(first, 0.0, z)], axis=0)

    def fold(z2):
        return z2[:L] + z2[L:]

    ys = []
    for p in range(N_PAIRS):
        sl = slice(p * PAIR, (p + 1) * PAIR)
        a2, r2, v2 = stack(a_t[:, sl]), stack(r_t[:, sl]), stack(v_all[:, sl])
        bk = jnp.concatenate([b_t[:, sl], b_t[:, sl], k_t[:, sl], k_t[:, sl]], axis=0)
        g = _sdot_nt(jnp.concatenate([a2, r2], axis=0), bk)
        n_ab = jnp.where(m_strict, g[:2 * L, :2 * L], 0.0)
        a_ak = jnp.where(m_strict, g[:2 * L, 2 * L:], 0.0)
        a_rb = jnp.where(m_incl, g[2 * L:, :2 * L], 0.0)
        a_rk = jnp.where(m_incl, g[2 * L:, 2 * L:], 0.0)
        tinv = eye + n_ab
        pw = n_ab
        span = 2
        while span < L:
            pw = _sdot(pw, pw)
            tinv = tinv + _sdot(tinv, pw)
            span *= 2
        s0 = state[p]
        w2 = stack(_sdot_nt(a_t[:, sl], s0)) + _sdot(a_ak, v2)
        u2 = _sdot(tinv, w2)
        u = fold(u2)
        y = _sdot_nt(r_t[:, sl], s0) + fold(_sdot(a_rb, u2) + _sdot(a_rk, v2))
        ys.append(y)
        upd = _sdot(jnp.concatenate([u, v_all[:, sl]], axis=0).T,
                    jnp.concatenate([b_h[:, sl], k_h[:, sl]], axis=0))
        state[p] = s0 * gam_last[:, sl] + jnp.where(m_state, upd, 0.0)
    y_ref[0] = jnp.concatenate(ys, axis=1)


def _rwkv_scan(r, lw, k, v, a, b):
    B, S, D = r.shape
    L = min(SCAN_CHUNK, S)
    tile = pl.BlockSpec((1, L, D), lambda bb, c: (bb, c, 0))
    return pl.pallas_call(
        functools.partial(_rwkv_scan_kernel, L=L),
        grid=(B, S // L),
        in_specs=[tile] * 6,
        out_specs=tile,
        out_shape=jax.ShapeDtypeStruct((B, S, D), F32),
        scratch_shapes=[pltpu.VMEM((N_PAIRS, PAIR, PAIR), F32)],
        compiler_params=_cparams("arbitrary", "arbitrary"),
        name="rwkv_scan",
    )(r, lw, k, v, a, b)


def _rwkv_post_kernel(x_ref, y_ref, bonus_ref, gate_ref, g_ref, gng_ref, gnb_ref, wo_ref, e_ref, et_ref,
                      o_ref):
    y = y_ref[...]
    mean = _head_sum(y, e_ref, et_ref) * (1.0 / RWKV_HEAD)
    d = y - mean
    var = _head_sum(d * d, e_ref, et_ref) * (1.0 / RWKV_HEAD)
    yn = d * lax.rsqrt(var + RWKV_GN_EPS) * gng_ref[...] + gnb_ref[...]
    t = _bdot((yn + bonus_ref[...]) * gate_ref[...], wo_ref[...])
    o_ref[...] = x_ref[...] + _rms(t, g_ref[...])


def _rwkv_post(x2d, y2d, bonus2d, gate2d, g1, gn_g, gn_b, w_o_bf, e, et):
    T, D = x2d.shape
    tm = min(512, T)
    tile = pl.BlockSpec((tm, D), lambda r: (r, 0))
    consts = [g1, gn_g, gn_b, w_o_bf, e, et]
    return pl.pallas_call(
        _rwkv_post_kernel,
        grid=(T // tm,),
        in_specs=[tile] * 4 + [_const_spec(c.shape) for c in consts],
        out_specs=tile,
        out_shape=jax.ShapeDtypeStruct((T, D), F32),
        compiler_params=_cparams("arbitrary"),
        name="rwkv_post",
    )(x2d, y2d, bonus2d, gate2d, *consts)


def _head_selectors():
    lane = jnp.arange(D_MODEL)[:, None] // RWKV_HEAD
    e = (lane == jnp.arange(LANES)[None, :]).astype(BF16)
    return e, e.T


def kernel(x, mem, ln_gains, mem_norm, a_conv_w, a_conv_b, a_w_in, a_b_in, a_gate_w, a_gate_b, a_lambda, a_w_out, a_b_out, b_mu, b_w_rkv, b_w0, b_w1, b_w2, b_a0, b_a1, b_a2, b_g1, b_g2, b_k_k, b_k_a, b_r_k, b_gn_g, b_gn_b, b_w_o, c_w_q, c_w_kv, c_w_o, m_w_up, m_w_down):
    B, S, D = x.shape
    M = mem.shape[1]
    depth = ln_gains.shape[0]
    assert D == D_MODEL and depth == 2
    bf = lambda w: w.astype(BF16)
    row = lambda p: p.reshape(1, -1)

    kv = _mem_kv(mem.reshape(B * M, D), row(mem_norm), bf(c_w_kv)).reshape(depth, B, M, 2 * D)
    e, et = _head_selectors()

    for i in range(depth):
        g = ln_gains[i]
        j = i // 2
        if i % 2 == 0:
            x = _rglru_layer(x, g[0:2], a_conv_w[j], row(a_conv_b[j]), bf(a_w_in[j]), row(a_b_in[j]),
                             bf(a_gate_w[j]), a_gate_b[j].reshape(2, D), row(a_lambda[j]),
                             bf(a_w_out[j]), row(a_b_out[j]))
        else:
            r, lw, k, v, a, b, gate, bonus = _rwkv_pre(
                x, g[0:1], b_mu[j], bf(b_w_rkv[j]), row(b_w0[j]), bf(b_w1[j]), bf(b_w2[j]), row(b_a0[j]),
                bf(b_a1[j]), bf(b_a2[j]), bf(b_g1[j]), bf(b_g2[j]), row(b_k_k[j]), row(b_k_a[j]),
                row(b_r_k[j]), e, et)
            y = _rwkv_scan(r, lw, k, v, a, b)
            x = _rwkv_post(x.reshape(B * S, D), y.reshape(B * S, D), bonus.reshape(B * S, D),
                           gate.reshape(B * S, D), g[1:2], row(b_gn_g[j]), row(b_gn_b[j]), bf(b_w_o[j]),
                           e, et).reshape(B, S, D)
        x = _xattn_layer(x, kv, i, g[2:4], bf(c_w_q[i]), bf(c_w_o[i]))
        x = _mlp_layer(x.reshape(B * S, D), g[4:6], bf(m_w_up[i]), bf(m_w_down[i])).reshape(B, S, D)
    return x
```

```python
import functools

import jax
import jax.numpy as jnp
from jax import lax
from jax.experimental import pallas as pl
from jax.experimental.pallas import tpu as pltpu

F32 = jnp.float32
BF16 = jnp.bfloat16

D_MODEL = 1024
RMS_EPS = 1e-6
LRU_HEADS = 4
LRU_BLOCK = D_MODEL // LRU_HEADS
CONV_WIDTH = 4
LRU_C = 8.0
RWKV_HEAD = 64
RWKV_HEADS = D_MODEL // RWKV_HEAD
RWKV_GN_EPS = 64e-5
MEM_HEADS = 4
MEM_HEAD_DIM = D_MODEL // MEM_HEADS
D_FF = 4 * D_MODEL

LANES = 128
SUBLANES = 8
PAIR = 2 * RWKV_HEAD
N_PAIRS = D_MODEL // PAIR
SCAN_CHUNK = 64
VMEM_LIMIT = 56 * 1024 * 1024

PREC_SCORE = "bf16"
PREC_INV = "bf16"
PREC_MIX = "bf16"
PREC_STATE = "bf16"


def _cparams(*sem):
    return pltpu.CompilerParams(dimension_semantics=sem, vmem_limit_bytes=VMEM_LIMIT)


def _rms(x, g):
    return x * lax.rsqrt(jnp.mean(x * x, axis=-1, keepdims=True) + RMS_EPS) * g


def _sigmoid(x):
    return 1.0 / (1.0 + jnp.exp(-x))


def _softplus(x):
    return jnp.maximum(x, 0.0) + jnp.log(1.0 + jnp.exp(-jnp.abs(x)))


def _gelu_tanh(x):
    return 0.5 * x * (1.0 + jnp.tanh(0.7978845608028654 * (x + 0.044715 * (x * x * x))))


def _bdot(a, b):
    return jnp.dot(a.astype(BF16), b.astype(BF16), preferred_element_type=F32)


def _const_spec(shape):
    nd = len(shape)
    return pl.BlockSpec(shape, lambda *_: (0,) * nd)


def _memkv_kernel(mem_ref, gn_ref, wkv_ref, out_ref):
    mn = _rms(mem_ref[...], gn_ref[...])
    out_ref[...] = _bdot(mn, wkv_ref[...]).astype(BF16)


def _mem_kv(mem2d, mem_norm, w_kv_bf):
    rows = mem2d.shape[0]
    depth = w_kv_bf.shape[0]
    tm = min(512, rows)
    return pl.pallas_call(
        _memkv_kernel,
        grid=(depth, rows // tm),
        in_specs=[
            pl.BlockSpec((tm, D_MODEL), lambda i, r: (r, 0)),
            _const_spec((1, D_MODEL)),
            pl.BlockSpec((None, D_MODEL, 2 * D_MODEL), lambda i, r: (i, 0, 0)),
        ],
        out_specs=pl.BlockSpec((None, tm, 2 * D_MODEL), lambda i, r: (i, r, 0)),
        out_shape=jax.ShapeDtypeStruct((depth, rows, 2 * D_MODEL), BF16),
        compiler_params=_cparams("arbitrary", "arbitrary"),
        name="mem_kv",
    )(mem2d, mem_norm, w_kv_bf)


def _rglru_kernel(x_ref, g_ref, convw_ref, convb_ref, win_ref, bin_ref, gw_ref, gb_ref, lam_ref,
                  wout_ref, bout_ref, o_ref, ubuf, a_s, b_s, h_s, hcar, *, ts):
    s = pl.program_id(1)

    @pl.when(s == 0)
    def _():
        ubuf[0:SUBLANES, :] = jnp.zeros((SUBLANES, D_MODEL), F32)
        hcar[...] = jnp.zeros((1, D_MODEL), F32)

    x = x_ref[0]
    hn = _rms(x, g_ref[0:1, :])
    proj = _bdot(hn, win_ref[...]) + bin_ref[...]
    yb = _gelu_tanh(proj[:, :D_MODEL])
    ubuf[SUBLANES:SUBLANES + ts, :] = proj[:, D_MODEL:]
    conv = convb_ref[...] + ubuf[pl.ds(SUBLANES - CONV_WIDTH + 1, ts), :] * convw_ref[0:1, :]
    for tap in range(1, CONV_WIDTH):
        conv = conv + ubuf[pl.ds(SUBLANES - CONV_WIDTH + 1 + tap, ts), :] * convw_ref[tap:tap + 1, :]
    ubuf[0:SUBLANES, :] = ubuf[ts:ts + SUBLANES, :]

    conv_bf = conv.astype(BF16)
    r_parts, i_parts = [], []
    for h in range(LRU_HEADS):
        cb = conv_bf[:, h * LRU_BLOCK:(h + 1) * LRU_BLOCK]
        r_parts.append(jnp.dot(cb, gw_ref[0, h], preferred_element_type=F32))
        i_parts.append(jnp.dot(cb, gw_ref[1, h], preferred_element_type=F32))
    r_gate = _sigmoid(jnp.concatenate(r_parts, axis=1) + gb_ref[0:1, :])
    i_gate = _sigmoid(jnp.concatenate(i_parts, axis=1) + gb_ref[1:2, :])
    a = jnp.exp((-LRU_C) * r_gate * _softplus(-lam_ref[...]))
    a_s[...] = a
    b_s[...] = jnp.sqrt(1.0 - a * a) * i_gate * conv

    def body(t, h):
        h = a_s[pl.ds(t, 1), :] * h + b_s[pl.ds(t, 1), :]
        h_s[pl.ds(t, 1), :] = h
        return h

    hcar[...] = lax.fori_loop(0, ts, body, hcar[...], unroll=8)
    out = _bdot(h_s[...] * yb, wout_ref[...]) + bout_ref[...]
    o_ref[0] = x + _rms(out, g_ref[1:2, :])


def _rglru_layer(x, g01, conv_w, conv_b, w_in_bf, b_in, gate_w_bf, gate_b, lam, w_out_bf, b_out):
    B, S, D = x.shape
    ts = min(256, S)
    tile = pl.BlockSpec((1, ts, D), lambda b, s: (b, s, 0))
    return pl.pallas_call(
        functools.partial(_rglru_kernel, ts=ts),
        grid=(B, S // ts),
        in_specs=[
            tile,
            _const_spec((2, D)),
            _const_spec((CONV_WIDTH, D)),
            _const_spec((1, D)),
            _const_spec((D, 2 * D)),
            _const_spec((1, 2 * D)),
            _const_spec((2, LRU_HEADS, LRU_BLOCK, LRU_BLOCK)),
            _const_spec((2, D)),
            _const_spec((1, D)),
            _const_spec((D, D)),
            _const_spec((1, D)),
        ],
        out_specs=tile,
        out_shape=jax.ShapeDtypeStruct((B, S, D), F32),
        scratch_shapes=[
            pltpu.VMEM((ts + SUBLANES, D), F32),
            pltpu.VMEM((ts, D), F32),
            pltpu.VMEM((ts, D), F32),
            pltpu.VMEM((ts, D), F32),
            pltpu.VMEM((1, D), F32),
        ],
        compiler_params=_cparams("arbitrary", "arbitrary"),
        name="rglru_layer",
    )(x, g01, conv_w, conv_b, w_in_bf, b_in, gate_w_bf, gate_b, lam, w_out_bf, b_out)


def _xattn_kernel(x_ref, k_ref, v_ref, g_ref, wq_ref, wo_ref, o_ref):
    x = x_ref[0]
    hn = _rms(x, g_ref[0:1, :])
    q = (_bdot(hn, wq_ref[...]) * (MEM_HEAD_DIM ** -0.5)).astype(BF16)
    outs = []
    for h in range(MEM_HEADS):
        sl = slice(h * MEM_HEAD_DIM, (h + 1) * MEM_HEAD_DIM)
        sc = lax.dot_general(q[:, sl], k_ref[:, sl], (((1,), (1,)), ((), ())),
                             preferred_element_type=F32)
        p = jnp.exp(sc - jnp.max(sc, axis=-1, keepdims=True))
        l = jnp.sum(p, axis=-1, keepdims=True)
        outs.append(jnp.dot(p.astype(BF16), v_ref[:, sl], preferred_element_type=F32) / l)
    c = _bdot(jnp.concatenate(outs, axis=1), wo_ref[...])
    o_ref[0] = x + _rms(c, g_ref[1:2, :])


def _xattn_layer(x, kv, layer, g23, w_q_bf, w_o_bf):
    B, S, D = x.shape
    M = kv.shape[2]
    tq = min(512, S)
    tile = pl.BlockSpec((1, tq, D), lambda b, s: (b, s, 0))
    return pl.pallas_call(
        _xattn_kernel,
        grid=(B, S // tq),
        in_specs=[
            tile,
            pl.BlockSpec((None, None, M, D), lambda b, s: (layer, b, 0, 0)),
            pl.BlockSpec((None, None, M, D), lambda b, s: (layer, b, 0, 1)),
            _const_spec((2, D)),
            _const_spec((D, D)),
            _const_spec((D, D)),
        ],
        out_specs=tile,
        out_shape=jax.ShapeDtypeStruct((B, S, D), F32),
        compiler_params=_cparams("arbitrary", "arbitrary"),
        name="xattn_layer",
    )(x, kv, kv, g23, w_q_bf, w_o_bf)


def _mlp_kernel(x_ref, g_ref, wup_ref, wdn_ref, o_ref):
    x = x_ref[...]
    hn = _rms(x, g_ref[0:1, :]).astype(BF16)
    acc = None
    for c in range(D_FF // D_MODEL):
        sl = slice(c * D_MODEL, (c + 1) * D_MODEL)
        h = jnp.maximum(jnp.dot(hn, wup_ref[:, sl], preferred_element_type=F32), 0.0)
        part = jnp.dot((h * h).astype(BF16), wdn_ref[sl, :], preferred_element_type=F32)
        acc = part if acc is None else acc + part
    o_ref[...] = x + _rms(acc, g_ref[1:2, :])


def _mlp_layer(x2d, g45, w_up_bf, w_down_bf):
    T, D = x2d.shape
    tm = min(512, T)
    tile = pl.BlockSpec((tm, D), lambda r: (r, 0))
    return pl.pallas_call(
        _mlp_kernel,
        grid=(T // tm,),
        in_specs=[
            tile,
            _const_spec((2, D)),
            pl.BlockSpec((D, D_FF), lambda r: (0, 0), pipeline_mode=pl.Buffered(1)),
            pl.BlockSpec((D_FF, D), lambda r: (0, 0), pipeline_mode=pl.Buffered(1)),
        ],
        out_specs=tile,
        out_shape=jax.ShapeDtypeStruct((T, D), F32),
        compiler_params=_cparams("arbitrary"),
        name="mlp_layer",
    )(x2d, g45, w_up_bf, w_down_bf)


def _head_sum(x, e_ref, et_ref):
    hi = x.astype(BF16)
    lo = (x - hi.astype(F32)).astype(BF16)
    s = (jnp.dot(hi, e_ref[...], preferred_element_type=F32)
         + jnp.dot(lo, e_ref[...], preferred_element_type=F32))
    shi = s.astype(BF16)
    slo = (s - shi.astype(F32)).astype(BF16)
    return (jnp.dot(shi, et_ref[...], preferred_element_type=F32)
            + jnp.dot(slo, et_ref[...], preferred_element_type=F32))


def _rwkv_pre_kernel(x_ref, g_ref, mu_ref, wrkv_ref, w0_ref, w1_ref, w2_ref, a0_ref, a1_ref, a2_ref,
                     g1_ref, g2_ref, kk_ref, ka_ref, rk_ref, e_ref, et_ref,
                     r_out, lw_out, k_out, v_out, a_out, b_out, g_out, bonus_out, hbuf, *, ts):
    s = pl.program_id(1)

    @pl.when(s == 0)
    def _():
        hbuf[0:SUBLANES, :] = jnp.zeros((SUBLANES, D_MODEL), F32)

    hn = _rms(x_ref[0], g_ref[...])
    hbuf[SUBLANES:SUBLANES + ts, :] = hn
    xx = hbuf[pl.ds(SUBLANES - 1, ts), :] - hn
    hbuf[0:SUBLANES, :] = hbuf[ts:ts + SUBLANES, :]

    def mix(i):
        return hn + xx * mu_ref[i:i + 1, :]

    r = _bdot(mix(0), wrkv_ref[0])
    k = _bdot(mix(2), wrkv_ref[1])
    v = _bdot(mix(3), wrkv_ref[2])
    z = w0_ref[...] + _bdot(jnp.tanh(_bdot(mix(1), w1_ref[...])), w2_ref[...])
    lw = -jnp.exp(-_softplus(-z) - 0.5)
    alr = _sigmoid(a0_ref[...] + _bdot(_bdot(mix(4), a1_ref[...]), a2_ref[...]))
    g = _bdot(_sigmoid(_bdot(mix(5), g1_ref[...])), g2_ref[...])

    kk = k * kk_ref[...]
    kk = kk / jnp.maximum(jnp.sqrt(_head_sum(kk * kk, e_ref, et_ref)), 1e-12)
    k = k * (1.0 + (alr - 1.0) * ka_ref[...])
    r_out[0] = r
    lw_out[0] = lw
    k_out[0] = k
    v_out[0] = v
    a_out[0] = -kk
    b_out[0] = kk * alr
    g_out[0] = g
    bonus_out[0] = _head_sum(r * k * rk_ref[...], e_ref, et_ref) * v


def _rwkv_pre(x, g0, mu, w_rkv_bf, w0, w1_bf, w2_bf, a0, a1_bf, a2_bf, g1_bf, g2_bf, k_k, k_a, r_k, e, et):
    B, S, D = x.shape
    ts = min(256, S)
    tile = pl.BlockSpec((1, ts, D), lambda b, s: (b, s, 0))
    consts = [g0, mu, w_rkv_bf, w0, w1_bf, w2_bf, a0, a1_bf, a2_bf, g1_bf, g2_bf, k_k, k_a, r_k, e, et]
    return pl.pallas_call(
        functools.partial(_rwkv_pre_kernel, ts=ts),
        grid=(B, S // ts),
        in_specs=[tile] + [_const_spec(c.shape) for c in consts],
        out_specs=[tile] * 8,
        out_shape=[jax.ShapeDtypeStruct((B, S, D), F32)] * 8,
        scratch_shapes=[pltpu.VMEM((ts + SUBLANES, D), F32)],
        compiler_params=_cparams("arbitrary", "arbitrary"),
        name="rwkv_pre",
    )(x, *consts)


_NN = (((1,), (0,)), ((), ()))
_NT = (((1,), (1,)), ((), ()))
_BNN = (((2,), (1,)), ((0,), (0,)))
_BNT = (((2,), (2,)), ((0,), (0,)))
_BTN = (((1,), (1,)), ((0,), (0,)))


def _split(a):
    hi = a.astype(BF16)
    return hi, (a - hi.astype(F32)).astype(BF16)


def _sdot(a, b, mode, dims=_NN):
    if mode == "f32":
        return lax.dot_general(a, b, dims, precision=lax.Precision.HIGHEST, preferred_element_type=F32)
    if mode == "bf16":
        return lax.dot_general(a.astype(BF16), b.astype(BF16), dims, preferred_element_type=F32)
    a_hi, a_lo = _split(a)
    b_hi, b_lo = _split(b)
    return (lax.dot_general(a_hi, b_hi, dims, preferred_element_type=F32)
            + lax.dot_general(a_hi, b_lo, dims, preferred_element_type=F32)
            + lax.dot_general(a_lo, b_hi, dims, preferred_element_type=F32))


def _rwkv_scan_kernel(r_ref, lw_ref, k_ref, v_ref, a_ref, b_ref, y_ref, state, *, L):
    @pl.when(pl.program_id(1) == 0)
    def _():
        state[...] = jnp.zeros(state.shape, F32)

    row_l = lax.broadcasted_iota(jnp.int32, (L, L), 0)
    col_l = lax.broadcasted_iota(jnp.int32, (L, L), 1)
    tri_incl = (row_l >= col_l).astype(F32)
    lw = lw_ref[0]
    c = jnp.dot(tri_incl, lw, precision=lax.Precision.HIGHEST, preferred_element_type=F32)
    c_end = c[L - 1:L, :]
    gam = jnp.exp(c)
    gam_inv = jnp.exp(-c)
    gam_end = jnp.exp(c_end - c)
    r_t = r_ref[0] * gam
    a_t = a_ref[0] * jnp.exp(c - lw)
    b_t = b_ref[0] * gam_inv
    k_t = k_ref[0] * gam_inv
    b_h = b_ref[0] * gam_end
    k_h = k_ref[0] * gam_end
    gam_last = jnp.exp(c_end)
    v_all = v_ref[0]

    row = lax.broadcasted_iota(jnp.int32, (2 * L, 2 * L), 0)
    col = lax.broadcasted_iota(jnp.int32, (2 * L, 2 * L), 1)
    same_head = (row >= L) == (col >= L)
    m_strict = same_head & (row > col)
    m_incl = same_head & (row >= col)
    eye = (row == col).astype(F32)
    m_off = []
    m = 1
    while m < L:
        m_off.append(((row // (2 * m)) == (col // (2 * m))) & ((row % (2 * m)) >= m) & ((col % (2 * m)) < m))
        m *= 2
    prow = lax.broadcasted_iota(jnp.int32, (PAIR, PAIR), 0)
    pcol = lax.broadcasted_iota(jnp.int32, (PAIR, PAIR), 1)
    m_state = (prow >= RWKV_HEAD) == (pcol >= RWKV_HEAD)
    first = lax.broadcasted_iota(jnp.int32, (L, PAIR), 1) < RWKV_HEAD

    def pairs(z):
        return jnp.stack([z[:, p * PAIR:(p + 1) * PAIR] for p in range(N_PAIRS)], axis=0)

    def stack(z):
        return jnp.concatenate([jnp.where(first, z, 0.0), jnp.where(first, 0.0, z)], axis=1)

    def fold(z2):
        return z2[:, :L] + z2[:, L:]

    a_p, r_p, v_p, b_p, k_p = pairs(a_t), pairs(r_t), pairs(v_all), pairs(b_t), pairs(k_t)
    a2, r2, v2 = stack(a_p), stack(r_p), stack(v_p)
    g = _sdot(jnp.concatenate([a2, r2], axis=1), jnp.concatenate([b_p, b_p, k_p, k_p], axis=1),
              PREC_SCORE, _BNT)
    n_ab = g[:, :2 * L, :2 * L]
    a_ak = jnp.where(m_strict, g[:, :2 * L, 2 * L:], 0.0)
    a_rb = jnp.where(m_incl, g[:, 2 * L:, :2 * L], 0.0)
    a_rk = jnp.where(m_incl, g[:, 2 * L:, 2 * L:], 0.0)
    tinv = eye + jnp.where(m_off[0], n_ab, 0.0)
    for m_lvl in m_off[1:]:
        n_off = jnp.where(m_lvl, n_ab, 0.0)
        tinv = tinv + _sdot(tinv, _sdot(n_off, tinv, PREC_INV, _BNN), PREC_INV, _BNN)
    s0 = state[...]
    w2 = stack(_sdot(a_p, s0, PREC_STATE, _BNT)) + _sdot(a_ak, v2, PREC_MIX, _BNN)
    u2 = _sdot(tinv, w2, PREC_MIX, _BNN)
    u = fold(u2)
    y = (_sdot(r_p, s0, PREC_STATE, _BNT)
         + fold(_sdot(a_rb, u2, PREC_MIX, _BNN) + _sdot(a_rk, v2, PREC_MIX, _BNN)))
    upd = _sdot(jnp.concatenate([u, v_p], axis=1), jnp.concatenate([pairs(b_h), pairs(k_h)], axis=1),
                PREC_STATE, _BTN)
    state[...] = s0 * pairs(gam_last) + jnp.where(m_state, upd, 0.0)
    y_ref[0] = jnp.concatenate([y[p] for p in range(N_PAIRS)], axis=1)


def _rwkv_scan(r, lw, k, v, a, b):
    B, S, D = r.shape
    L = min(SCAN_CHUNK, S)
    tile = pl.BlockSpec((1, L, D), lambda bb, c: (bb, c, 0))
    return pl.pallas_call(
        functools.partial(_rwkv_scan_kernel, L=L),
        grid=(B, S // L),
        in_specs=[tile] * 6,
        out_specs=tile,
        out_shape=jax.ShapeDtypeStruct((B, S, D), F32),
        scratch_shapes=[pltpu.VMEM((N_PAIRS, PAIR, PAIR), F32)],
        compiler_params=_cparams("arbitrary", "arbitrary"),
        name="rwkv_scan",
    )(r, lw, k, v, a, b)


def _rwkv_post_kernel(x_ref, y_ref, bonus_ref, gate_ref, g_ref, gng_ref, gnb_ref, wo_ref, e_ref, et_ref,
                      o_ref):
    y = y_ref[...]
    mean = _head_sum(y, e_ref, et_ref) * (1.0 / RWKV_HEAD)
    d = y - mean
    var = _head_sum(d * d, e_ref, et_ref) * (1.0 / RWKV_HEAD)
    yn = d * lax.rsqrt(var + RWKV_GN_EPS) * gng_ref[...] + gnb_ref[...]
    t = _bdot((yn + bonus_ref[...]) * gate_ref[...], wo_ref[...])
    o_ref[...] = x_ref[...] + _rms(t, g_ref[...])


def _rwkv_post(x2d, y2d, bonus2d, gate2d, g1, gn_g, gn_b, w_o_bf, e, et):
    T, D = x2d.shape
    tm = min(512, T)
    tile = pl.BlockSpec((tm, D), lambda r: (r, 0))
    consts = [g1, gn_g, gn_b, w_o_bf, e, et]
    return pl.pallas_call(
        _rwkv_post_kernel,
        grid=(T // tm,),
        in_specs=[tile] * 4 + [_const_spec(c.shape) for c in consts],
        out_specs=tile,
        out_shape=jax.ShapeDtypeStruct((T, D), F32),
        compiler_params=_cparams("arbitrary"),
        name="rwkv_post",
    )(x2d, y2d, bonus2d, gate2d, *consts)


def _head_selectors():
    lane = jnp.arange(D_MODEL)[:, None] // RWKV_HEAD
    e = (lane == jnp.arange(LANES)[None, :]).astype(BF16)
    return e, e.T


def kernel(x, mem, ln_gains, mem_norm, a_conv_w, a_conv_b, a_w_in, a_b_in, a_gate_w, a_gate_b, a_lambda, a_w_out, a_b_out, b_mu, b_w_rkv, b_w0, b_w1, b_w2, b_a0, b_a1, b_a2, b_g1, b_g2, b_k_k, b_k_a, b_r_k, b_gn_g, b_gn_b, b_w_o, c_w_q, c_w_kv, c_w_o, m_w_up, m_w_down):
    B, S, D = x.shape
    M = mem.shape[1]
    depth = ln_gains.shape[0]
    assert D == D_MODEL and depth == 2
    bf = lambda w: w.astype(BF16)
    row = lambda p: p.reshape(1, -1)

    kv = _mem_kv(mem.reshape(B * M, D), row(mem_norm), bf(c_w_kv)).reshape(depth, B, M, 2 * D)
    e, et = _head_selectors()

    for i in range(depth):
        g = ln_gains[i]
        j = i // 2
        if i % 2 == 0:
            x = _rglru_layer(x, g[0:2], a_conv_w[j], row(a_conv_b[j]), bf(a_w_in[j]), row(a_b_in[j]),
                             bf(a_gate_w[j]), a_gate_b[j].reshape(2, D), row(a_lambda[j]),
                             bf(a_w_out[j]), row(a_b_out[j]))
        else:
            r, lw, k, v, a, b, gate, bonus = _rwkv_pre(
                x, g[0:1], b_mu[j], bf(b_w_rkv[j]), row(b_w0[j]), bf(b_w1[j]), bf(b_w2[j]), row(b_a0[j]),
                bf(b_a1[j]), bf(b_a2[j]), bf(b_g1[j]), bf(b_g2[j]), row(b_k_k[j]), row(b_k_a[j]),
                row(b_r_k[j]), e, et)
            y = _rwkv_scan(r, lw, k, v, a, b)
            x = _rwkv_post(x.reshape(B * S, D), y.reshape(B * S, D), bonus.reshape(B * S, D),
                           gate.reshape(B * S, D), g[1:2], row(b_gn_g[j]), row(b_gn_b[j]), bf(b_w_o[j]),
                           e, et).reshape(B, S, D)
        x = _xattn_layer(x, kv, i, g[2:4], bf(c_w_q[i]), bf(c_w_o[i]))
        x = _mlp_layer(x.reshape(B * S, D), g[4:6], bf(m_w_up[i]), bf(m_w_down[i])).reshape(B, S, D)
    return x
```

```python
import functools

import jax
import jax.numpy as jnp
from jax import lax
from jax.experimental import pallas as pl
from jax.experimental.pallas import tpu as pltpu

F32 = jnp.float32
BF16 = jnp.bfloat16

D_MODEL = 1024
RMS_EPS = 1e-6
LRU_HEADS = 4
LRU_BLOCK = D_MODEL // LRU_HEADS
CONV_WIDTH = 4
LRU_C = 8.0
RWKV_HEAD = 64
RWKV_HEADS = D_MODEL // RWKV_HEAD
RWKV_GN_EPS = 64e-5
MEM_HEADS = 4
MEM_HEAD_DIM = D_MODEL // MEM_HEADS
D_FF = 4 * D_MODEL

LANES = 128
SUBLANES = 8
QUAD = 256
N_QUADS = D_MODEL // QUAD
SCAN_CHUNK = 64
SCAN_ROWS = 4
VMEM_LIMIT = 56 * 1024 * 1024

SCAN_DTYPE = BF16


def _cparams(*sem):
    return pltpu.CompilerParams(dimension_semantics=sem, vmem_limit_bytes=VMEM_LIMIT)


def _rms(x, g):
    return x * lax.rsqrt(jnp.mean(x * x, axis=-1, keepdims=True) + RMS_EPS) * g


def _sigmoid(x):
    return 1.0 / (1.0 + jnp.exp(-x))


def _softplus(x):
    return jnp.maximum(x, 0.0) + jnp.log(1.0 + jnp.exp(-jnp.abs(x)))


def _gelu_tanh(x):
    return 0.5 * x * (1.0 + jnp.tanh(0.7978845608028654 * (x + 0.044715 * (x * x * x))))


def _bdot(a, b):
    return jnp.dot(a.astype(BF16), b.astype(BF16), preferred_element_type=F32)


def _const_spec(shape):
    nd = len(shape)
    return pl.BlockSpec(shape, lambda *_: (0,) * nd)


def _memkv_kernel(mem_ref, gn_ref, wkv_ref, out_ref):
    mn = _rms(mem_ref[...], gn_ref[...])
    out_ref[...] = _bdot(mn, wkv_ref[...]).astype(BF16)


def _mem_kv(mem2d, mem_norm, w_kv_bf):
    rows = mem2d.shape[0]
    depth = w_kv_bf.shape[0]
    tm = min(512, rows)
    return pl.pallas_call(
        _memkv_kernel,
        grid=(depth, rows // tm),
        in_specs=[
            pl.BlockSpec((tm, D_MODEL), lambda i, r: (r, 0)),
            _const_spec((1, D_MODEL)),
            pl.BlockSpec((None, D_MODEL, 2 * D_MODEL), lambda i, r: (i, 0, 0)),
        ],
        out_specs=pl.BlockSpec((None, tm, 2 * D_MODEL), lambda i, r: (i, r, 0)),
        out_shape=jax.ShapeDtypeStruct((depth, rows, 2 * D_MODEL), BF16),
        compiler_params=_cparams("arbitrary", "arbitrary"),
        name="mem_kv",
    )(mem2d, mem_norm, w_kv_bf)


def _rglru_kernel(x_ref, g_ref, convw_ref, convb_ref, win_ref, bin_ref, gw_ref, gb_ref, lam_ref,
                  wout_ref, bout_ref, o_ref, ubuf, a_s, b_s, h_s, hcar, *, ts):
    s = pl.program_id(1)

    @pl.when(s == 0)
    def _():
        ubuf[0:SUBLANES, :] = jnp.zeros((SUBLANES, D_MODEL), F32)
        hcar[...] = jnp.zeros((1, D_MODEL), F32)

    x = x_ref[0]
    hn = _rms(x, g_ref[0:1, :])
    proj = _bdot(hn, win_ref[...]) + bin_ref[...]
    yb = _gelu_tanh(proj[:, :D_MODEL])
    ubuf[SUBLANES:SUBLANES + ts, :] = proj[:, D_MODEL:]
    conv = convb_ref[...] + ubuf[pl.ds(SUBLANES - CONV_WIDTH + 1, ts), :] * convw_ref[0:1, :]
    for tap in range(1, CONV_WIDTH):
        conv = conv + ubuf[pl.ds(SUBLANES - CONV_WIDTH + 1 + tap, ts), :] * convw_ref[tap:tap + 1, :]
    ubuf[0:SUBLANES, :] = ubuf[ts:ts + SUBLANES, :]

    conv_bf = conv.astype(BF16)
    r_parts, i_parts = [], []
    for h in range(LRU_HEADS):
        cb = conv_bf[:, h * LRU_BLOCK:(h + 1) * LRU_BLOCK]
        r_parts.append(jnp.dot(cb, gw_ref[0, h], preferred_element_type=F32))
        i_parts.append(jnp.dot(cb, gw_ref[1, h], preferred_element_type=F32))
    r_gate = _sigmoid(jnp.concatenate(r_parts, axis=1) + gb_ref[0:1, :])
    i_gate = _sigmoid(jnp.concatenate(i_parts, axis=1) + gb_ref[1:2, :])
    a = jnp.exp((-LRU_C) * r_gate * _softplus(-lam_ref[...]))
    a_s[...] = a
    b_s[...] = jnp.sqrt(1.0 - a * a) * i_gate * conv

    def body(t, h):
        h = a_s[pl.ds(t, 1), :] * h + b_s[pl.ds(t, 1), :]
        h_s[pl.ds(t, 1), :] = h
        return h

    hcar[...] = lax.fori_loop(0, ts, body, hcar[...], unroll=8)
    out = _bdot(h_s[...] * yb, wout_ref[...]) + bout_ref[...]
    o_ref[0] = x + _rms(out, g_ref[1:2, :])


def _rglru_layer(x, g01, conv_w, conv_b, w_in_bf, b_in, gate_w_bf, gate_b, lam, w_out_bf, b_out):
    B, S, D = x.shape
    ts = min(256, S)
    tile = pl.BlockSpec((1, ts, D), lambda b, s: (b, s, 0))
    return pl.pallas_call(
        functools.partial(_rglru_kernel, ts=ts),
        grid=(B, S // ts),
        in_specs=[
            tile,
            _const_spec((2, D)),
            _const_spec((CONV_WIDTH, D)),
            _const_spec((1, D)),
            _const_spec((D, 2 * D)),
            _const_spec((1, 2 * D)),
            _const_spec((2, LRU_HEADS, LRU_BLOCK, LRU_BLOCK)),
            _const_spec((2, D)),
            _const_spec((1, D)),
            _const_spec((D, D)),
            _const_spec((1, D)),
        ],
        out_specs=tile,
        out_shape=jax.ShapeDtypeStruct((B, S, D), F32),
        scratch_shapes=[
            pltpu.VMEM((ts + SUBLANES, D), F32),
            pltpu.VMEM((ts, D), F32),
            pltpu.VMEM((ts, D), F32),
            pltpu.VMEM((ts, D), F32),
            pltpu.VMEM((1, D), F32),
        ],
        compiler_params=_cparams("arbitrary", "arbitrary"),
        name="rglru_layer",
    )(x, g01, conv_w, conv_b, w_in_bf, b_in, gate_w_bf, gate_b, lam, w_out_bf, b_out)


def _xattn_kernel(x_ref, k_ref, v_ref, g_ref, wq_ref, wo_ref, o_ref):
    x = x_ref[0]
    hn = _rms(x, g_ref[0:1, :])
    q = (_bdot(hn, wq_ref[...]) * (MEM_HEAD_DIM ** -0.5)).astype(BF16)
    outs = []
    for h in range(MEM_HEADS):
        sl = slice(h * MEM_HEAD_DIM, (h + 1) * MEM_HEAD_DIM)
        sc = lax.dot_general(q[:, sl], k_ref[:, sl], (((1,), (1,)), ((), ())),
                             preferred_element_type=F32)
        p = jnp.exp(sc - jnp.max(sc, axis=-1, keepdims=True))
        l = jnp.sum(p, axis=-1, keepdims=True)
        outs.append(jnp.dot(p.astype(BF16), v_ref[:, sl], preferred_element_type=F32) / l)
    c = _bdot(jnp.concatenate(outs, axis=1), wo_ref[...])
    o_ref[0] = x + _rms(c, g_ref[1:2, :])


def _xattn_layer(x, kv, layer, g23, w_q_bf, w_o_bf):
    B, S, D = x.shape
    M = kv.shape[2]
    tq = min(512, S)
    tile = pl.BlockSpec((1, tq, D), lambda b, s: (b, s, 0))
    return pl.pallas_call(
        _xattn_kernel,
        grid=(B, S // tq),
        in_specs=[
            tile,
            pl.BlockSpec((None, None, M, D), lambda b, s: (layer, b, 0, 0)),
            pl.BlockSpec((None, None, M, D), lambda b, s: (layer, b, 0, 1)),
            _const_spec((2, D)),
            _const_spec((D, D)),
            _const_spec((D, D)),
        ],
        out_specs=tile,
        out_shape=jax.ShapeDtypeStruct((B, S, D), F32),
        compiler_params=_cparams("arbitrary", "arbitrary"),
        name="xattn_layer",
    )(x, kv, kv, g23, w_q_bf, w_o_bf)


def _mlp_kernel(x_ref, g_ref, wup_ref, wdn_ref, o_ref):
    x = x_ref[...]
    hn = _rms(x, g_ref[0:1, :]).astype(BF16)
    acc = None
    for c in range(D_FF // D_MODEL):
        sl = slice(c * D_MODEL, (c + 1) * D_MODEL)
        h = jnp.maximum(jnp.dot(hn, wup_ref[:, sl], preferred_element_type=F32), 0.0)
        part = jnp.dot((h * h).astype(BF16), wdn_ref[sl, :], preferred_element_type=F32)
        acc = part if acc is None else acc + part
    o_ref[...] = x + _rms(acc, g_ref[1:2, :])


def _mlp_layer(x2d, g45, w_up_bf, w_down_bf):
    T, D = x2d.shape
    tm = min(512, T)
    tile = pl.BlockSpec((tm, D), lambda r: (r, 0))
    return pl.pallas_call(
        _mlp_kernel,
        grid=(T // tm,),
        in_specs=[
            tile,
            _const_spec((2, D)),
            pl.BlockSpec((D, D_FF), lambda r: (0, 0), pipeline_mode=pl.Buffered(1)),
            pl.BlockSpec((D_FF, D), lambda r: (0, 0), pipeline_mode=pl.Buffered(1)),
        ],
        out_specs=tile,
        out_shape=jax.ShapeDtypeStruct((T, D), F32),
        compiler_params=_cparams("arbitrary"),
        name="mlp_layer",
    )(x2d, g45, w_up_bf, w_down_bf)


def _head_sum(x, e_ref, et_ref):
    hi = x.astype(BF16)
    lo = (x - hi.astype(F32)).astype(BF16)
    s = (jnp.dot(hi, e_ref[...], preferred_element_type=F32)
         + jnp.dot(lo, e_ref[...], preferred_element_type=F32))
    shi = s.astype(BF16)
    slo = (s - shi.astype(F32)).astype(BF16)
    return (jnp.dot(shi, et_ref[...], preferred_element_type=F32)
            + jnp.dot(slo, et_ref[...], preferred_element_type=F32))


def _rwkv_pre_kernel(x_ref, g_ref, mu_ref, wrkv_ref, w0_ref, w1_ref, w2_ref, a0_ref, a1_ref, a2_ref,
                     g1_ref, g2_ref, kk_ref, ka_ref, rk_ref, e_ref, et_ref,
                     r_out, c_out, k_out, v_out, a_out, b_out, g_out, bonus_out, hbuf, *, ts):
    s = pl.program_id(1)

    @pl.when(s == 0)
    def _():
        hbuf[0:SUBLANES, :] = jnp.zeros((SUBLANES, D_MODEL), F32)

    hn = _rms(x_ref[0], g_ref[...])
    hbuf[SUBLANES:SUBLANES + ts, :] = hn
    xx = hbuf[pl.ds(SUBLANES - 1, ts), :] - hn
    hbuf[0:SUBLANES, :] = hbuf[ts:ts + SUBLANES, :]

    def mix(i):
        return hn + xx * mu_ref[i:i + 1, :]

    r = _bdot(mix(0), wrkv_ref[0])
    k = _bdot(mix(2), wrkv_ref[1])
    v = _bdot(mix(3), wrkv_ref[2])
    z = w0_ref[...] + _bdot(jnp.tanh(_bdot(mix(1), w1_ref[...])), w2_ref[...])
    lw = -jnp.exp(-_softplus(-z) - 0.5)
    alr = _sigmoid(a0_ref[...] + _bdot(_bdot(mix(4), a1_ref[...]), a2_ref[...]))
    g = _bdot(_sigmoid(_bdot(mix(5), g1_ref[...])), g2_ref[...])

    kk = k * kk_ref[...]
    kk = kk / jnp.maximum(jnp.sqrt(_head_sum(kk * kk, e_ref, et_ref)), 1e-12)
    k = k * (1.0 + (alr - 1.0) * ka_ref[...])
    L = min(SCAN_CHUNK, ts)
    ti = lax.broadcasted_iota(jnp.int32, (ts, ts), 0)
    si = lax.broadcasted_iota(jnp.int32, (ts, ts), 1)
    tri = ((ti // L == si // L) & (ti >= si)).astype(BF16)
    lw_hi = lw.astype(BF16)
    lw_lo = (lw - lw_hi.astype(F32)).astype(BF16)
    r_out[0] = r
    c_out[0] = (jnp.dot(tri, lw_hi, preferred_element_type=F32)
                + jnp.dot(tri, lw_lo, preferred_element_type=F32))
    k_out[0] = k
    v_out[0] = v
    a_out[0] = -kk
    b_out[0] = kk * alr
    g_out[0] = g
    bonus_out[0] = _head_sum(r * k * rk_ref[...], e_ref, et_ref) * v


def _rwkv_pre(x, g0, mu, w_rkv_bf, w0, w1_bf, w2_bf, a0, a1_bf, a2_bf, g1_bf, g2_bf, k_k, k_a, r_k, e, et):
    B, S, D = x.shape
    ts = min(256, S)
    tile = pl.BlockSpec((1, ts, D), lambda b, s: (b, s, 0))
    consts = [g0, mu, w_rkv_bf, w0, w1_bf, w2_bf, a0, a1_bf, a2_bf, g1_bf, g2_bf, k_k, k_a, r_k, e, et]
    return pl.pallas_call(
        functools.partial(_rwkv_pre_kernel, ts=ts),
        grid=(B, S // ts),
        in_specs=[tile] + [_const_spec(c.shape) for c in consts],
        out_specs=[tile] * 8,
        out_shape=[jax.ShapeDtypeStruct((B, S, D), F32)] * 8,
        scratch_shapes=[pltpu.VMEM((ts + SUBLANES, D), F32)],
        compiler_params=_cparams("arbitrary", "arbitrary"),
        name="rwkv_pre",
    )(x, *consts)


_BNN = (((2,), (1,)), ((0,), (0,)))
_BNT = (((2,), (2,)), ((0,), (0,)))
_BTN = (((1,), (1,)), ((0,), (0,)))


def _qdot(a, b, dims=_BNN):
    return lax.dot_general(a.astype(SCAN_DTYPE), b.astype(SCAN_DTYPE), dims, preferred_element_type=F32)


def _rwkv_scan_kernel(r_ref, c_ref, k_ref, v_ref, a_ref, b_ref, y_ref, state, *, L):
    @pl.when(pl.program_id(1) == 0)
    def _():
        state[...] = jnp.zeros(state.shape, F32)

    G = c_ref.shape[0]
    c = c_ref[...]
    first_row = lax.broadcasted_iota(jnp.int32, c.shape, 1) == 0
    c_prev = jnp.where(first_row, 0.0, pltpu.roll(c, shift=1, axis=1))
    c_end = c[:, L - 1:L, :]
    gam = jnp.exp(c)
    gam_inv = jnp.exp(-c)
    gam_end = jnp.exp(c_end - c)
    r_t = r_ref[...] * gam
    a_t = a_ref[...] * jnp.exp(c_prev)
    b_t = b_ref[...] * gam_inv
    k_t = k_ref[...] * gam_inv
    b_h = b_ref[...] * gam_end
    k_h = k_ref[...] * gam_end
    gam_last = jnp.exp(c_end)

    hq = QUAD // RWKV_HEAD
    t_idx = lax.broadcasted_iota(jnp.int32, (L, hq * L), 0)
    s_idx = lax.broadcasted_iota(jnp.int32, (L, hq * L), 1) % L
    m_strict = t_idx > s_idx
    m_incl = t_idx >= s_idx
    eye = (t_idx == s_idx).astype(F32)
    m_off = []
    m = 1
    while m < L:
        m_off.append(((t_idx // (2 * m)) == (s_idx // (2 * m))) & ((t_idx % (2 * m)) >= m)
                     & ((s_idx % (2 * m)) < m))
        m *= 2
    qrow = lax.broadcasted_iota(jnp.int32, (QUAD, QUAD), 0)
    qcol = lax.broadcasted_iota(jnp.int32, (QUAD, QUAD), 1)
    m_state = (qrow // RWKV_HEAD) == (qcol // RWKV_HEAD)
    lane_head = lax.broadcasted_iota(jnp.int32, (L, QUAD), 1) // RWKV_HEAD
    lane_blk = lax.broadcasted_iota(jnp.int32, (L, hq * L), 1) // L

    def quads(z):
        return jnp.stack([z[g][:, q * QUAD:(q + 1) * QUAD] for g in range(G) for q in range(N_QUADS)], axis=0)

    def bdiag(z, blk):
        z = z.astype(SCAN_DTYPE)
        zero = jnp.zeros_like(z)
        return jnp.concatenate([jnp.where(blk == h, z, zero) for h in range(hq)], axis=1)

    a_q, r_q, v_q, b_q, k_q = quads(a_t), quads(r_t), quads(v_ref[...]), quads(b_t), quads(k_t)
    ar = jnp.concatenate([a_q, r_q], axis=1)
    g_b = _qdot(ar, bdiag(b_q, lane_head), _BNT)
    g_k = _qdot(ar, bdiag(k_q, lane_head), _BNT)
    n_ab = g_b[:, :L]
    a_rb = jnp.where(m_incl, g_b[:, L:], 0.0)
    a_ak = jnp.where(m_strict, g_k[:, :L], 0.0)
    a_rk = jnp.where(m_incl, g_k[:, L:], 0.0)
    tinv = eye + jnp.where(m_off[0], n_ab, 0.0)
    for m_lvl in m_off[1:]:
        n_off = jnp.where(m_lvl, n_ab, 0.0)
        tinv = tinv + _qdot(tinv, bdiag(_qdot(n_off, bdiag(tinv, lane_blk)), lane_blk))
    s0 = state[...]
    xs = _qdot(ar, s0, _BNT)
    vs = _qdot(jnp.concatenate([a_ak, a_rk], axis=1), bdiag(v_q, lane_head))
    u = _qdot(tinv, bdiag(xs[:, :L] + vs[:, :L], lane_head))
    y = xs[:, L:] + vs[:, L:] + _qdot(a_rb, bdiag(u, lane_head))
    upd = _qdot(jnp.concatenate([u, v_q], axis=1), jnp.concatenate([quads(b_h), quads(k_h)], axis=1), _BTN)
    state[...] = s0 * quads(gam_last) + jnp.where(m_state, upd, 0.0)
    for g in range(G):
        y_ref[g] = jnp.concatenate([y[g * N_QUADS + q] for q in range(N_QUADS)], axis=1)


def _rwkv_scan(r, c, k, v, a, b):
    B, S, D = r.shape
    L = min(SCAN_CHUNK, S)
    G = SCAN_ROWS if B % SCAN_ROWS == 0 else 1
    tile = pl.BlockSpec((G, L, D), lambda bb, c: (bb, c, 0))
    return pl.pallas_call(
        functools.partial(_rwkv_scan_kernel, L=L),
        grid=(B // G, S // L),
        in_specs=[tile] * 6,
        out_specs=tile,
        out_shape=jax.ShapeDtypeStruct((B, S, D), F32),
        scratch_shapes=[pltpu.VMEM((G * N_QUADS, QUAD, QUAD), F32)],
        compiler_params=_cparams("arbitrary", "arbitrary"),
        name="rwkv_scan",
    )(r, c, k, v, a, b)


def _rwkv_post_kernel(x_ref, y_ref, bonus_ref, gate_ref, g_ref, gng_ref, gnb_ref, wo_ref, e_ref, et_ref,
                      o_ref):
    y = y_ref[...]
    mean = _head_sum(y, e_ref, et_ref) * (1.0 / RWKV_HEAD)
    d = y - mean
    var = _head_sum(d * d, e_ref, et_ref) * (1.0 / RWKV_HEAD)
    yn = d * lax.rsqrt(var + RWKV_GN_EPS) * gng_ref[...] + gnb_ref[...]
    t = _bdot((yn + bonus_ref[...]) * gate_ref[...], wo_ref[...])
    o_ref[...] = x_ref[...] + _rms(t, g_ref[...])


def _rwkv_post(x2d, y2d, bonus2d, gate2d, g1, gn_g, gn_b, w_o_bf, e, et):
    T, D = x2d.shape
    tm = min(512, T)
    tile = pl.BlockSpec((tm, D), lambda r: (r, 0))
    consts = [g1, gn_g, gn_b, w_o_bf, e, et]
    return pl.pallas_call(
        _rwkv_post_kernel,
        grid=(T // tm,),
        in_specs=[tile] * 4 + [_const_spec(c.shape) for c in consts],
        out_specs=tile,
        out_shape=jax.ShapeDtypeStruct((T, D), F32),
        compiler_params=_cparams("arbitrary"),
        name="rwkv_post",
    )(x2d, y2d, bonus2d, gate2d, *consts)


def _head_selectors():
    lane = jnp.arange(D_MODEL)[:, None] // RWKV_HEAD
    e = (lane == jnp.arange(LANES)[None, :]).astype(BF16)
    return e, e.T


def kernel(x, mem, ln_gains, mem_norm, a_conv_w, a_conv_b, a_w_in, a_b_in, a_gate_w, a_gate_b, a_lambda, a_w_out, a_b_out, b_mu, b_w_rkv, b_w0, b_w1, b_w2, b_a0, b_a1, b_a2, b_g1, b_g2, b_k_k, b_k_a, b_r_k, b_gn_g, b_gn_b, b_w_o, c_w_q, c_w_kv, c_w_o, m_w_up, m_w_down):
    B, S, D = x.shape
    M = mem.shape[1]
    depth = ln_gains.shape[0]
    assert D == D_MODEL and depth == 2
    bf = lambda w: w.astype(BF16)
    row = lambda p: p.reshape(1, -1)

    kv = _mem_kv(mem.reshape(B * M, D), row(mem_norm), bf(c_w_kv)).reshape(depth, B, M, 2 * D)
    e, et = _head_selectors()

    for i in range(depth):
        g = ln_gains[i]
        j = i // 2
        if i % 2 == 0:
            x = _rglru_layer(x, g[0:2], a_conv_w[j], row(a_conv_b[j]), bf(a_w_in[j]), row(a_b_in[j]),
                             bf(a_gate_w[j]), a_gate_b[j].reshape(2, D), row(a_lambda[j]),
                             bf(a_w_out[j]), row(a_b_out[j]))
        else:
            r, c, k, v, a, b, gate, bonus = _rwkv_pre(
                x, g[0:1], b_mu[j], bf(b_w_rkv[j]), row(b_w0[j]), bf(b_w1[j]), bf(b_w2[j]), row(b_a0[j]),
                bf(b_a1[j]), bf(b_a2[j]), bf(b_g1[j]), bf(b_g2[j]), row(b_k_k[j]), row(b_k_a[j]),
                row(b_r_k[j]), e, et)
            y = _rwkv_scan(r, c, k, v, a, b)
            x = _rwkv_post(x.reshape(B * S, D), y.reshape(B * S, D), bonus.reshape(B * S, D),
                           gate.reshape(B * S, D), g[1:2], row(b_gn_g[j]), row(b_gn_b[j]), bf(b_w_o[j]),
                           e, et).reshape(B, S, D)
        x = _xattn_layer(x, kv, i, g[2:4], bf(c_w_q[i]), bf(c_w_o[i]))
        x = _mlp_layer(x.reshape(B * S, D), g[4:6], bf(m_w_up[i]), bf(m_w_down[i])).reshape(B, S, D)
    return x
```

```python
import functools

import jax
import jax.numpy as jnp
from jax import lax
from jax.experimental import pallas as pl
from jax.experimental.pallas import tpu as pltpu

F32 = jnp.float32
BF16 = jnp.bfloat16

D_MODEL = 1024
RMS_EPS = 1e-6
LRU_HEADS = 4
LRU_BLOCK = D_MODEL // LRU_HEADS
CONV_WIDTH = 4
LRU_C = 8.0
RWKV_HEAD = 64
RWKV_HEADS = D_MODEL // RWKV_HEAD
RWKV_GN_EPS = 64e-5
MEM_HEADS = 4
MEM_HEAD_DIM = D_MODEL // MEM_HEADS
D_FF = 4 * D_MODEL

LANES = 128
SUBLANES = 8
QUAD = 256
N_QUADS = D_MODEL // QUAD
SCAN_CHUNK = 64
SCAN_ROWS = 4
VMEM_LIMIT = 56 * 1024 * 1024

SCAN_DTYPE = BF16


def _cparams(*sem):
    return pltpu.CompilerParams(dimension_semantics=sem, vmem_limit_bytes=VMEM_LIMIT)


def _rms(x, g):
    return x * lax.rsqrt(jnp.mean(x * x, axis=-1, keepdims=True) + RMS_EPS) * g


def _sigmoid(x):
    return 1.0 / (1.0 + jnp.exp(-x))


def _softplus(x):
    return jnp.maximum(x, 0.0) + jnp.log(1.0 + jnp.exp(-jnp.abs(x)))


def _gelu_tanh(x):
    return 0.5 * x * (1.0 + jnp.tanh(0.7978845608028654 * (x + 0.044715 * (x * x * x))))


def _bdot(a, b):
    return jnp.dot(a.astype(BF16), b.astype(BF16), preferred_element_type=F32)


def _shift_rows(x, tail, k):
    rolled = pltpu.roll(x, shift=k, axis=0)
    row = lax.broadcasted_iota(jnp.int32, (SUBLANES, x.shape[1]), 0)
    head = jnp.where(row < k, pltpu.roll(tail, shift=k, axis=0), rolled[:SUBLANES])
    return jnp.concatenate([head, rolled[SUBLANES:]], axis=0)


def _const_spec(shape):
    nd = len(shape)
    return pl.BlockSpec(shape, lambda *_: (0,) * nd)


def _memkv_kernel(mem_ref, gn_ref, wkv_ref, out_ref):
    mn = _rms(mem_ref[...], gn_ref[...])
    out_ref[...] = _bdot(mn, wkv_ref[...]).astype(BF16)


def _mem_kv(mem2d, mem_norm, w_kv_bf):
    rows = mem2d.shape[0]
    depth = w_kv_bf.shape[0]
    tm = min(512, rows)
    return pl.pallas_call(
        _memkv_kernel,
        grid=(depth, rows // tm),
        in_specs=[
            pl.BlockSpec((tm, D_MODEL), lambda i, r: (r, 0)),
            _const_spec((1, D_MODEL)),
            pl.BlockSpec((None, D_MODEL, 2 * D_MODEL), lambda i, r: (i, 0, 0)),
        ],
        out_specs=pl.BlockSpec((None, tm, 2 * D_MODEL), lambda i, r: (i, r, 0)),
        out_shape=jax.ShapeDtypeStruct((depth, rows, 2 * D_MODEL), BF16),
        compiler_params=_cparams("arbitrary", "arbitrary"),
        name="mem_kv",
    )(mem2d, mem_norm, w_kv_bf)


def _rglru_kernel(x_ref, g_ref, convw_ref, convb_ref, win_ref, bin_ref, gw_ref, gb_ref, lam_ref,
                  wout_ref, bout_ref, o_ref, ubuf, a_s, b_s, h_s, hcar, *, ts):
    s = pl.program_id(1)

    @pl.when(s == 0)
    def _():
        ubuf[...] = jnp.zeros((SUBLANES, D_MODEL), F32)
        hcar[...] = jnp.zeros((1, D_MODEL), F32)

    x = x_ref[0]
    hn = _rms(x, g_ref[0:1, :])
    proj = _bdot(hn, win_ref[...]) + bin_ref[...]
    yb = _gelu_tanh(proj[:, :D_MODEL])
    u = proj[:, D_MODEL:]
    tail = ubuf[...]
    conv = convb_ref[...] + u * convw_ref[CONV_WIDTH - 1:CONV_WIDTH, :]
    for tap in range(CONV_WIDTH - 1):
        conv = conv + _shift_rows(u, tail, CONV_WIDTH - 1 - tap) * convw_ref[tap:tap + 1, :]
    ubuf[...] = u[ts - SUBLANES:, :]

    conv_bf = conv.astype(BF16)
    r_parts, i_parts = [], []
    for h in range(LRU_HEADS):
        cb = conv_bf[:, h * LRU_BLOCK:(h + 1) * LRU_BLOCK]
        r_parts.append(jnp.dot(cb, gw_ref[0, h], preferred_element_type=F32))
        i_parts.append(jnp.dot(cb, gw_ref[1, h], preferred_element_type=F32))
    r_gate = _sigmoid(jnp.concatenate(r_parts, axis=1) + gb_ref[0:1, :])
    i_gate = _sigmoid(jnp.concatenate(i_parts, axis=1) + gb_ref[1:2, :])
    a = jnp.exp((-LRU_C) * r_gate * _softplus(-lam_ref[...]))
    a_s[...] = a
    b_s[...] = jnp.sqrt(1.0 - a * a) * i_gate * conv

    def body(t, h):
        h = a_s[pl.ds(t, 1), :] * h + b_s[pl.ds(t, 1), :]
        h_s[pl.ds(t, 1), :] = h
        return h

    hcar[...] = lax.fori_loop(0, ts, body, hcar[...], unroll=8)
    out = _bdot(h_s[...] * yb, wout_ref[...]) + bout_ref[...]
    o_ref[0] = x + _rms(out, g_ref[1:2, :])


def _rglru_layer(x, g01, conv_w, conv_b, w_in_bf, b_in, gate_w_bf, gate_b, lam, w_out_bf, b_out):
    B, S, D = x.shape
    ts = min(256, S)
    tile = pl.BlockSpec((1, ts, D), lambda b, s: (b, s, 0))
    return pl.pallas_call(
        functools.partial(_rglru_kernel, ts=ts),
        grid=(B, S // ts),
        in_specs=[
            tile,
            _const_spec((2, D)),
            _const_spec((CONV_WIDTH, D)),
            _const_spec((1, D)),
            _const_spec((D, 2 * D)),
            _const_spec((1, 2 * D)),
            _const_spec((2, LRU_HEADS, LRU_BLOCK, LRU_BLOCK)),
            _const_spec((2, D)),
            _const_spec((1, D)),
            _const_spec((D, D)),
            _const_spec((1, D)),
        ],
        out_specs=tile,
        out_shape=jax.ShapeDtypeStruct((B, S, D), F32),
        scratch_shapes=[
            pltpu.VMEM((SUBLANES, D), F32),
            pltpu.VMEM((ts, D), F32),
            pltpu.VMEM((ts, D), F32),
            pltpu.VMEM((ts, D), F32),
            pltpu.VMEM((1, D), F32),
        ],
        compiler_params=_cparams("arbitrary", "arbitrary"),
        name="rglru_layer",
    )(x, g01, conv_w, conv_b, w_in_bf, b_in, gate_w_bf, gate_b, lam, w_out_bf, b_out)


def _xattn_kernel(x_ref, k_ref, v_ref, g_ref, wq_ref, wo_ref, o_ref):
    x = x_ref[0]
    hn = _rms(x, g_ref[0:1, :])
    q = (_bdot(hn, wq_ref[...]) * (MEM_HEAD_DIM ** -0.5)).astype(BF16)
    outs = []
    for h in range(MEM_HEADS):
        sl = slice(h * MEM_HEAD_DIM, (h + 1) * MEM_HEAD_DIM)
        sc = lax.dot_general(q[:, sl], k_ref[:, sl], (((1,), (1,)), ((), ())),
                             preferred_element_type=F32)
        p = jnp.exp(sc - jnp.max(sc, axis=-1, keepdims=True))
        l = jnp.sum(p, axis=-1, keepdims=True)
        outs.append(jnp.dot(p.astype(BF16), v_ref[:, sl], preferred_element_type=F32) / l)
    c = _bdot(jnp.concatenate(outs, axis=1), wo_ref[...])
    o_ref[0] = x + _rms(c, g_ref[1:2, :])


def _xattn_layer(x, kv, layer, g23, w_q_bf, w_o_bf):
    B, S, D = x.shape
    M = kv.shape[2]
    tq = min(512, S)
    tile = pl.BlockSpec((1, tq, D), lambda b, s: (b, s, 0))
    return pl.pallas_call(
        _xattn_kernel,
        grid=(B, S // tq),
        in_specs=[
            tile,
            pl.BlockSpec((None, None, M, D), lambda b, s: (layer, b, 0, 0)),
            pl.BlockSpec((None, None, M, D), lambda b, s: (layer, b, 0, 1)),
            _const_spec((2, D)),
            _const_spec((D, D)),
            _const_spec((D, D)),
        ],
        out_specs=tile,
        out_shape=jax.ShapeDtypeStruct((B, S, D), F32),
        compiler_params=_cparams("arbitrary", "arbitrary"),
        name="xattn_layer",
    )(x, kv, kv, g23, w_q_bf, w_o_bf)


def _mlp_kernel(x_ref, g_ref, wup_ref, wdn_ref, o_ref):
    x = x_ref[...]
    hn = _rms(x, g_ref[0:1, :]).astype(BF16)
    acc = None
    for c in range(D_FF // D_MODEL):
        sl = slice(c * D_MODEL, (c + 1) * D_MODEL)
        h = jnp.maximum(jnp.dot(hn, wup_ref[:, sl], preferred_element_type=F32), 0.0)
        part = jnp.dot((h * h).astype(BF16), wdn_ref[sl, :], preferred_element_type=F32)
        acc = part if acc is None else acc + part
    o_ref[...] = x + _rms(acc, g_ref[1:2, :])


def _mlp_layer(x2d, g45, w_up_bf, w_down_bf):
    T, D = x2d.shape
    tm = min(512, T)
    tile = pl.BlockSpec((tm, D), lambda r: (r, 0))
    return pl.pallas_call(
        _mlp_kernel,
        grid=(T // tm,),
        in_specs=[
            tile,
            _const_spec((2, D)),
            pl.BlockSpec((D, D_FF), lambda r: (0, 0), pipeline_mode=pl.Buffered(1)),
            pl.BlockSpec((D_FF, D), lambda r: (0, 0), pipeline_mode=pl.Buffered(1)),
        ],
        out_specs=tile,
        out_shape=jax.ShapeDtypeStruct((T, D), F32),
        compiler_params=_cparams("arbitrary"),
        name="mlp_layer",
    )(x2d, g45, w_up_bf, w_down_bf)


def _head_sum(x, e_ref, exact=True):
    hi = x.astype(BF16)
    lo = (x - hi.astype(F32)).astype(BF16) if exact else None
    e = e_ref[...]
    parts = []
    for q in range(N_QUADS):
        sl = slice(q * QUAD, (q + 1) * QUAD)
        part = jnp.dot(hi[:, sl], e, preferred_element_type=F32)
        if exact:
            part = part + jnp.dot(lo[:, sl], e, preferred_element_type=F32)
        parts.append(part)
    return jnp.concatenate(parts, axis=1)


def _rwkv_pre_kernel(x_ref, g_ref, mu_ref, wrkv_ref, w0_ref, w1_ref, w2_ref, a0_ref, a1_ref, a2_ref,
                     g1_ref, g2_ref, kk_ref, ka_ref, rk_ref, e_ref,
                     r_out, c_out, k_out, v_out, a_out, b_out, g_out, bonus_out, hbuf, *, ts):
    s = pl.program_id(1)

    @pl.when(s == 0)
    def _():
        hbuf[...] = jnp.zeros((SUBLANES, D_MODEL), F32)

    hn = _rms(x_ref[0], g_ref[...])
    xx = (_shift_rows(hn, hbuf[...], 1) - hn).astype(BF16)
    hbuf[...] = hn[ts - SUBLANES:, :]
    hn_bf = hn.astype(BF16)
    mu_bf = mu_ref[...].astype(BF16)

    def mix(i):
        return hn_bf + xx * mu_bf[i:i + 1, :]

    r = _bdot(mix(0), wrkv_ref[0])
    k = _bdot(mix(2), wrkv_ref[1])
    v = _bdot(mix(3), wrkv_ref[2])
    z = w0_ref[...] + _bdot(jnp.tanh(_bdot(mix(1), w1_ref[...])), w2_ref[...])
    lw = -jnp.exp(-_softplus(-z) - 0.5)
    alr = _sigmoid(a0_ref[...] + _bdot(_bdot(mix(4), a1_ref[...]), a2_ref[...]))
    g = _bdot(_sigmoid(_bdot(mix(5), g1_ref[...])), g2_ref[...])

    kk = k * kk_ref[...]
    kk = kk / jnp.maximum(jnp.sqrt(_head_sum(kk * kk, e_ref, exact=False)), 1e-12)
    k = k * (1.0 + (alr - 1.0) * ka_ref[...])
    L = min(SCAN_CHUNK, ts)
    ti = lax.broadcasted_iota(jnp.int32, (ts, ts), 0)
    si = lax.broadcasted_iota(jnp.int32, (ts, ts), 1)
    tri = ((ti // L == si // L) & (ti >= si)).astype(BF16)
    lw_hi = lw.astype(BF16)
    lw_lo = (lw - lw_hi.astype(F32)).astype(BF16)
    r_out[0] = r
    c_out[0] = (jnp.dot(tri, lw_hi, preferred_element_type=F32)
                + jnp.dot(tri, lw_lo, preferred_element_type=F32))
    k_out[0] = k
    v_out[0] = v
    a_out[0] = -kk
    b_out[0] = kk * alr
    g_out[0] = g
    bonus_out[0] = _head_sum(r * k * rk_ref[...], e_ref, exact=False) * v


def _rwkv_pre(x, g0, mu, w_rkv_bf, w0, w1_bf, w2_bf, a0, a1_bf, a2_bf, g1_bf, g2_bf, k_k, k_a, r_k, e):
    B, S, D = x.shape
    ts = min(256, S)
    tile = pl.BlockSpec((1, ts, D), lambda b, s: (b, s, 0))
    consts = [g0, mu, w_rkv_bf, w0, w1_bf, w2_bf, a0, a1_bf, a2_bf, g1_bf, g2_bf, k_k, k_a, r_k, e]
    return pl.pallas_call(
        functools.partial(_rwkv_pre_kernel, ts=ts),
        grid=(B, S // ts),
        in_specs=[tile] + [_const_spec(c.shape) for c in consts],
        out_specs=[tile] * 8,
        out_shape=[jax.ShapeDtypeStruct((B, S, D), F32)] * 8,
        scratch_shapes=[pltpu.VMEM((SUBLANES, D), F32)],
        compiler_params=_cparams("arbitrary", "arbitrary"),
        name="rwkv_pre",
    )(x, *consts)


_BNN = (((2,), (1,)), ((0,), (0,)))
_BNT = (((2,), (2,)), ((0,), (0,)))
_BTN = (((1,), (1,)), ((0,), (0,)))


def _qdot(a, b, dims=_BNN):
    return lax.dot_general(a.astype(SCAN_DTYPE), b.astype(SCAN_DTYPE), dims, preferred_element_type=F32)


def _rwkv_scan_kernel(r_ref, c_ref, k_ref, v_ref, a_ref, b_ref, y_ref, state, *, L):
    @pl.when(pl.program_id(1) == 0)
    def _():
        state[...] = jnp.zeros(state.shape, F32)

    G = c_ref.shape[0]
    c = c_ref[...]
    first_row = lax.broadcasted_iota(jnp.int32, c.shape, 1) == 0
    c_prev = jnp.where(first_row, 0.0, pltpu.roll(c, shift=1, axis=1))
    c_end = c[:, L - 1:L, :]
    gam = jnp.exp(c)
    gam_inv = jnp.exp(-c)
    gam_end = jnp.exp(c_end - c)
    r_t = r_ref[...] * gam
    a_t = a_ref[...] * jnp.exp(c_prev)
    b_t = b_ref[...] * gam_inv
    k_t = k_ref[...] * gam_inv
    b_h = b_ref[...] * gam_end
    k_h = k_ref[...] * gam_end
    gam_last = jnp.exp(c_end)

    hq = QUAD // RWKV_HEAD
    t_idx = lax.broadcasted_iota(jnp.int32, (L, hq * L), 0)
    s_idx = lax.broadcasted_iota(jnp.int32, (L, hq * L), 1) % L
    m_strict = t_idx > s_idx
    m_incl = t_idx >= s_idx
    eye = (t_idx == s_idx).astype(F32)
    m_off = []
    m = 1
    while m < L:
        m_off.append(((t_idx // (2 * m)) == (s_idx // (2 * m))) & ((t_idx % (2 * m)) >= m)
                     & ((s_idx % (2 * m)) < m))
        m *= 2
    qrow = lax.broadcasted_iota(jnp.int32, (QUAD, QUAD), 0)
    qcol = lax.broadcasted_iota(jnp.int32, (QUAD, QUAD), 1)
    m_state = (qrow // RWKV_HEAD) == (qcol // RWKV_HEAD)
    lane_head = lax.broadcasted_iota(jnp.int32, (L, QUAD), 1) // RWKV_HEAD
    lane_blk = lax.broadcasted_iota(jnp.int32, (L, hq * L), 1) // L

    def quads(z):
        return jnp.stack([z[g][:, q * QUAD:(q + 1) * QUAD] for g in range(G) for q in range(N_QUADS)], axis=0)

    def bdiag(z, blk):
        z = z.astype(SCAN_DTYPE)
        zero = jnp.zeros_like(z)
        return jnp.concatenate([jnp.where(blk == h, z, zero) for h in range(hq)], axis=1)

    a_q, r_q, v_q, b_q, k_q = quads(a_t), quads(r_t), quads(v_ref[...]), quads(b_t), quads(k_t)
    ar = jnp.concatenate([a_q, r_q], axis=1)
    g_b = _qdot(ar, bdiag(b_q, lane_head), _BNT)
    g_k = _qdot(ar, bdiag(k_q, lane_head), _BNT)
    n_ab = g_b[:, :L]
    a_rb = jnp.where(m_incl, g_b[:, L:], 0.0)
    a_ak = jnp.where(m_strict, g_k[:, :L], 0.0)
    a_rk = jnp.where(m_incl, g_k[:, L:], 0.0)
    tinv = eye + jnp.where(m_off[0], n_ab, 0.0)
    for m_lvl in m_off[1:]:
        n_off = jnp.where(m_lvl, n_ab, 0.0)
        tinv = tinv + _qdot(tinv, bdiag(_qdot(n_off, bdiag(tinv, lane_blk)), lane_blk))
    s0 = state[...]
    xs = _qdot(ar, s0, _BNT)
    vs = _qdot(jnp.concatenate([a_ak, a_rk], axis=1), bdiag(v_q, lane_head))
    u = _qdot(tinv, bdiag(xs[:, :L] + vs[:, :L], lane_head))
    y = xs[:, L:] + vs[:, L:] + _qdot(a_rb, bdiag(u, lane_head))
    upd = _qdot(jnp.concatenate([u, v_q], axis=1), jnp.concatenate([quads(b_h), quads(k_h)], axis=1), _BTN)
    state[...] = s0 * quads(gam_last) + jnp.where(m_state, upd, 0.0)
    for g in range(G):
        y_ref[g] = jnp.concatenate([y[g * N_QUADS + q] for q in range(N_QUADS)], axis=1)


def _rwkv_scan(r, c, k, v, a, b):
    B, S, D = r.shape
    L = min(SCAN_CHUNK, S)
    G = SCAN_ROWS if B % SCAN_ROWS == 0 else 1
    tile = pl.BlockSpec((G, L, D), lambda bb, c: (bb, c, 0))
    return pl.pallas_call(
        functools.partial(_rwkv_scan_kernel, L=L),
        grid=(B // G, S // L),
        in_specs=[tile] * 6,
        out_specs=tile,
        out_shape=jax.ShapeDtypeStruct((B, S, D), F32),
        scratch_shapes=[pltpu.VMEM((G * N_QUADS, QUAD, QUAD), F32)],
        compiler_params=_cparams("arbitrary", "arbitrary"),
        name="rwkv_scan",
    )(r, c, k, v, a, b)


def _rwkv_post_kernel(x_ref, y_ref, bonus_ref, gate_ref, g_ref, gng_ref, gnb_ref, wo_ref, e_ref, o_ref):
    y = y_ref[...]
    mean = _head_sum(y, e_ref) * (1.0 / RWKV_HEAD)
    d = y - mean
    var = _head_sum(d * d, e_ref, exact=False) * (1.0 / RWKV_HEAD)
    yn = d * lax.rsqrt(var + RWKV_GN_EPS) * gng_ref[...] + gnb_ref[...]
    t = _bdot((yn + bonus_ref[...]) * gate_ref[...], wo_ref[...])
    o_ref[...] = x_ref[...] + _rms(t, g_ref[...])


def _rwkv_post(x2d, y2d, bonus2d, gate2d, g1, gn_g, gn_b, w_o_bf, e):
    T, D = x2d.shape
    tm = min(512, T)
    tile = pl.BlockSpec((tm, D), lambda r: (r, 0))
    consts = [g1, gn_g, gn_b, w_o_bf, e]
    return pl.pallas_call(
        _rwkv_post_kernel,
        grid=(T // tm,),
        in_specs=[tile] * 4 + [_const_spec(c.shape) for c in consts],
        out_specs=tile,
        out_shape=jax.ShapeDtypeStruct((T, D), F32),
        compiler_params=_cparams("arbitrary"),
        name="rwkv_post",
    )(x2d, y2d, bonus2d, gate2d, *consts)


def _head_selector():
    head = jnp.arange(QUAD) // RWKV_HEAD
    return (head[:, None] == head[None, :]).astype(BF16)


def kernel(x, mem, ln_gains, mem_norm, a_conv_w, a_conv_b, a_w_in, a_b_in, a_gate_w, a_gate_b, a_lambda, a_w_out, a_b_out, b_mu, b_w_rkv, b_w0, b_w1, b_w2, b_a0, b_a1, b_a2, b_g1, b_g2, b_k_k, b_k_a, b_r_k, b_gn_g, b_gn_b, b_w_o, c_w_q, c_w_kv, c_w_o, m_w_up, m_w_down):
    B, S, D = x.shape
    M = mem.shape[1]
    depth = ln_gains.shape[0]
    assert D == D_MODEL and depth == 2
    bf = lambda w: w.astype(BF16)
    row = lambda p: p.reshape(1, -1)

    kv = _mem_kv(mem.reshape(B * M, D), row(mem_norm), bf(c_w_kv)).reshape(depth, B, M, 2 * D)
    e = _head_selector()

    for i in range(depth):
        g = ln_gains[i]
        j = i // 2
        if i % 2 == 0:
            x = _rglru_layer(x, g[0:2], a_conv_w[j], row(a_conv_b[j]), bf(a_w_in[j]), row(a_b_in[j]),
                             bf(a_gate_w[j]), a_gate_b[j].reshape(2, D), row(a_lambda[j]),
                             bf(a_w_out[j]), row(a_b_out[j]))
        else:
            r, c, k, v, a, b, gate, bonus = _rwkv_pre(
                x, g[0:1], b_mu[j], bf(b_w_rkv[j]), row(b_w0[j]), bf(b_w1[j]), bf(b_w2[j]), row(b_a0[j]),
                bf(b_a1[j]), bf(b_a2[j]), bf(b_g1[j]), bf(b_g2[j]), row(b_k_k[j]), row(b_k_a[j]),
                row(b_r_k[j]), e)
            y = _rwkv_scan(r, c, k, v, a, b)
            x = _rwkv_post(x.reshape(B * S, D), y.reshape(B * S, D), bonus.reshape(B * S, D),
                           gate.reshape(B * S, D), g[1:2], row(b_gn_g[j]), row(b_gn_b[j]), bf(b_w_o[j]),
                           e).reshape(B, S, D)
        x = _xattn_layer(x, kv, i, g[2:4], bf(c_w_q[i]), bf(c_w_o[i]))
        x = _mlp_layer(x.reshape(B * S, D), g[4:6], bf(m_w_up[i]), bf(m_w_down[i])).reshape(B, S, D)
    return x
```

```python
import functools

import jax
import jax.numpy as jnp
from jax import lax
from jax.experimental import pallas as pl
from jax.experimental.pallas import tpu as pltpu

F32 = jnp.float32
BF16 = jnp.bfloat16

D_MODEL = 1024
RMS_EPS = 1e-6
LRU_HEADS = 4
LRU_BLOCK = D_MODEL // LRU_HEADS
CONV_WIDTH = 4
LRU_C = 8.0
RWKV_HEAD = 64
RWKV_HEADS = D_MODEL // RWKV_HEAD
RWKV_GN_EPS = 64e-5
DECAY_LOG_SCALE = -0.6065306597126334
MEM_HEADS = 4
MEM_HEAD_DIM = D_MODEL // MEM_HEADS
D_FF = 4 * D_MODEL

LANES = 128
SUBLANES = 8
QUAD = 256
N_QUADS = D_MODEL // QUAD
SCAN_CHUNK = 64
SCAN_ROWS = 4
VMEM_LIMIT = 56 * 1024 * 1024

SCAN_DTYPE = BF16


def _cparams(*sem):
    return pltpu.CompilerParams(dimension_semantics=sem, vmem_limit_bytes=VMEM_LIMIT)


def _rms(x, g):
    return x * lax.rsqrt(jnp.mean(x * x, axis=-1, keepdims=True) + RMS_EPS) * g


def _sigmoid(x):
    return 1.0 / (1.0 + jnp.exp(-x))


def _softplus(x):
    return jnp.maximum(x, 0.0) + jnp.log(1.0 + jnp.exp(-jnp.abs(x)))


def _gelu_tanh(x):
    return 0.5 * x * (1.0 + jnp.tanh(0.7978845608028654 * (x + 0.044715 * (x * x * x))))


def _bdot(a, b):
    return jnp.dot(a.astype(BF16), b.astype(BF16), preferred_element_type=F32)


def _shift_rows(x, tail, k):
    rolled = pltpu.roll(x, shift=k, axis=0)
    row = lax.broadcasted_iota(jnp.int32, (SUBLANES, x.shape[1]), 0)
    head = jnp.where(row < k, pltpu.roll(tail, shift=k, axis=0), rolled[:SUBLANES])
    return jnp.concatenate([head, rolled[SUBLANES:]], axis=0)


def _const_spec(shape):
    nd = len(shape)
    return pl.BlockSpec(shape, lambda *_: (0,) * nd)


def _memkv_kernel(mem_ref, gn_ref, wkv_ref, out_ref):
    mn = _rms(mem_ref[...], gn_ref[...])
    out_ref[...] = _bdot(mn, wkv_ref[...]).astype(BF16)


def _mem_kv(mem2d, mem_norm, w_kv_bf):
    rows = mem2d.shape[0]
    depth = w_kv_bf.shape[0]
    tm = min(512, rows)
    return pl.pallas_call(
        _memkv_kernel,
        grid=(depth, rows // tm),
        in_specs=[
            pl.BlockSpec((tm, D_MODEL), lambda i, r: (r, 0)),
            _const_spec((1, D_MODEL)),
            pl.BlockSpec((None, D_MODEL, 2 * D_MODEL), lambda i, r: (i, 0, 0)),
        ],
        out_specs=pl.BlockSpec((None, tm, 2 * D_MODEL), lambda i, r: (i, r, 0)),
        out_shape=jax.ShapeDtypeStruct((depth, rows, 2 * D_MODEL), BF16),
        compiler_params=_cparams("arbitrary", "arbitrary"),
        name="mem_kv",
    )(mem2d, mem_norm, w_kv_bf)


def _rglru_kernel(x_ref, g_ref, convw_ref, convb_ref, win_ref, bin_ref, gw_ref, gb_ref, lam_ref,
                  wout_ref, bout_ref, o_ref, ubuf, a_s, b_s, h_s, hcar, *, ts):
    s = pl.program_id(1)

    @pl.when(s == 0)
    def _():
        ubuf[...] = jnp.zeros((SUBLANES, D_MODEL), F32)
        hcar[...] = jnp.zeros((1, D_MODEL), F32)

    x = x_ref[0]
    hn = _rms(x, g_ref[0:1, :])
    proj = _bdot(hn, win_ref[...]) + bin_ref[...]
    yb = _gelu_tanh(proj[:, :D_MODEL])
    u = proj[:, D_MODEL:]
    tail = ubuf[...]
    conv = convb_ref[...] + u * convw_ref[CONV_WIDTH - 1:CONV_WIDTH, :]
    for tap in range(CONV_WIDTH - 1):
        conv = conv + _shift_rows(u, tail, CONV_WIDTH - 1 - tap) * convw_ref[tap:tap + 1, :]
    ubuf[...] = u[ts - SUBLANES:, :]

    conv_bf = conv.astype(BF16)
    r_parts, i_parts = [], []
    for h in range(LRU_HEADS):
        cb = conv_bf[:, h * LRU_BLOCK:(h + 1) * LRU_BLOCK]
        r_parts.append(jnp.dot(cb, gw_ref[0, h], preferred_element_type=F32))
        i_parts.append(jnp.dot(cb, gw_ref[1, h], preferred_element_type=F32))
    r_gate = _sigmoid(jnp.concatenate(r_parts, axis=1) + gb_ref[0:1, :])
    i_gate = _sigmoid(jnp.concatenate(i_parts, axis=1) + gb_ref[1:2, :])
    a = jnp.exp((-LRU_C) * r_gate * _softplus(-lam_ref[...]))
    a_s[...] = a
    b_s[...] = jnp.sqrt(1.0 - a * a) * i_gate * conv

    def body(t, h):
        h = a_s[pl.ds(t, 1), :] * h + b_s[pl.ds(t, 1), :]
        h_s[pl.ds(t, 1), :] = h
        return h

    hcar[...] = lax.fori_loop(0, ts, body, hcar[...], unroll=8)
    out = _bdot(h_s[...] * yb, wout_ref[...]) + bout_ref[...]
    o_ref[0] = x + _rms(out, g_ref[1:2, :])


def _rglru_layer(x, g01, conv_w, conv_b, w_in_bf, b_in, gate_w_bf, gate_b, lam, w_out_bf, b_out):
    B, S, D = x.shape
    ts = min(256, S)
    tile = pl.BlockSpec((1, ts, D), lambda b, s: (b, s, 0))
    return pl.pallas_call(
        functools.partial(_rglru_kernel, ts=ts),
        grid=(B, S // ts),
        in_specs=[
            tile,
            _const_spec((2, D)),
            _const_spec((CONV_WIDTH, D)),
            _const_spec((1, D)),
            _const_spec((D, 2 * D)),
            _const_spec((1, 2 * D)),
            _const_spec((2, LRU_HEADS, LRU_BLOCK, LRU_BLOCK)),
            _const_spec((2, D)),
            _const_spec((1, D)),
            _const_spec((D, D)),
            _const_spec((1, D)),
        ],
        out_specs=tile,
        out_shape=jax.ShapeDtypeStruct((B, S, D), F32),
        scratch_shapes=[
            pltpu.VMEM((SUBLANES, D), F32),
            pltpu.VMEM((ts, D), F32),
            pltpu.VMEM((ts, D), F32),
            pltpu.VMEM((ts, D), F32),
            pltpu.VMEM((1, D), F32),
        ],
        compiler_params=_cparams("arbitrary", "arbitrary"),
        name="rglru_layer",
    )(x, g01, conv_w, conv_b, w_in_bf, b_in, gate_w_bf, gate_b, lam, w_out_bf, b_out)


def _xattn_kernel(x_ref, k_ref, v_ref, g_ref, wq_ref, wo_ref, o_ref):
    x = x_ref[0]
    hn = _rms(x, g_ref[0:1, :])
    q = (_bdot(hn, wq_ref[...]) * (MEM_HEAD_DIM ** -0.5)).astype(BF16)
    outs = []
    for h in range(MEM_HEADS):
        sl = slice(h * MEM_HEAD_DIM, (h + 1) * MEM_HEAD_DIM)
        sc = lax.dot_general(q[:, sl], k_ref[:, sl], (((1,), (1,)), ((), ())),
                             preferred_element_type=F32)
        p = jnp.exp(sc - jnp.max(sc, axis=-1, keepdims=True))
        l = jnp.sum(p, axis=-1, keepdims=True)
        outs.append(jnp.dot(p.astype(BF16), v_ref[:, sl], preferred_element_type=F32) / l)
    c = _bdot(jnp.concatenate(outs, axis=1), wo_ref[...])
    o_ref[0] = x + _rms(c, g_ref[1:2, :])


def _xattn_layer(x, kv, layer, g23, w_q_bf, w_o_bf):
    B, S, D = x.shape
    M = kv.shape[2]
    tq = min(512, S)
    tile = pl.BlockSpec((1, tq, D), lambda b, s: (b, s, 0))
    return pl.pallas_call(
        _xattn_kernel,
        grid=(B, S // tq),
        in_specs=[
            tile,
            pl.BlockSpec((None, None, M, D), lambda b, s: (layer, b, 0, 0)),
            pl.BlockSpec((None, None, M, D), lambda b, s: (layer, b, 0, 1)),
            _const_spec((2, D)),
            _const_spec((D, D)),
            _const_spec((D, D)),
        ],
        out_specs=tile,
        out_shape=jax.ShapeDtypeStruct((B, S, D), F32),
        compiler_params=_cparams("arbitrary", "arbitrary"),
        name="xattn_layer",
    )(x, kv, kv, g23, w_q_bf, w_o_bf)


def _mlp_kernel(x_ref, g_ref, wup_ref, wdn_ref, o_ref):
    x = x_ref[...]
    hn = _rms(x, g_ref[0:1, :]).astype(BF16)
    acc = None
    for c in range(D_FF // D_MODEL):
        sl = slice(c * D_MODEL, (c + 1) * D_MODEL)
        h = jnp.maximum(jnp.dot(hn, wup_ref[:, sl], preferred_element_type=F32), 0.0)
        part = jnp.dot((h * h).astype(BF16), wdn_ref[sl, :], preferred_element_type=F32)
        acc = part if acc is None else acc + part
    o_ref[...] = x + _rms(acc, g_ref[1:2, :])


def _mlp_layer(x2d, g45, w_up_bf, w_down_bf):
    T, D = x2d.shape
    tm = min(512, T)
    tile = pl.BlockSpec((tm, D), lambda r: (r, 0))
    return pl.pallas_call(
        _mlp_kernel,
        grid=(T // tm,),
        in_specs=[
            tile,
            _const_spec((2, D)),
            pl.BlockSpec((D, D_FF), lambda r: (0, 0), pipeline_mode=pl.Buffered(1)),
            pl.BlockSpec((D_FF, D), lambda r: (0, 0), pipeline_mode=pl.Buffered(1)),
        ],
        out_specs=tile,
        out_shape=jax.ShapeDtypeStruct((T, D), F32),
        compiler_params=_cparams("arbitrary"),
        name="mlp_layer",
    )(x2d, g45, w_up_bf, w_down_bf)


def _head_sum(x, e_ref, exact=True):
    hi = x.astype(BF16)
    lo = (x - hi.astype(F32)).astype(BF16) if exact else None
    e = e_ref[...]
    parts = []
    for q in range(N_QUADS):
        sl = slice(q * QUAD, (q + 1) * QUAD)
        part = jnp.dot(hi[:, sl], e, preferred_element_type=F32)
        if exact:
            part = part + jnp.dot(lo[:, sl], e, preferred_element_type=F32)
        parts.append(part)
    return jnp.concatenate(parts, axis=1)


def _rwkv_pre_kernel(x_ref, g_ref, mu_ref, wrkv_ref, w0_ref, w1_ref, w2_ref, a0_ref, a1_ref, a2_ref,
                     g1_ref, g2_ref, kk_ref, ka_ref, rk_ref, e_ref,
                     r_out, c_out, k_out, v_out, a_out, b_out, g_out, bonus_out, hbuf, *, ts):
    s = pl.program_id(1)

    @pl.when(s == 0)
    def _():
        hbuf[...] = jnp.zeros((SUBLANES, D_MODEL), F32)

    hn = _rms(x_ref[0], g_ref[...])
    xx = (_shift_rows(hn, hbuf[...], 1) - hn).astype(BF16)
    hbuf[...] = hn[ts - SUBLANES:, :]
    hn_bf = hn.astype(BF16)
    mu_bf = mu_ref[...].astype(BF16)

    def mix(i):
        return hn_bf + xx * mu_bf[i:i + 1, :]

    r = _bdot(mix(0), wrkv_ref[0])
    k = _bdot(mix(2), wrkv_ref[1])
    v = _bdot(mix(3), wrkv_ref[2])
    z = w0_ref[...] + _bdot(jnp.tanh(_bdot(mix(1), w1_ref[...])), w2_ref[...])
    lw = DECAY_LOG_SCALE * _sigmoid(z)
    alr = _sigmoid(a0_ref[...] + _bdot(_bdot(mix(4), a1_ref[...]), a2_ref[...]))
    g = _bdot(_sigmoid(_bdot(mix(5), g1_ref[...])), g2_ref[...])

    kk = k * kk_ref[...]
    kk = kk * lax.rsqrt(jnp.maximum(_head_sum(kk * kk, e_ref, exact=False), 1e-24))
    ka = ka_ref[...]
    k = k * ((1.0 - ka) + alr * ka)
    L = min(SCAN_CHUNK, ts)
    ti = lax.broadcasted_iota(jnp.int32, (ts, ts), 0)
    si = lax.broadcasted_iota(jnp.int32, (ts, ts), 1)
    tri = ((ti // L == si // L) & (ti >= si)).astype(BF16)
    lw_hi = lw.astype(BF16)
    lw_lo = (lw - lw_hi.astype(F32)).astype(BF16)
    r_out[0] = r.astype(BF16)
    c_out[0] = (jnp.dot(tri, lw_hi, preferred_element_type=F32)
                + jnp.dot(tri, lw_lo, preferred_element_type=F32))
    k_out[0] = k.astype(BF16)
    v_out[0] = v.astype(BF16)
    a_out[0] = (-kk).astype(BF16)
    b_out[0] = (kk * alr).astype(BF16)
    g_out[0] = g.astype(BF16)
    bonus_out[0] = (_head_sum(r * k * rk_ref[...], e_ref, exact=False) * v).astype(BF16)


def _rwkv_pre(x, g0, mu, w_rkv_bf, w0, w1_bf, w2_bf, a0, a1_bf, a2_bf, g1_bf, g2_bf, k_k, k_a, r_k, e):
    B, S, D = x.shape
    ts = min(256, S)
    tile = pl.BlockSpec((1, ts, D), lambda b, s: (b, s, 0))
    consts = [g0, mu, w_rkv_bf, w0, w1_bf, w2_bf, a0, a1_bf, a2_bf, g1_bf, g2_bf, k_k, k_a, r_k, e]
    return pl.pallas_call(
        functools.partial(_rwkv_pre_kernel, ts=ts),
        grid=(B, S // ts),
        in_specs=[tile] + [_const_spec(c.shape) for c in consts],
        out_specs=[tile] * 8,
        out_shape=[jax.ShapeDtypeStruct((B, S, D), F32 if i == 1 else BF16) for i in range(8)],
        scratch_shapes=[pltpu.VMEM((SUBLANES, D), F32)],
        compiler_params=_cparams("arbitrary", "arbitrary"),
        name="rwkv_pre",
    )(x, *consts)


_BNN = (((2,), (1,)), ((0,), (0,)))
_BNT = (((2,), (2,)), ((0,), (0,)))
_BTN = (((1,), (1,)), ((0,), (0,)))


def _qdot(a, b, dims=_BNN):
    return lax.dot_general(a.astype(SCAN_DTYPE), b.astype(SCAN_DTYPE), dims, preferred_element_type=F32)


def _rwkv_scan_kernel(r_ref, c_ref, k_ref, v_ref, a_ref, b_ref, y_ref, state, *, L):
    @pl.when(pl.program_id(1) == 0)
    def _():
        state[...] = jnp.zeros(state.shape, F32)

    G = c_ref.shape[0]
    c = c_ref[...]
    first_row = lax.broadcasted_iota(jnp.int32, c.shape, 1) == 0
    c_prev = jnp.where(first_row, 0.0, pltpu.roll(c, shift=1, axis=1))
    c_end = c[:, L - 1:L, :]
    gam = jnp.exp(c)
    gam_inv = jnp.exp(-c)
    gam_end = jnp.exp(c_end - c)
    b_in = b_ref[...].astype(F32)
    k_in = k_ref[...].astype(F32)
    r_t = r_ref[...].astype(F32) * gam
    a_t = a_ref[...].astype(F32) * jnp.exp(c_prev)
    b_t = b_in * gam_inv
    k_t = k_in * gam_inv
    b_h = b_in * gam_end
    k_h = k_in * gam_end
    gam_last = jnp.exp(c_end)

    hq = QUAD // RWKV_HEAD
    t_idx = lax.broadcasted_iota(jnp.int32, (L, hq * L), 0)
    s_idx = lax.broadcasted_iota(jnp.int32, (L, hq * L), 1) % L
    m_strict = t_idx > s_idx
    m_incl = t_idx >= s_idx
    eye = (t_idx == s_idx).astype(F32)
    m_off = []
    m = 1
    while m < L:
        m_off.append(((t_idx // (2 * m)) == (s_idx // (2 * m))) & ((t_idx % (2 * m)) >= m)
                     & ((s_idx % (2 * m)) < m))
        m *= 2
    qrow = lax.broadcasted_iota(jnp.int32, (QUAD, QUAD), 0)
    qcol = lax.broadcasted_iota(jnp.int32, (QUAD, QUAD), 1)
    m_state = (qrow // RWKV_HEAD) == (qcol // RWKV_HEAD)
    lane_head = lax.broadcasted_iota(jnp.int32, (L, QUAD), 1) // RWKV_HEAD
    lane_blk = lax.broadcasted_iota(jnp.int32, (L, hq * L), 1) // L

    def quads(z):
        return jnp.stack([z[g][:, q * QUAD:(q + 1) * QUAD] for g in range(G) for q in range(N_QUADS)], axis=0)

    def bdiag(z, blk):
        z = z.astype(SCAN_DTYPE)
        zero = jnp.zeros_like(z)
        return jnp.concatenate([jnp.where(blk == h, z, zero) for h in range(hq)], axis=1)

    a_q, r_q, v_q, b_q, k_q = quads(a_t), quads(r_t), quads(v_ref[...]), quads(b_t), quads(k_t)
    ar = jnp.concatenate([a_q, r_q], axis=1)
    g_b = _qdot(ar, bdiag(b_q, lane_head), _BNT)
    g_k = _qdot(ar, bdiag(k_q, lane_head), _BNT)
    n_ab = g_b[:, :L]
    a_rb = jnp.where(m_incl, g_b[:, L:], 0.0)
    a_ak = jnp.where(m_strict, g_k[:, :L], 0.0)
    a_rk = jnp.where(m_incl, g_k[:, L:], 0.0)
    tinv = eye + jnp.where(m_off[0], n_ab, 0.0)
    for m_lvl in m_off[1:]:
        n_off = jnp.where(m_lvl, n_ab, 0.0)
        tinv = tinv + _qdot(tinv, bdiag(_qdot(n_off, bdiag(tinv, lane_blk)), lane_blk))
    s0 = state[...]
    xs = _qdot(ar, s0, _BNT)
    vs = _qdot(jnp.concatenate([a_ak, a_rk], axis=1), bdiag(v_q, lane_head))
    u = _qdot(tinv, bdiag(xs[:, :L] + vs[:, :L], lane_head))
    y = xs[:, L:] + vs[:, L:] + _qdot(a_rb, bdiag(u, lane_head))
    upd = _qdot(jnp.concatenate([u, v_q], axis=1), jnp.concatenate([quads(b_h), quads(k_h)], axis=1), _BTN)
    state[...] = s0 * quads(gam_last) + jnp.where(m_state, upd, 0.0)
    for g in range(G):
        y_ref[g] = jnp.concatenate([y[g * N_QUADS + q] for q in range(N_QUADS)], axis=1)


def _rwkv_scan(r, c, k, v, a, b):
    B, S, D = r.shape
    L = min(SCAN_CHUNK, S)
    G = SCAN_ROWS if B % SCAN_ROWS == 0 else 1
    tile = pl.BlockSpec((G, L, D), lambda bb, c: (bb, c, 0))
    return pl.pallas_call(
        functools.partial(_rwkv_scan_kernel, L=L),
        grid=(B // G, S // L),
        in_specs=[tile] * 6,
        out_specs=tile,
        out_shape=jax.ShapeDtypeStruct((B, S, D), F32),
        scratch_shapes=[pltpu.VMEM((G * N_QUADS, QUAD, QUAD), F32)],
        compiler_params=_cparams("arbitrary", "arbitrary"),
        name="rwkv_scan",
    )(r, c, k, v, a, b)


def _rwkv_post_kernel(x_ref, y_ref, bonus_ref, gate_ref, g_ref, gng_ref, gnb_ref, wo_ref, e_ref, o_ref):
    y = y_ref[...]
    mean = _head_sum(y, e_ref)
    d = y - mean
    var = _head_sum(d * d, e_ref, exact=False)
    yn = d * lax.rsqrt(var + RWKV_GN_EPS) * gng_ref[...] + gnb_ref[...]
    t = _bdot((yn + bonus_ref[...]) * gate_ref[...], wo_ref[...])
    o_ref[...] = x_ref[...] + _rms(t, g_ref[...])


def _rwkv_post(x2d, y2d, bonus2d, gate2d, g1, gn_g, gn_b, w_o_bf, e):
    T, D = x2d.shape
    tm = min(512, T)
    tile = pl.BlockSpec((tm, D), lambda r: (r, 0))
    consts = [g1, gn_g, gn_b, w_o_bf, e]
    return pl.pallas_call(
        _rwkv_post_kernel,
        grid=(T // tm,),
        in_specs=[tile] * 4 + [_const_spec(c.shape) for c in consts],
        out_specs=tile,
        out_shape=jax.ShapeDtypeStruct((T, D), F32),
        compiler_params=_cparams("arbitrary"),
        name="rwkv_post",
    )(x2d, y2d, bonus2d, gate2d, *consts)


def _head_selector(value):
    head = jnp.arange(QUAD) // RWKV_HEAD
    return jnp.where(head[:, None] == head[None, :], value, 0.0).astype(BF16)


def kernel(x, mem, ln_gains, mem_norm, a_conv_w, a_conv_b, a_w_in, a_b_in, a_gate_w, a_gate_b, a_lambda, a_w_out, a_b_out, b_mu, b_w_rkv, b_w0, b_w1, b_w2, b_a0, b_a1, b_a2, b_g1, b_g2, b_k_k, b_k_a, b_r_k, b_gn_g, b_gn_b, b_w_o, c_w_q, c_w_kv, c_w_o, m_w_up, m_w_down):
    B, S, D = x.shape
    M = mem.shape[1]
    depth = ln_gains.shape[0]
    assert D == D_MODEL and depth == 2
    bf = lambda w: w.astype(BF16)
    row = lambda p: p.reshape(1, -1)

    kv = _mem_kv(mem.reshape(B * M, D), row(mem_norm), bf(c_w_kv)).reshape(depth, B, M, 2 * D)
    e = _head_selector(1.0)
    e_mean = _head_selector(1.0 / RWKV_HEAD)

    for i in range(depth):
        g = ln_gains[i]
        j = i // 2
        if i % 2 == 0:
            x = _rglru_layer(x, g[0:2], a_conv_w[j], row(a_conv_b[j]), bf(a_w_in[j]), row(a_b_in[j]),
                             bf(a_gate_w[j]), a_gate_b[j].reshape(2, D), row(a_lambda[j]),
                             bf(a_w_out[j]), row(a_b_out[j]))
        else:
            r, c, k, v, a, b, gate, bonus = _rwkv_pre(
                x, g[0:1], b_mu[j], bf(b_w_rkv[j]), row(b_w0[j]), bf(b_w1[j]), bf(b_w2[j]), row(b_a0[j]),
                bf(b_a1[j]), bf(b_a2[j]), bf(b_g1[j]), bf(b_g2[j]), row(b_k_k[j]), row(b_k_a[j]),
                row(b_r_k[j]), e)
            y = _rwkv_scan(r, c, k, v, a, b)
            x = _rwkv_post(x.reshape(B * S, D), y.reshape(B * S, D), bonus.reshape(B * S, D),
                           gate.reshape(B * S, D), g[1:2], row(b_gn_g[j]), row(b_gn_b[j]), bf(b_w_o[j]),
                           e_mean).reshape(B, S, D)
        x = _xattn_layer(x, kv, i, g[2:4], bf(c_w_q[i]), bf(c_w_o[i]))
        x = _mlp_layer(x.reshape(B * S, D), g[4:6], bf(m_w_up[i]), bf(m_w_down[i])).reshape(B, S, D)
    return x
```

```python
import functools

import jax
import jax.numpy as jnp
from jax import lax
from jax.experimental import pallas as pl
from jax.experimental.pallas import tpu as pltpu

F32 = jnp.float32
BF16 = jnp.bfloat16

D_MODEL = 1024
RMS_EPS = 1e-6
LRU_HEADS = 4
LRU_BLOCK = D_MODEL // LRU_HEADS
CONV_WIDTH = 4
LRU_C = 8.0
RWKV_HEAD = 64
RWKV_HEADS = D_MODEL // RWKV_HEAD
RWKV_GN_EPS = 64e-5
DECAY_LOG_SCALE = -0.6065306597126334
MEM_HEADS = 4
MEM_HEAD_DIM = D_MODEL // MEM_HEADS
D_FF = 4 * D_MODEL

LANES = 128
SUBLANES = 8
QUAD = 256
N_QUADS = D_MODEL // QUAD
SEQ_TILE = 256
TAIL_TILE = 512
MLP_PARTS = 4
SCAN_PARTS = 1
SCAN_CHUNK = 64
SCAN_ROWS = 4
VMEM_LIMIT = 56 * 1024 * 1024

SCAN_DTYPE = BF16


def _cparams(*sem):
    return pltpu.CompilerParams(dimension_semantics=sem, vmem_limit_bytes=VMEM_LIMIT)


def _rms(x, g):
    return x * lax.rsqrt(jnp.mean(x * x, axis=-1, keepdims=True) + RMS_EPS) * g


def _sigmoid(x):
    return 1.0 / (1.0 + jnp.exp(-x))


def _softplus(x):
    return jnp.maximum(x, 0.0) + jnp.log(1.0 + jnp.exp(-jnp.abs(x)))


def _gelu_tanh(x):
    return 0.5 * x * (1.0 + jnp.tanh(0.7978845608028654 * (x + 0.044715 * (x * x * x))))


def _bdot(a, b):
    return jnp.dot(a.astype(BF16), b.astype(BF16), preferred_element_type=F32)


def _shift_rows(x, tail, k):
    rolled = pltpu.roll(x, shift=k, axis=0)
    row = lax.broadcasted_iota(jnp.int32, (SUBLANES, x.shape[1]), 0)
    head = jnp.where(row < k, pltpu.roll(tail, shift=k, axis=0), rolled[:SUBLANES])
    return jnp.concatenate([head, rolled[SUBLANES:]], axis=0)


def _const_spec(shape, single=False):
    nd = len(shape)
    if single:
        return pl.BlockSpec(shape, lambda *_: (0,) * nd, pipeline_mode=pl.Buffered(1))
    return pl.BlockSpec(shape, lambda *_: (0,) * nd)


def _memkv_kernel(mem_ref, gn_ref, wkv_ref, out_ref):
    mn = _rms(mem_ref[...], gn_ref[...])
    out_ref[...] = _bdot(mn, wkv_ref[...]).astype(BF16)


def _mem_kv(mem2d, mem_norm, w_kv_bf):
    rows = mem2d.shape[0]
    depth = w_kv_bf.shape[0]
    tm = min(512, rows)
    return pl.pallas_call(
        _memkv_kernel,
        grid=(depth, rows // tm),
        in_specs=[
            pl.BlockSpec((tm, D_MODEL), lambda i, r: (r, 0)),
            _const_spec((1, D_MODEL)),
            pl.BlockSpec((None, D_MODEL, 2 * D_MODEL), lambda i, r: (i, 0, 0)),
        ],
        out_specs=pl.BlockSpec((None, tm, 2 * D_MODEL), lambda i, r: (i, r, 0)),
        out_shape=jax.ShapeDtypeStruct((depth, rows, 2 * D_MODEL), BF16),
        compiler_params=_cparams("arbitrary", "arbitrary"),
        name="mem_kv",
    )(mem2d, mem_norm, w_kv_bf)


def _run(gen):
    while True:
        try:
            next(gen)
        except StopIteration as stop:
            return stop.value


def _run_interleaved(order, gens):
    done = {}
    for name in order:
        try:
            next(gens[name])
        except StopIteration as stop:
            done[name] = stop.value
    return done


def _rglru_block(x, g_ref, convw_ref, convb_ref, win_ref, bin_ref, gw_ref, gb_ref, lam_ref,
                 wout_ref, bout_ref, ubuf, a_s, b_s, h_s, hcar):
    ts = x.shape[0]
    hn = _rms(x, g_ref[0:1, :])
    proj = _bdot(hn, win_ref[...]) + bin_ref[...]
    yield
    yb = _gelu_tanh(proj[:, :D_MODEL])
    u = proj[:, D_MODEL:]
    tail = ubuf[...]
    conv = convb_ref[...] + u * convw_ref[CONV_WIDTH - 1:CONV_WIDTH, :]
    for tap in range(CONV_WIDTH - 1):
        conv = conv + _shift_rows(u, tail, CONV_WIDTH - 1 - tap) * convw_ref[tap:tap + 1, :]
    ubuf[...] = u[ts - SUBLANES:, :]
    yield
    conv_bf = conv.astype(BF16)
    r_parts, i_parts = [], []
    for h in range(LRU_HEADS):
        cb = conv_bf[:, h * LRU_BLOCK:(h + 1) * LRU_BLOCK]
        r_parts.append(jnp.dot(cb, gw_ref[0, h], preferred_element_type=F32))
        i_parts.append(jnp.dot(cb, gw_ref[1, h], preferred_element_type=F32))
    r_gate = _sigmoid(jnp.concatenate(r_parts, axis=1) + gb_ref[0:1, :])
    i_gate = _sigmoid(jnp.concatenate(i_parts, axis=1) + gb_ref[1:2, :])
    a = jnp.exp((-LRU_C) * r_gate * _softplus(-lam_ref[...]))
    a_s[...] = a
    b_s[...] = jnp.sqrt(1.0 - a * a) * i_gate * conv
    yield

    def body(t, h):
        h = a_s[pl.ds(t, 1), :] * h + b_s[pl.ds(t, 1), :]
        h_s[pl.ds(t, 1), :] = h
        return h

    h = hcar[...]
    for part in range(SCAN_PARTS):
        rows = ts // SCAN_PARTS
        h = lax.fori_loop(part * rows, (part + 1) * rows, body, h, unroll=True)
        yield
    hcar[...] = h
    out = _bdot(h_s[...] * yb, wout_ref[...]) + bout_ref[...]
    return x + _rms(out, g_ref[1:2, :])


def _xattn_block(x, k_ref, v_ref, g_ref, wq_ref, wo_ref):
    hn = _rms(x, g_ref[0:1, :])
    q = (_bdot(hn, wq_ref[...]) * (MEM_HEAD_DIM ** -0.5)).astype(BF16)
    yield
    outs = []
    for h in range(MEM_HEADS):
        sl = slice(h * MEM_HEAD_DIM, (h + 1) * MEM_HEAD_DIM)
        sc = lax.dot_general(q[:, sl], k_ref[:, sl], (((1,), (1,)), ((), ())),
                             preferred_element_type=F32)
        p = jnp.exp(sc - jnp.max(sc, axis=-1, keepdims=True))
        l = jnp.sum(p, axis=-1, keepdims=True)
        outs.append(jnp.dot(p.astype(BF16), v_ref[:, sl], preferred_element_type=F32) / l)
    yield
    c = _bdot(jnp.concatenate(outs, axis=1), wo_ref[...])
    return x + _rms(c, g_ref[1:2, :])


def _mlp_block(x, g_ref, wup_ref, wdn_ref):
    hn = _rms(x, g_ref[0:1, :]).astype(BF16)
    acc = None
    for c in range(MLP_PARTS):
        sl = slice(c * (D_FF // MLP_PARTS), (c + 1) * (D_FF // MLP_PARTS))
        h = jnp.maximum(jnp.dot(hn, wup_ref[:, sl], preferred_element_type=F32), 0.0)
        part = jnp.dot((h * h).astype(BF16), wdn_ref[sl, :], preferred_element_type=F32)
        acc = part if acc is None else acc + part
        if c + 1 < MLP_PARTS:
            yield
    return x + _rms(acc, g_ref[1:2, :])


def _xattn_mlp_block(x, k_ref, v_ref, gains_ref, wq_ref, wo_ref, wup_ref, wdn_ref):
    x1 = yield from _xattn_block(x, k_ref, v_ref, gains_ref.at[2:4], wq_ref, wo_ref)
    yield
    return (yield from _mlp_block(x1, gains_ref.at[4:6], wup_ref, wdn_ref))


_LAYER0_ORDER = ("b", "a", "b", "b") + ("a", "b") * MLP_PARTS + ("a",) * (3 + SCAN_PARTS - MLP_PARTS)


def _layer0_kernel(x_ref, k_ref, v_ref, gains_ref, convw_ref, convb_ref, win_ref, bin_ref, gw_ref, gb_ref,
                   lam_ref, wout_ref, bout_ref, wq_ref, wo_ref, wup_ref, wdn_ref, o_ref,
                   ubuf, a_s, b_s, h_s, hcar, mid, *, tiles_per_seq):
    i = pl.program_id(0)

    @pl.when(i == 0)
    def _():
        mid[...] = jnp.zeros(mid.shape, F32)

    @pl.when(i % tiles_per_seq == 0)
    def _():
        ubuf[...] = jnp.zeros(ubuf.shape, F32)
        hcar[...] = jnp.zeros(hcar.shape, F32)

    done = _run_interleaved(_LAYER0_ORDER, {
        "a": _rglru_block(x_ref[0], gains_ref.at[0:2], convw_ref, convb_ref, win_ref, bin_ref, gw_ref,
                          gb_ref, lam_ref, wout_ref, bout_ref, ubuf, a_s, b_s, h_s, hcar),
        "b": _xattn_mlp_block(mid[...], k_ref, v_ref, gains_ref, wq_ref, wo_ref, wup_ref, wdn_ref),
    })
    o_ref[0] = done["b"]
    mid[...] = done["a"]


def _layer0(x, kv, layer, gains, conv_w, conv_b, w_in_bf, b_in, gate_w_bf, gate_b, lam, w_out_bf, b_out,
            w_q_bf, w_o_bf, w_up_bf, w_down_bf):
    B, S, D = x.shape
    M = kv.shape[2]
    ts = min(SEQ_TILE, S)
    n = S // ts
    last = B * n - 1

    def cur(i):
        j = jnp.minimum(i, last)
        return j // n, j % n

    def prev(i):
        j = jnp.maximum(i - 1, 0)
        return j // n, j % n

    weights = [conv_w, conv_b, w_in_bf, b_in, gate_w_bf, gate_b, lam, w_out_bf, b_out,
               w_q_bf, w_o_bf, w_up_bf, w_down_bf]
    return pl.pallas_call(
        functools.partial(_layer0_kernel, tiles_per_seq=n),
        grid=(B * n + 1,),
        in_specs=[
            pl.BlockSpec((1, ts, D), lambda i: (*cur(i), 0)),
            pl.BlockSpec((None, None, M, D), lambda i: (layer, prev(i)[0], 0, 0)),
            pl.BlockSpec((None, None, M, D), lambda i: (layer, prev(i)[0], 0, 1)),
            _const_spec(gains.shape),
        ] + [_const_spec(w.shape, single=True) for w in weights],
        out_specs=pl.BlockSpec((1, ts, D), lambda i: (*prev(i), 0)),
        out_shape=jax.ShapeDtypeStruct((B, S, D), F32),
        scratch_shapes=[
            pltpu.VMEM((SUBLANES, D), F32),
            pltpu.VMEM((ts, D), F32),
            pltpu.VMEM((ts, D), F32),
            pltpu.VMEM((ts, D), F32),
            pltpu.VMEM((1, D), F32),
            pltpu.VMEM((ts, D), F32),
        ],
        compiler_params=_cparams("arbitrary"),
        name="layer0",
    )(x, kv, kv, gains, *weights)


def _head_sum(x, e_ref, exact=True):
    hi = x.astype(BF16)
    lo = (x - hi.astype(F32)).astype(BF16) if exact else None
    e = e_ref[...]
    parts = []
    for q in range(N_QUADS):
        sl = slice(q * QUAD, (q + 1) * QUAD)
        part = jnp.dot(hi[:, sl], e, preferred_element_type=F32)
        if exact:
            part = part + jnp.dot(lo[:, sl], e, preferred_element_type=F32)
        parts.append(part)
    return jnp.concatenate(parts, axis=1)


def _rwkv_pre_kernel(x_ref, g_ref, mu_ref, wrkv_ref, w0_ref, w1_ref, w2_ref, a0_ref, a1_ref, a2_ref,
                     g1_ref, g2_ref, kk_ref, ka_ref, rk_ref, e_ref,
                     r_out, c_out, k_out, v_out, a_out, b_out, g_out, bonus_out, hbuf, *, ts):
    s = pl.program_id(1)

    @pl.when(s == 0)
    def _():
        hbuf[...] = jnp.zeros((SUBLANES, D_MODEL), F32)

    hn = _rms(x_ref[0], g_ref[...])
    xx = (_shift_rows(hn, hbuf[...], 1) - hn).astype(BF16)
    hbuf[...] = hn[ts - SUBLANES:, :]
    hn_bf = hn.astype(BF16)
    mu_bf = mu_ref[...].astype(BF16)

    def mix(i):
        return hn_bf + xx * mu_bf[i:i + 1, :]

    r = _bdot(mix(0), wrkv_ref[0])
    k = _bdot(mix(2), wrkv_ref[1])
    v = _bdot(mix(3), wrkv_ref[2])
    z = w0_ref[...] + _bdot(jnp.tanh(_bdot(mix(1), w1_ref[...])), w2_ref[...])
    lw = DECAY_LOG_SCALE * _sigmoid(z)
    alr = _sigmoid(a0_ref[...] + _bdot(_bdot(mix(4), a1_ref[...]), a2_ref[...]))
    g = _bdot(_sigmoid(_bdot(mix(5), g1_ref[...])), g2_ref[...])

    kk = k * kk_ref[...]
    kk = kk * lax.rsqrt(jnp.maximum(_head_sum(kk * kk, e_ref, exact=False), 1e-24))
    ka = ka_ref[...]
    k = k * ((1.0 - ka) + alr * ka)
    L = min(SCAN_CHUNK, ts)
    ti = lax.broadcasted_iota(jnp.int32, (ts, ts), 0)
    si = lax.broadcasted_iota(jnp.int32, (ts, ts), 1)
    tri = ((ti // L == si // L) & (ti >= si)).astype(BF16)
    lw_hi = lw.astype(BF16)
    lw_lo = (lw - lw_hi.astype(F32)).astype(BF16)
    r_out[0] = r.astype(BF16)
    c_out[0] = (jnp.dot(tri, lw_hi, preferred_element_type=F32)
                + jnp.dot(tri, lw_lo, preferred_element_type=F32))
    k_out[0] = k.astype(BF16)
    v_out[0] = v.astype(BF16)
    a_out[0] = (-kk).astype(BF16)
    b_out[0] = (kk * alr).astype(BF16)
    g_out[0] = g.astype(BF16)
    bonus_out[0] = (_head_sum(r * k * rk_ref[...], e_ref, exact=False) * v).astype(BF16)


def _rwkv_pre(x, g0, mu, w_rkv_bf, w0, w1_bf, w2_bf, a0, a1_bf, a2_bf, g1_bf, g2_bf, k_k, k_a, r_k, e):
    B, S, D = x.shape
    ts = min(256, S)
    tile = pl.BlockSpec((1, ts, D), lambda b, s: (b, s, 0))
    consts = [g0, mu, w_rkv_bf, w0, w1_bf, w2_bf, a0, a1_bf, a2_bf, g1_bf, g2_bf, k_k, k_a, r_k, e]
    return pl.pallas_call(
        functools.partial(_rwkv_pre_kernel, ts=ts),
        grid=(B, S // ts),
        in_specs=[tile] + [_const_spec(c.shape) for c in consts],
        out_specs=[tile] * 8,
        out_shape=[jax.ShapeDtypeStruct((B, S, D), F32 if i == 1 else BF16) for i in range(8)],
        scratch_shapes=[pltpu.VMEM((SUBLANES, D), F32)],
        compiler_params=_cparams("arbitrary", "arbitrary"),
        name="rwkv_pre",
    )(x, *consts)


_BNN = (((2,), (1,)), ((0,), (0,)))
_BNT = (((2,), (2,)), ((0,), (0,)))
_BTN = (((1,), (1,)), ((0,), (0,)))


def _qdot(a, b, dims=_BNN):
    return lax.dot_general(a.astype(SCAN_DTYPE), b.astype(SCAN_DTYPE), dims, preferred_element_type=F32)


def _rwkv_scan_kernel(r_ref, c_ref, k_ref, v_ref, a_ref, b_ref, y_ref, state, *, L):
    @pl.when(pl.program_id(1) == 0)
    def _():
        state[...] = jnp.zeros(state.shape, F32)

    G = c_ref.shape[0]
    c = c_ref[...]
    first_row = lax.broadcasted_iota(jnp.int32, c.shape, 1) == 0
    c_prev = jnp.where(first_row, 0.0, pltpu.roll(c, shift=1, axis=1))
    c_end = c[:, L - 1:L, :]
    gam = jnp.exp(c)
    gam_inv = jnp.exp(-c)
    gam_end = jnp.exp(c_end - c)
    b_in = b_ref[...].astype(F32)
    k_in = k_ref[...].astype(F32)
    r_t = r_ref[...].astype(F32) * gam
    a_t = a_ref[...].astype(F32) * jnp.exp(c_prev)
    b_t = b_in * gam_inv
    k_t = k_in * gam_inv
    b_h = b_in * gam_end
    k_h = k_in * gam_end
    gam_last = jnp.exp(c_end)

    hq = QUAD // RWKV_HEAD
    t_idx = lax.broadcasted_iota(jnp.int32, (L, hq * L), 0)
    s_idx = lax.broadcasted_iota(jnp.int32, (L, hq * L), 1) % L
    m_strict = t_idx > s_idx
    m_incl = t_idx >= s_idx
    eye = (t_idx == s_idx).astype(F32)
    m_off = []
    m = 1
    while m < L:
        m_off.append(((t_idx // (2 * m)) == (s_idx // (2 * m))) & ((t_idx % (2 * m)) >= m)
                     & ((s_idx % (2 * m)) < m))
        m *= 2
    qrow = lax.broadcasted_iota(jnp.int32, (QUAD, QUAD), 0)
    qcol = lax.broadcasted_iota(jnp.int32, (QUAD, QUAD), 1)
    m_state = (qrow // RWKV_HEAD) == (qcol // RWKV_HEAD)
    lane_head = lax.broadcasted_iota(jnp.int32, (L, QUAD), 1) // RWKV_HEAD
    lane_blk = lax.broadcasted_iota(jnp.int32, (L, hq * L), 1) // L

    def quads(z):
        return jnp.stack([z[g][:, q * QUAD:(q + 1) * QUAD] for g in range(G) for q in range(N_QUADS)], axis=0)

    def bdiag(z, blk):
        z = z.astype(SCAN_DTYPE)
        zero = jnp.zeros_like(z)
        return jnp.concatenate([jnp.where(blk == h, z, zero) for h in range(hq)], axis=1)

    a_q, r_q, v_q, b_q, k_q = quads(a_t), quads(r_t), quads(v_ref[...]), quads(b_t), quads(k_t)
    ar = jnp.concatenate([a_q, r_q], axis=1)
    g_b = _qdot(ar, bdiag(b_q, lane_head), _BNT)
    g_k = _qdot(ar, bdiag(k_q, lane_head), _BNT)
    n_ab = g_b[:, :L]
    a_rb = jnp.where(m_incl, g_b[:, L:], 0.0)
    a_ak = jnp.where(m_strict, g_k[:, :L], 0.0)
    a_rk = jnp.where(m_incl, g_k[:, L:], 0.0)
    tinv = eye + jnp.where(m_off[0], n_ab, 0.0)
    for m_lvl in m_off[1:]:
        n_off = jnp.where(m_lvl, n_ab, 0.0)
        tinv = tinv + _qdot(tinv, bdiag(_qdot(n_off, bdiag(tinv, lane_blk)), lane_blk))
    s0 = state[...]
    xs = _qdot(ar, s0, _BNT)
    vs = _qdot(jnp.concatenate([a_ak, a_rk], axis=1), bdiag(v_q, lane_head))
    u = _qdot(tinv, bdiag(xs[:, :L] + vs[:, :L], lane_head))
    y = xs[:, L:] + vs[:, L:] + _qdot(a_rb, bdiag(u, lane_head))
    upd = _qdot(jnp.concatenate([u, v_q], axis=1), jnp.concatenate([quads(b_h), quads(k_h)], axis=1), _BTN)
    state[...] = s0 * quads(gam_last) + jnp.where(m_state, upd, 0.0)
    for g in range(G):
        y_ref[g] = jnp.concatenate([y[g * N_QUADS + q] for q in range(N_QUADS)], axis=1)


def _rwkv_scan(r, c, k, v, a, b):
    B, S, D = r.shape
    L = min(SCAN_CHUNK, S)
    G = SCAN_ROWS if B % SCAN_ROWS == 0 else 1
    tile = pl.BlockSpec((G, L, D), lambda bb, c: (bb, c, 0))
    return pl.pallas_call(
        functools.partial(_rwkv_scan_kernel, L=L),
        grid=(B // G, S // L),
        in_specs=[tile] * 6,
        out_specs=tile,
        out_shape=jax.ShapeDtypeStruct((B, S, D), F32),
        scratch_shapes=[pltpu.VMEM((G * N_QUADS, QUAD, QUAD), F32)],
        compiler_params=_cparams("arbitrary", "arbitrary"),
        name="rwkv_scan",
    )(r, c, k, v, a, b)


def _rwkv_tail_kernel(x_ref, y_ref, bonus_ref, gate_ref, k_ref, v_ref, gains_ref, gng_ref, gnb_ref, e_ref,
                      wmix_ref, wq_ref, wo_ref, wup_ref, wdn_ref, o_ref):
    y = y_ref[0]
    mean = _head_sum(y, e_ref)
    d = y - mean
    var = _head_sum(d * d, e_ref, exact=False)
    yn = d * lax.rsqrt(var + RWKV_GN_EPS) * gng_ref[...] + gnb_ref[...]
    t = _bdot((yn + bonus_ref[0]) * gate_ref[0], wmix_ref[...])
    x1 = x_ref[0] + _rms(t, gains_ref[1:2, :])
    o_ref[0] = _run(_xattn_mlp_block(x1, k_ref, v_ref, gains_ref, wq_ref, wo_ref, wup_ref, wdn_ref))


def _rwkv_tail(x, y, bonus, gate, kv, layer, gains, gn_g, gn_b, e_mean, w_mix_bf, w_q_bf, w_o_bf, w_up_bf,
               w_down_bf):
    B, S, D = x.shape
    M = kv.shape[2]
    ts = min(TAIL_TILE, S)
    tile = pl.BlockSpec((1, ts, D), lambda b, s: (b, s, 0))
    weights = [w_mix_bf, w_q_bf, w_o_bf, w_up_bf, w_down_bf]
    return pl.pallas_call(
        _rwkv_tail_kernel,
        grid=(B, S // ts),
        in_specs=[tile] * 4 + [
            pl.BlockSpec((None, None, M, D), lambda b, s: (layer, b, 0, 0)),
            pl.BlockSpec((None, None, M, D), lambda b, s: (layer, b, 0, 1)),
            _const_spec(gains.shape), _const_spec(gn_g.shape), _const_spec(gn_b.shape),
            _const_spec(e_mean.shape),
        ] + [_const_spec(w.shape, single=True) for w in weights],
        out_specs=tile,
        out_shape=jax.ShapeDtypeStruct((B, S, D), F32),
        compiler_params=_cparams("arbitrary", "arbitrary"),
        name="rwkv_tail",
    )(x, y, bonus, gate, kv, kv, gains, gn_g, gn_b, e_mean, *weights)


def _head_selector(value):
    head = jnp.arange(QUAD) // RWKV_HEAD
    return jnp.where(head[:, None] == head[None, :], value, 0.0).astype(BF16)


def kernel(x, mem, ln_gains, mem_norm, a_conv_w, a_conv_b, a_w_in, a_b_in, a_gate_w, a_gate_b, a_lambda, a_w_out, a_b_out, b_mu, b_w_rkv, b_w0, b_w1, b_w2, b_a0, b_a1, b_a2, b_g1, b_g2, b_k_k, b_k_a, b_r_k, b_gn_g, b_gn_b, b_w_o, c_w_q, c_w_kv, c_w_o, m_w_up, m_w_down):
    B, S, D = x.shape
    M = mem.shape[1]
    depth = ln_gains.shape[0]
    assert D == D_MODEL and depth == 2
    bf = lambda w: w.astype(BF16)
    row = lambda p: p.reshape(1, -1)

    kv = _mem_kv(mem.reshape(B * M, D), row(mem_norm), bf(c_w_kv)).reshape(depth, B, M, 2 * D)
    e = _head_selector(1.0)
    e_mean = _head_selector(1.0 / RWKV_HEAD)

    for i in range(depth):
        g = ln_gains[i]
        j = i // 2
        if i % 2 == 0:
            x = _layer0(x, kv, i, g, a_conv_w[j], row(a_conv_b[j]), bf(a_w_in[j]), row(a_b_in[j]),
                        bf(a_gate_w[j]), a_gate_b[j].reshape(2, D), row(a_lambda[j]), bf(a_w_out[j]),
                        row(a_b_out[j]), bf(c_w_q[i]), bf(c_w_o[i]), bf(m_w_up[i]), bf(m_w_down[i]))
        else:
            r, c, k, v, a, b, gate, bonus = _rwkv_pre(
                x, g[0:1], b_mu[j], bf(b_w_rkv[j]), row(b_w0[j]), bf(b_w1[j]), bf(b_w2[j]), row(b_a0[j]),
                bf(b_a1[j]), bf(b_a2[j]), bf(b_g1[j]), bf(b_g2[j]), row(b_k_k[j]), row(b_k_a[j]),
                row(b_r_k[j]), e)
            y = _rwkv_scan(r, c, k, v, a, b)
            x = _rwkv_tail(x, y, bonus, gate, kv, i, g, row(b_gn_g[j]), row(b_gn_b[j]), e_mean, bf(b_w_o[j]),
                           bf(c_w_q[i]), bf(c_w_o[i]), bf(m_w_up[i]), bf(m_w_down[i]))
    return x
```

```python
import functools

import jax
import jax.numpy as jnp
from jax import lax
from jax.experimental import pallas as pl
from jax.experimental.pallas import tpu as pltpu

F32 = jnp.float32
BF16 = jnp.bfloat16

D_MODEL = 1024
RMS_EPS = 1e-6
LRU_HEADS = 4
LRU_BLOCK = D_MODEL // LRU_HEADS
CONV_WIDTH = 4
LRU_C = 8.0
RWKV_HEAD = 64
RWKV_HEADS = D_MODEL // RWKV_HEAD
RWKV_GN_EPS = 64e-5
DECAY_LOG_SCALE = -0.6065306597126334
MEM_HEADS = 4
MEM_HEAD_DIM = D_MODEL // MEM_HEADS
D_FF = 4 * D_MODEL

LANES = 128
SUBLANES = 8
QUAD = 256
N_QUADS = D_MODEL // QUAD
SEQ_TILE = 256
TAIL_TILE = 512
MLP_PARTS = 4
SCAN_CHUNK = 64
SCAN_ROWS = 4
VMEM_LIMIT = 56 * 1024 * 1024

SCAN_DTYPE = BF16


def _cparams(*sem):
    return pltpu.CompilerParams(dimension_semantics=sem, vmem_limit_bytes=VMEM_LIMIT)


def _rms(x, g):
    return x * lax.rsqrt(jnp.mean(x * x, axis=-1, keepdims=True) + RMS_EPS) * g


def _sigmoid(x):
    return 1.0 / (1.0 + jnp.exp(-x))


def _softplus(x):
    return jnp.maximum(x, 0.0) + jnp.log(1.0 + jnp.exp(-jnp.abs(x)))


def _gelu_tanh(x):
    return 0.5 * x * (1.0 + jnp.tanh(0.7978845608028654 * (x + 0.044715 * (x * x * x))))


def _bdot(a, b):
    return jnp.dot(a.astype(BF16), b.astype(BF16), preferred_element_type=F32)


def _shift_rows(x, tail, k):
    rolled = pltpu.roll(x, shift=k, axis=0)
    row = lax.broadcasted_iota(jnp.int32, (SUBLANES, x.shape[1]), 0)
    head = jnp.where(row < k, pltpu.roll(tail, shift=k, axis=0), rolled[:SUBLANES])
    return jnp.concatenate([head, rolled[SUBLANES:]], axis=0)


def _const_spec(shape, single=False):
    nd = len(shape)
    if single:
        return pl.BlockSpec(shape, lambda *_: (0,) * nd, pipeline_mode=pl.Buffered(1))
    return pl.BlockSpec(shape, lambda *_: (0,) * nd)


def _memkv_kernel(mem_ref, gn_ref, wkv_ref, out_ref):
    mn = _rms(mem_ref[...], gn_ref[...])
    out_ref[...] = _bdot(mn, wkv_ref[...]).astype(BF16)


def _mem_kv(mem2d, mem_norm, w_kv_bf):
    rows = mem2d.shape[0]
    depth = w_kv_bf.shape[0]
    tm = min(512, rows)
    return pl.pallas_call(
        _memkv_kernel,
        grid=(depth, rows // tm),
        in_specs=[
            pl.BlockSpec((tm, D_MODEL), lambda i, r: (r, 0)),
            _const_spec((1, D_MODEL)),
            pl.BlockSpec((None, D_MODEL, 2 * D_MODEL), lambda i, r: (i, 0, 0)),
        ],
        out_specs=pl.BlockSpec((None, tm, 2 * D_MODEL), lambda i, r: (i, r, 0)),
        out_shape=jax.ShapeDtypeStruct((depth, rows, 2 * D_MODEL), BF16),
        compiler_params=_cparams("arbitrary", "arbitrary"),
        name="mem_kv",
    )(mem2d, mem_norm, w_kv_bf)


def _run(gen):
    while True:
        try:
            next(gen)
        except StopIteration as stop:
            return stop.value


def _run_interleaved(order, gens):
    done = {}
    for name in order:
        try:
            next(gens[name])
        except StopIteration as stop:
            done[name] = stop.value
    return done


def _rglru_block(x, g_ref, convw_ref, convb_ref, win_ref, bin_ref, gw_ref, gb_ref, lam_ref,
                 wout_ref, bout_ref, ubuf, a_s, b_s, h_s, hcar):
    ts = x.shape[0]
    hn = _rms(x, g_ref[0:1, :])
    proj = _bdot(hn, win_ref[...]) + bin_ref[...]
    yield
    yb = _gelu_tanh(proj[:, :D_MODEL])
    u = proj[:, D_MODEL:]
    tail = ubuf[...]
    conv = convb_ref[...] + u * convw_ref[CONV_WIDTH - 1:CONV_WIDTH, :]
    for tap in range(CONV_WIDTH - 1):
        conv = conv + _shift_rows(u, tail, CONV_WIDTH - 1 - tap) * convw_ref[tap:tap + 1, :]
    ubuf[...] = u[ts - SUBLANES:, :]
    yield
    conv_bf = conv.astype(BF16)
    r_parts, i_parts = [], []
    for h in range(LRU_HEADS):
        cb = conv_bf[:, h * LRU_BLOCK:(h + 1) * LRU_BLOCK]
        r_parts.append(jnp.dot(cb, gw_ref[0, h], preferred_element_type=F32))
        i_parts.append(jnp.dot(cb, gw_ref[1, h], preferred_element_type=F32))
    r_gate = _sigmoid(jnp.concatenate(r_parts, axis=1) + gb_ref[0:1, :])
    i_gate = _sigmoid(jnp.concatenate(i_parts, axis=1) + gb_ref[1:2, :])
    a = jnp.exp((-LRU_C) * r_gate * _softplus(-lam_ref[...]))
    a_s[...] = a
    b_s[...] = jnp.sqrt(1.0 - a * a) * i_gate * conv
    yield

    def body(t, h):
        h = a_s[pl.ds(t, 1), :] * h + b_s[pl.ds(t, 1), :]
        h_s[pl.ds(t, 1), :] = h
        return h

    hcar[...] = lax.fori_loop(0, ts, body, hcar[...], unroll=True)
    yield
    out =_bdot(h_s[...] * yb, wout_ref[...]) + bout_ref[...]
    return x + _rms(out, g_ref[1:2, :])


def _xattn_block(x, k_ref, v_ref, g_ref, wq_ref, wo_ref):
    hn = _rms(x, g_ref[0:1, :])
    q = _bdot(hn, wq_ref[...]).astype(BF16)
    yield
    outs = []
    for h in range(MEM_HEADS):
        sl = slice(h * MEM_HEAD_DIM, (h + 1) * MEM_HEAD_DIM)
        sc = lax.dot_general(q[:, sl], k_ref[:, sl], (((1,), (1,)), ((), ())),
                             preferred_element_type=F32)
        p = jnp.exp(sc - jnp.max(sc, axis=-1, keepdims=True))
        l = jnp.sum(p, axis=-1, keepdims=True)
        outs.append(jnp.dot(p.astype(BF16), v_ref[:, sl], preferred_element_type=F32) / l)
    yield
    c = _bdot(jnp.concatenate(outs, axis=1), wo_ref[...])
    return x + _rms(c, g_ref[1:2, :])


def _mlp_block(x, g_ref, wup_ref, wdn_ref):
    hn = _rms(x, g_ref[0:1, :]).astype(BF16)
    acc = None
    for c in range(MLP_PARTS):
        sl = slice(c * (D_FF // MLP_PARTS), (c + 1) * (D_FF // MLP_PARTS))
        h = jnp.maximum(jnp.dot(hn, wup_ref[:, sl], preferred_element_type=F32), 0.0)
        part = jnp.dot((h * h).astype(BF16), wdn_ref[sl, :], preferred_element_type=F32)
        acc = part if acc is None else acc + part
        if c + 1 < MLP_PARTS:
            yield
    return x + _rms(acc, g_ref[1:2, :])


def _xattn_mlp_block(x, k_ref, v_ref, gains_ref, wq_ref, wo_ref, wup_ref, wdn_ref):
    x1 = yield from _xattn_block(x, k_ref, v_ref, gains_ref.at[2:4], wq_ref, wo_ref)
    yield
    return (yield from _mlp_block(x1, gains_ref.at[4:6], wup_ref, wdn_ref))


_LAYER0_ORDER = ("b", "a", "b", "b") + ("a", "b") * MLP_PARTS


def _layer0_kernel(x_ref, k_ref, v_ref, gains_ref, convw_ref, convb_ref, win_ref, bin_ref, gw_ref, gb_ref,
                   lam_ref, wout_ref, bout_ref, wq_ref, wo_ref, wup_ref, wdn_ref, o_ref,
                   ubuf, a_s, b_s, h_s, hcar, mid, *, tiles_per_seq):
    i = pl.program_id(0)

    @pl.when(i == 0)
    def _():
        mid[...] = jnp.zeros(mid.shape, F32)

    @pl.when(i % tiles_per_seq == 0)
    def _():
        ubuf[...] = jnp.zeros(ubuf.shape, F32)
        hcar[...] = jnp.zeros(hcar.shape, F32)

    done = _run_interleaved(_LAYER0_ORDER, {
        "a": _rglru_block(x_ref[0], gains_ref.at[0:2], convw_ref, convb_ref, win_ref, bin_ref, gw_ref,
                          gb_ref, lam_ref, wout_ref, bout_ref, ubuf, a_s, b_s, h_s, hcar),
        "b": _xattn_mlp_block(mid[...], k_ref, v_ref, gains_ref, wq_ref, wo_ref, wup_ref, wdn_ref),
    })
    o_ref[0] = done["b"]
    mid[...] = done["a"]


def _layer0(x, kv, layer, gains, conv_w, conv_b, w_in_bf, b_in, gate_w_bf, gate_b, lam, w_out_bf, b_out,
            w_q_bf, w_o_bf, w_up_bf, w_down_bf):
    B, S, D = x.shape
    M = kv.shape[2]
    ts = min(SEQ_TILE, S)
    n = S // ts
    last = B * n - 1

    def cur(i):
        j = jnp.minimum(i, last)
        return j // n, j % n

    def prev(i):
        j = jnp.maximum(i - 1, 0)
        return j // n, j % n

    weights = [conv_w, conv_b, w_in_bf, b_in, gate_w_bf, gate_b, lam, w_out_bf, b_out,
               w_q_bf, w_o_bf, w_up_bf, w_down_bf]
    return pl.pallas_call(
        functools.partial(_layer0_kernel, tiles_per_seq=n),
        grid=(B * n + 1,),
        in_specs=[
            pl.BlockSpec((1, ts, D), lambda i: (*cur(i), 0)),
            pl.BlockSpec((None, None, M, D), lambda i: (layer, prev(i)[0], 0, 0)),
            pl.BlockSpec((None, None, M, D), lambda i: (layer, prev(i)[0], 0, 1)),
            _const_spec(gains.shape),
        ] + [_const_spec(w.shape, single=True) for w in weights],
        out_specs=pl.BlockSpec((1, ts, D), lambda i: (*prev(i), 0)),
        out_shape=jax.ShapeDtypeStruct((B, S, D), F32),
        scratch_shapes=[
            pltpu.VMEM((SUBLANES, D), F32),
            pltpu.VMEM((ts, D), F32),
            pltpu.VMEM((ts, D), F32),
            pltpu.VMEM((ts, D), F32),
            pltpu.VMEM((1, D), F32),
            pltpu.VMEM((ts, D), F32),
        ],
        compiler_params=_cparams("arbitrary"),
        name="layer0",
    )(x, kv, kv, gains, *weights)


def _head_sum(x, e_ref, exact=True):
    hi = x.astype(BF16)
    lo = (x - hi.astype(F32)).astype(BF16) if exact else None
    e = e_ref[...]
    parts = []
    for q in range(N_QUADS):
        sl = slice(q * QUAD, (q + 1) * QUAD)
        part = jnp.dot(hi[:, sl], e, preferred_element_type=F32)
        if exact:
            part = part + jnp.dot(lo[:, sl], e, preferred_element_type=F32)
        parts.append(part)
    return jnp.concatenate(parts, axis=1)


def _rwkv_pre_kernel(x_ref, g_ref, mu_ref, wrkv_ref, w0_ref, w1_ref, w2_ref, a0_ref, a1_ref, a2_ref,
                     g1_ref, g2_ref, kk_ref, ka_ref, rk_ref, e_ref,
                     r_out, c_out, k_out, v_out, a_out, b_out, g_out, bonus_out, hbuf, *, ts):
    s = pl.program_id(1)

    @pl.when(s == 0)
    def _():
        hbuf[...] = jnp.zeros((SUBLANES, D_MODEL), F32)

    hn = _rms(x_ref[0], g_ref[...])
    xx = (_shift_rows(hn, hbuf[...], 1) - hn).astype(BF16)
    hbuf[...] = hn[ts - SUBLANES:, :]
    hn_bf = hn.astype(BF16)
    mu_bf = mu_ref[...].astype(BF16)

    def mix(i):
        return hn_bf + xx * mu_bf[i:i + 1, :]

    r = _bdot(mix(0), wrkv_ref[0])
    k = _bdot(mix(2), wrkv_ref[1])
    v = _bdot(mix(3), wrkv_ref[2])
    z = w0_ref[...] + _bdot(jnp.tanh(_bdot(mix(1), w1_ref[...])), w2_ref[...])
    lw = DECAY_LOG_SCALE * _sigmoid(z)
    alr = _sigmoid(a0_ref[...] + _bdot(_bdot(mix(4), a1_ref[...]), a2_ref[...]))
    g = _bdot(_sigmoid(_bdot(mix(5), g1_ref[...])), g2_ref[...])

    kk = k * kk_ref[...]
    kk = kk * lax.rsqrt(jnp.maximum(_head_sum(kk * kk, e_ref, exact=False), 1e-24))
    ka = ka_ref[...]
    k = k * ((1.0 - ka) + alr * ka)
    L = min(SCAN_CHUNK, ts)
    ti = lax.broadcasted_iota(jnp.int32, (ts, ts), 0)
    si = lax.broadcasted_iota(jnp.int32, (ts, ts), 1)
    tri = ((ti // L == si // L) & (ti >= si)).astype(BF16)
    lw_hi = lw.astype(BF16)
    lw_lo = (lw - lw_hi.astype(F32)).astype(BF16)
    r_out[0] = r.astype(BF16)
    c_out[0] = (jnp.dot(tri, lw_hi, preferred_element_type=F32)
                + jnp.dot(tri, lw_lo, preferred_element_type=F32))
    k_out[0] = k.astype(BF16)
    v_out[0] = v.astype(BF16)
    a_out[0] = (-kk).astype(BF16)
    b_out[0] = (kk * alr).astype(BF16)
    g_out[0] = g.astype(BF16)
    bonus_out[0] = (_head_sum(r * k * rk_ref[...], e_ref, exact=False) * v).astype(BF16)


def _rwkv_pre(x, g0, mu, w_rkv_bf, w0, w1_bf, w2_bf, a0, a1_bf, a2_bf, g1_bf, g2_bf, k_k, k_a, r_k, e):
    B, S, D = x.shape
    ts = min(256, S)
    tile = pl.BlockSpec((1, ts, D), lambda b, s: (b, s, 0))
    consts = [g0, mu, w_rkv_bf, w0, w1_bf, w2_bf, a0, a1_bf, a2_bf, g1_bf, g2_bf, k_k, k_a, r_k, e]
    return pl.pallas_call(
        functools.partial(_rwkv_pre_kernel, ts=ts),
        grid=(B, S // ts),
        in_specs=[tile] + [_const_spec(c.shape) for c in consts],
        out_specs=[tile] * 8,
        out_shape=[jax.ShapeDtypeStruct((B, S, D), F32 if i == 1 else BF16) for i in range(8)],
        scratch_shapes=[pltpu.VMEM((SUBLANES, D), F32)],
        compiler_params=_cparams("arbitrary", "arbitrary"),
        name="rwkv_pre",
    )(x, *consts)


_BNN = (((2,), (1,)), ((0,), (0,)))
_BNT = (((2,), (2,)), ((0,), (0,)))
_BTN = (((1,), (1,)), ((0,), (0,)))


def _qdot(a, b, dims=_BNN):
    return lax.dot_general(a.astype(SCAN_DTYPE), b.astype(SCAN_DTYPE), dims, preferred_element_type=F32)


def _rwkv_scan_kernel(r_ref, c_ref, k_ref, v_ref, a_ref, b_ref, y_ref, state, *, L):
    @pl.when(pl.program_id(1) == 0)
    def _():
        state[...] = jnp.zeros(state.shape, F32)

    G = c_ref.shape[0]
    c = c_ref[...]
    first_row = lax.broadcasted_iota(jnp.int32, c.shape, 1) == 0
    c_prev = jnp.where(first_row, 0.0, pltpu.roll(c, shift=1, axis=1))
    c_end = c[:, L - 1:L, :]
    gam = jnp.exp(c)
    gam_inv = jnp.exp(-c)
    gam_end = jnp.exp(c_end - c)
    b_in = b_ref[...].astype(F32)
    k_in = k_ref[...].astype(F32)
    r_t = r_ref[...].astype(F32) * gam
    a_t = a_ref[...].astype(F32) * jnp.exp(c_prev)
    b_t = b_in * gam_inv
    k_t = k_in * gam_inv
    b_h = b_in * gam_end
    k_h = k_in * gam_end
    gam_last = jnp.exp(c_end)

    hq = QUAD // RWKV_HEAD
    t_idx = lax.broadcasted_iota(jnp.int32, (L, hq * L), 0)
    s_idx = lax.broadcasted_iota(jnp.int32, (L, hq * L), 1) % L
    m_strict = t_idx > s_idx
    m_incl = t_idx >= s_idx
    eye = (t_idx == s_idx).astype(F32)
    m_off = []
    m = 1
    while m < L:
        m_off.append(((t_idx // (2 * m)) == (s_idx // (2 * m))) & ((t_idx % (2 * m)) >= m)
                     & ((s_idx % (2 * m)) < m))
        m *= 2
    qrow = lax.broadcasted_iota(jnp.int32, (QUAD, QUAD), 0)
    qcol = lax.broadcasted_iota(jnp.int32, (QUAD, QUAD), 1)
    m_state = (qrow // RWKV_HEAD) == (qcol // RWKV_HEAD)
    lane_head = lax.broadcasted_iota(jnp.int32, (L, QUAD), 1) // RWKV_HEAD
    lane_blk = lax.broadcasted_iota(jnp.int32, (L, hq * L), 1) // L

    def quads(z):
        return jnp.stack([z[g][:, q * QUAD:(q + 1) * QUAD] for g in range(G) for q in range(N_QUADS)], axis=0)

    def bdiag(z, blk):
        z = z.astype(SCAN_DTYPE)
        zero = jnp.zeros_like(z)
        return jnp.concatenate([jnp.where(blk == h, z, zero) for h in range(hq)], axis=1)

    a_q, r_q, v_q, b_q, k_q = quads(a_t), quads(r_t), quads(v_ref[...]), quads(b_t), quads(k_t)
    ar = jnp.concatenate([a_q, r_q], axis=1)
    g_b = _qdot(ar, bdiag(b_q, lane_head), _BNT)
    g_k = _qdot(ar, bdiag(k_q, lane_head), _BNT)
    n_ab = g_b[:, :L]
    a_rb = jnp.where(m_incl, g_b[:, L:], 0.0)
    a_ak = jnp.where(m_strict, g_k[:, :L], 0.0)
    a_rk = jnp.where(m_incl, g_k[:, L:], 0.0)
    tinv = eye + jnp.where(m_off[0], n_ab, 0.0)
    for m_lvl in m_off[1:]:
        n_off = jnp.where(m_lvl, n_ab, 0.0)
        tinv = tinv + _qdot(tinv, bdiag(_qdot(n_off, bdiag(tinv, lane_blk)), lane_blk))
    s0 = state[...]
    xs = _qdot(ar, s0, _BNT)
    vs = _qdot(jnp.concatenate([a_ak, a_rk], axis=1), bdiag(v_q, lane_head))
    u = _qdot(tinv, bdiag(xs[:, :L] + vs[:, :L], lane_head))
    y = xs[:, L:] + vs[:, L:] + _qdot(a_rb, bdiag(u, lane_head))
    upd = _qdot(jnp.concatenate([u, v_q], axis=1), jnp.concatenate([quads(b_h), quads(k_h)], axis=1), _BTN)
    state[...] = s0 * quads(gam_last) + jnp.where(m_state, upd, 0.0)
    for g in range(G):
        y_ref[g] = jnp.concatenate([y[g * N_QUADS + q] for q in range(N_QUADS)], axis=1)


def _rwkv_scan(r, c, k, v, a, b):
    B, S, D = r.shape
    L = min(SCAN_CHUNK, S)
    G = SCAN_ROWS if B % SCAN_ROWS == 0 else 1
    tile = pl.BlockSpec((G, L, D), lambda bb, c: (bb, c, 0))
    return pl.pallas_call(
        functools.partial(_rwkv_scan_kernel, L=L),
        grid=(B // G, S // L),
        in_specs=[tile] * 6,
        out_specs=tile,
        out_shape=jax.ShapeDtypeStruct((B, S, D), F32),
        scratch_shapes=[pltpu.VMEM((G * N_QUADS, QUAD, QUAD), F32)],
        compiler_params=_cparams("arbitrary", "arbitrary"),
        name="rwkv_scan",
    )(r, c, k, v, a, b)


def _rwkv_tail_kernel(x_ref, y_ref, bonus_ref, gate_ref, k_ref, v_ref, gains_ref, gng_ref, gnb_ref, e_ref,
                      wmix_ref, wq_ref, wo_ref, wup_ref, wdn_ref, o_ref):
    y = y_ref[0]
    mean = _head_sum(y, e_ref)
    d = y - mean
    var = _head_sum(d * d, e_ref, exact=False)
    yn = d * lax.rsqrt(var + RWKV_GN_EPS) * gng_ref[...] + gnb_ref[...]
    t = _bdot((yn + bonus_ref[0]) * gate_ref[0], wmix_ref[...])
    x1 = x_ref[0] + _rms(t, gains_ref[1:2, :])
    o_ref[0] = _run(_xattn_mlp_block(x1, k_ref, v_ref, gains_ref, wq_ref, wo_ref, wup_ref, wdn_ref))


def _rwkv_tail(x, y, bonus, gate, kv, layer, gains, gn_g, gn_b, e_mean, w_mix_bf, w_q_bf, w_o_bf, w_up_bf,
               w_down_bf):
    B, S, D = x.shape
    M = kv.shape[2]
    ts = min(TAIL_TILE, S)
    tile = pl.BlockSpec((1, ts, D), lambda b, s: (b, s, 0))
    weights = [w_mix_bf, w_q_bf, w_o_bf, w_up_bf, w_down_bf]
    return pl.pallas_call(
        _rwkv_tail_kernel,
        grid=(B, S // ts),
        in_specs=[tile] * 4 + [
            pl.BlockSpec((None, None, M, D), lambda b, s: (layer, b, 0, 0)),
            pl.BlockSpec((None, None, M, D), lambda b, s: (layer, b, 0, 1)),
            _const_spec(gains.shape), _const_spec(gn_g.shape), _const_spec(gn_b.shape),
            _const_spec(e_mean.shape),
        ] + [_const_spec(w.shape, single=True) for w in weights],
        out_specs=tile,
        out_shape=jax.ShapeDtypeStruct((B, S, D), F32),
        compiler_params=_cparams("arbitrary", "arbitrary"),
        name="rwkv_tail",
    )(x, y, bonus, gate, kv, kv, gains, gn_g, gn_b, e_mean, *weights)


def _head_selector(value):
    head = jnp.arange(QUAD) // RWKV_HEAD
    return jnp.where(head[:, None] == head[None, :], value, 0.0).astype(BF16)


def kernel(x, mem, ln_gains, mem_norm, a_conv_w, a_conv_b, a_w_in, a_b_in, a_gate_w, a_gate_b, a_lambda, a_w_out, a_b_out, b_mu, b_w_rkv, b_w0, b_w1, b_w2, b_a0, b_a1, b_a2, b_g1, b_g2, b_k_k, b_k_a, b_r_k, b_gn_g, b_gn_b, b_w_o, c_w_q, c_w_kv, c_w_o, m_w_up, m_w_down):
    B, S, D = x.shape
    M = mem.shape[1]
    depth = ln_gains.shape[0]
    assert D == D_MODEL and depth == 2
    bf = lambda w: w.astype(BF16)
    row = lambda p: p.reshape(1, -1)

    kv = _mem_kv(mem.reshape(B * M, D), row(mem_norm), bf(c_w_kv)).reshape(depth, B, M, 2 * D)
    w_q_scaled = bf(c_w_q * (MEM_HEAD_DIM ** -0.5))
    e = _head_selector(1.0)
    e_mean = _head_selector(1.0 / RWKV_HEAD)

    for i in range(depth):
        g = ln_gains[i]
        j = i // 2
        if i % 2 == 0:
            x = _layer0(x, kv, i, g, a_conv_w[j], row(a_conv_b[j]), bf(a_w_in[j]), row(a_b_in[j]),
                        bf(a_gate_w[j]), a_gate_b[j].reshape(2, D), row(a_lambda[j]), bf(a_w_out[j]),
                        row(a_b_out[j]), w_q_scaled[i], bf(c_w_o[i]), bf(m_w_up[i]), bf(m_w_down[i]))
        else:
            r, c, k, v, a, b, gate, bonus = _rwkv_pre(
                x, g[0:1], b_mu[j], bf(b_w_rkv[j]), row(b_w0[j]), bf(b_w1[j]), bf(b_w2[j]), row(b_a0[j]),
                bf(b_a1[j]), bf(b_a2[j]), bf(b_g1[j]), bf(b_g2[j]), row(b_k_k[j]), row(b_k_a[j]),
                row(b_r_k[j]), e)
            y = _rwkv_scan(r, c, k, v, a, b)
            x = _rwkv_tail(x, y, bonus, gate, kv, i, g, row(b_gn_g[j]), row(b_gn_b[j]), e_mean, bf(b_w_o[j]),
                           w_q_scaled[i], bf(c_w_o[i]), bf(m_w_up[i]), bf(m_w_down[i]))
    return x
```

```python
import functools

import jax
import jax.numpy as jnp
from jax import lax
from jax.experimental import pallas as pl
from jax.experimental.pallas import tpu as pltpu

F32 = jnp.float32
BF16 = jnp.bfloat16

D_MODEL = 1024
RMS_EPS = 1e-6
LRU_HEADS = 4
LRU_BLOCK = D_MODEL // LRU_HEADS
CONV_WIDTH = 4
LRU_C = 8.0
RWKV_HEAD = 64
RWKV_HEADS = D_MODEL // RWKV_HEAD
RWKV_GN_EPS = 64e-5
DECAY_LOG_SCALE = -0.6065306597126334
MEM_HEADS = 4
MEM_HEAD_DIM = D_MODEL // MEM_HEADS
D_FF = 4 * D_MODEL

LANES = 128
SUBLANES = 8
QUAD = 256
N_QUADS = D_MODEL // QUAD
SEQ_TILE = 256
TAIL_TILE = 512
MLP_PARTS = 4
SCAN_CHUNK = 64
LEVEL_GROUP = 3
SCAN_ROWS = 4
VMEM_LIMIT = 56 * 1024 * 1024

SCAN_DTYPE = BF16


def _cparams(*sem):
    return pltpu.CompilerParams(dimension_semantics=sem, vmem_limit_bytes=VMEM_LIMIT)


def _rms(x, g):
    return x * lax.rsqrt(jnp.mean(x * x, axis=-1, keepdims=True) + RMS_EPS) * g


def _sigmoid(x):
    return 1.0 / (1.0 + jnp.exp(-x))


def _softplus(x):
    return jnp.maximum(x, 0.0) + jnp.log(1.0 + jnp.exp(-jnp.abs(x)))


def _gelu_tanh(x):
    return 0.5 * x * (1.0 + jnp.tanh(0.7978845608028654 * (x + 0.044715 * (x * x * x))))


def _bdot(a, b):
    return jnp.dot(a.astype(BF16), b.astype(BF16), preferred_element_type=F32)


def _shift_rows(x, tail, k):
    rolled = pltpu.roll(x, shift=k, axis=0)
    row = lax.broadcasted_iota(jnp.int32, (SUBLANES, x.shape[1]), 0)
    head = jnp.where(row < k, pltpu.roll(tail, shift=k, axis=0), rolled[:SUBLANES])
    return jnp.concatenate([head, rolled[SUBLANES:]], axis=0)


def _const_spec(shape, single=False):
    nd = len(shape)
    if single:
        return pl.BlockSpec(shape, lambda *_: (0,) * nd, pipeline_mode=pl.Buffered(1))
    return pl.BlockSpec(shape, lambda *_: (0,) * nd)


def _memkv_kernel(mem_ref, gn_ref, wkv_ref, out_ref):
    mn = _rms(mem_ref[...], gn_ref[...])
    out_ref[...] = _bdot(mn, wkv_ref[...]).astype(BF16)


def _mem_kv(mem2d, mem_norm, w_kv_bf):
    rows = mem2d.shape[0]
    depth = w_kv_bf.shape[0]
    tm = min(512, rows)
    return pl.pallas_call(
        _memkv_kernel,
        grid=(depth, rows // tm),
        in_specs=[
            pl.BlockSpec((tm, D_MODEL), lambda i, r: (r, 0)),
            _const_spec((1, D_MODEL)),
            pl.BlockSpec((None, D_MODEL, 2 * D_MODEL), lambda i, r: (i, 0, 0)),
        ],
        out_specs=pl.BlockSpec((None, tm, 2 * D_MODEL), lambda i, r: (i, r, 0)),
        out_shape=jax.ShapeDtypeStruct((depth, rows, 2 * D_MODEL), BF16),
        compiler_params=_cparams("arbitrary", "arbitrary"),
        name="mem_kv",
    )(mem2d, mem_norm, w_kv_bf)


def _run(gen):
    while True:
        try:
            next(gen)
        except StopIteration as stop:
            return stop.value


def _run_interleaved(order, gens):
    done = {}
    for name in order:
        try:
            next(gens[name])
        except StopIteration as stop:
            done[name] = stop.value
    return done


def _rglru_block(x, g_ref, convw_ref, convb_ref, win_ref, bin_ref, gw_ref, gb_ref, lam_ref,
                 wout_ref, bout_ref, ubuf, a_s, b_s, h_s, hcar):
    ts = x.shape[0]
    hn = _rms(x, g_ref[0:1, :])
    proj = _bdot(hn, win_ref[...]) + bin_ref[...]
    yield
    yb = _gelu_tanh(proj[:, :D_MODEL])
    u = proj[:, D_MODEL:]
    tail = ubuf[...]
    conv = convb_ref[...] + u * convw_ref[CONV_WIDTH - 1:CONV_WIDTH, :]
    for tap in range(CONV_WIDTH - 1):
        conv = conv + _shift_rows(u, tail, CONV_WIDTH - 1 - tap) * convw_ref[tap:tap + 1, :]
    ubuf[...] = u[ts - SUBLANES:, :]
    yield
    conv_bf = conv.astype(BF16)
    r_parts, i_parts = [], []
    for h in range(LRU_HEADS):
        cb = conv_bf[:, h * LRU_BLOCK:(h + 1) * LRU_BLOCK]
        r_parts.append(jnp.dot(cb, gw_ref[0, h], preferred_element_type=F32))
        i_parts.append(jnp.dot(cb, gw_ref[1, h], preferred_element_type=F32))
    r_gate = _sigmoid(jnp.concatenate(r_parts, axis=1) + gb_ref[0:1, :])
    i_gate = _sigmoid(jnp.concatenate(i_parts, axis=1) + gb_ref[1:2, :])
    a = jnp.exp((-LRU_C) * r_gate * _softplus(-lam_ref[...]))
    a_s[...] = a
    b_s[...] = jnp.sqrt(1.0 - a * a) * i_gate * conv
    yield

    def body(t, h):
        h = a_s[pl.ds(t, 1), :] * h + b_s[pl.ds(t, 1), :]
        h_s[pl.ds(t, 1), :] = h
        return h

    hcar[...] = lax.fori_loop(0, ts, body, hcar[...], unroll=True)
    yield
    out =_bdot(h_s[...] * yb, wout_ref[...]) + bout_ref[...]
    return x + _rms(out, g_ref[1:2, :])


def _xattn_block(x, k_ref, v_ref, g_ref, wq_ref, wo_ref):
    hn = _rms(x, g_ref[0:1, :])
    q = _bdot(hn, wq_ref[...]).astype(BF16)
    yield
    outs = []
    for h in range(MEM_HEADS):
        sl = slice(h * MEM_HEAD_DIM, (h + 1) * MEM_HEAD_DIM)
        sc = lax.dot_general(q[:, sl], k_ref[:, sl], (((1,), (1,)), ((), ())),
                             preferred_element_type=F32)
        p = jnp.exp(sc - jnp.max(sc, axis=-1, keepdims=True))
        l = jnp.sum(p, axis=-1, keepdims=True)
        outs.append(jnp.dot(p.astype(BF16), v_ref[:, sl], preferred_element_type=F32) / l)
    yield
    c = _bdot(jnp.concatenate(outs, axis=1), wo_ref[...])
    return x + _rms(c, g_ref[1:2, :])


def _mlp_block(x, g_ref, wup_ref, wdn_ref):
    hn = _rms(x, g_ref[0:1, :]).astype(BF16)
    acc = None
    for c in range(MLP_PARTS):
        sl = slice(c * (D_FF // MLP_PARTS), (c + 1) * (D_FF // MLP_PARTS))
        h = jnp.maximum(jnp.dot(hn, wup_ref[:, sl], preferred_element_type=F32), 0.0)
        part = jnp.dot((h * h).astype(BF16), wdn_ref[sl, :], preferred_element_type=F32)
        acc = part if acc is None else acc + part
        if c + 1 < MLP_PARTS:
            yield
    return x + _rms(acc, g_ref[1:2, :])


def _xattn_mlp_block(x, k_ref, v_ref, gains_ref, wq_ref, wo_ref, wup_ref, wdn_ref):
    x1 = yield from _xattn_block(x, k_ref, v_ref, gains_ref.at[2:4], wq_ref, wo_ref)
    yield
    return (yield from _mlp_block(x1, gains_ref.at[4:6], wup_ref, wdn_ref))


_LAYER0_ORDER = ("b", "a", "b", "b") + ("a", "b") * MLP_PARTS


def _layer0_kernel(x_ref, k_ref, v_ref, gains_ref, convw_ref, convb_ref, win_ref, bin_ref, gw_ref, gb_ref,
                   lam_ref, wout_ref, bout_ref, wq_ref, wo_ref, wup_ref, wdn_ref, o_ref,
                   ubuf, a_s, b_s, h_s, hcar, mid, *, tiles_per_seq):
    i = pl.program_id(0)

    @pl.when(i == 0)
    def _():
        mid[...] = jnp.zeros(mid.shape, F32)

    @pl.when(i % tiles_per_seq == 0)
    def _():
        ubuf[...] = jnp.zeros(ubuf.shape, F32)
        hcar[...] = jnp.zeros(hcar.shape, F32)

    done = _run_interleaved(_LAYER0_ORDER, {
        "a": _rglru_block(x_ref[0], gains_ref.at[0:2], convw_ref, convb_ref, win_ref, bin_ref, gw_ref,
                          gb_ref, lam_ref, wout_ref, bout_ref, ubuf, a_s, b_s, h_s, hcar),
        "b": _xattn_mlp_block(mid[...], k_ref, v_ref, gains_ref, wq_ref, wo_ref, wup_ref, wdn_ref),
    })
    o_ref[0] = done["b"]
    mid[...] = done["a"]


def _layer0(x, kv, layer, gains, conv_w, conv_b, w_in_bf, b_in, gate_w_bf, gate_b, lam, w_out_bf, b_out,
            w_q_bf, w_o_bf, w_up_bf, w_down_bf):
    B, S, D = x.shape
    M = kv.shape[2]
    ts = min(SEQ_TILE, S)
    n = S // ts
    last = B * n - 1

    def cur(i):
        j = jnp.minimum(i, last)
        return j // n, j % n

    def prev(i):
        j = jnp.maximum(i - 1, 0)
        return j // n, j % n

    weights = [conv_w, conv_b, w_in_bf, b_in, gate_w_bf, gate_b, lam, w_out_bf, b_out,
               w_q_bf, w_o_bf, w_up_bf, w_down_bf]
    return pl.pallas_call(
        functools.partial(_layer0_kernel, tiles_per_seq=n),
        grid=(B * n + 1,),
        in_specs=[
            pl.BlockSpec((1, ts, D), lambda i: (*cur(i), 0)),
            pl.BlockSpec((None, None, M, D), lambda i: (layer, prev(i)[0], 0, 0)),
            pl.BlockSpec((None, None, M, D), lambda i: (layer, prev(i)[0], 0, 1)),
            _const_spec(gains.shape),
        ] + [_const_spec(w.shape, single=True) for w in weights],
        out_specs=pl.BlockSpec((1, ts, D), lambda i: (*prev(i), 0)),
        out_shape=jax.ShapeDtypeStruct((B, S, D), F32),
        scratch_shapes=[
            pltpu.VMEM((SUBLANES, D), F32),
            pltpu.VMEM((ts, D), F32),
            pltpu.VMEM((ts, D), F32),
            pltpu.VMEM((ts, D), F32),
            pltpu.VMEM((1, D), F32),
            pltpu.VMEM((ts, D), F32),
        ],
        compiler_params=_cparams("arbitrary"),
        name="layer0",
    )(x, kv, kv, gains, *weights)


def _head_sum(x, e_ref, exact=True):
    hi = x.astype(BF16)
    lo = (x - hi.astype(F32)).astype(BF16) if exact else None
    e = e_ref[...]
    parts = []
    for q in range(N_QUADS):
        sl = slice(q * QUAD, (q + 1) * QUAD)
        part = jnp.dot(hi[:, sl], e, preferred_element_type=F32)
        if exact:
            part = part + jnp.dot(lo[:, sl], e, preferred_element_type=F32)
        parts.append(part)
    return jnp.concatenate(parts, axis=1)


def _rwkv_pre_kernel(x_ref, g_ref, mu_ref, wrkv_ref, w0_ref, w1_ref, w2_ref, a0_ref, a1_ref, a2_ref,
                     g1_ref, g2_ref, kk_ref, ka_ref, rk_ref, e_ref,
                     r_out, c_out, k_out, v_out, a_out, b_out, g_out, bonus_out, hbuf, *, ts):
    s = pl.program_id(1)

    @pl.when(s == 0)
    def _():
        hbuf[...] = jnp.zeros((SUBLANES, D_MODEL), F32)

    hn = _rms(x_ref[0], g_ref[...])
    xx = (_shift_rows(hn, hbuf[...], 1) - hn).astype(BF16)
    hbuf[...] = hn[ts - SUBLANES:, :]
    hn_bf = hn.astype(BF16)
    mu_bf = mu_ref[...].astype(BF16)

    def mix(i):
        return hn_bf + xx * mu_bf[i:i + 1, :]

    r = _bdot(mix(0), wrkv_ref[0])
    k = _bdot(mix(2), wrkv_ref[1])
    v = _bdot(mix(3), wrkv_ref[2])
    z = w0_ref[...] + _bdot(jnp.tanh(_bdot(mix(1), w1_ref[...])), w2_ref[...])
    lw = DECAY_LOG_SCALE * _sigmoid(z)
    alr = _sigmoid(a0_ref[...] + _bdot(_bdot(mix(4), a1_ref[...]), a2_ref[...]))
    g = _bdot(_sigmoid(_bdot(mix(5), g1_ref[...])), g2_ref[...])

    kk = k * kk_ref[...]
    kk = kk * lax.rsqrt(jnp.maximum(_head_sum(kk * kk, e_ref, exact=False), 1e-24))
    ka = ka_ref[...]
    k = k * ((1.0 - ka) + alr * ka)
    L = min(SCAN_CHUNK, ts)
    ti = lax.broadcasted_iota(jnp.int32, (ts, ts), 0)
    si = lax.broadcasted_iota(jnp.int32, (ts, ts), 1)
    tri = ((ti // L == si // L) & (ti >= si)).astype(BF16)
    lw_hi = lw.astype(BF16)
    lw_lo = (lw - lw_hi.astype(F32)).astype(BF16)
    r_out[0] = r.astype(BF16)
    c_out[0] = (jnp.dot(tri, lw_hi, preferred_element_type=F32)
                + jnp.dot(tri, lw_lo, preferred_element_type=F32))
    k_out[0] = k.astype(BF16)
    v_out[0] = v.astype(BF16)
    a_out[0] = (-kk).astype(BF16)
    b_out[0] = (kk * alr).astype(BF16)
    g_out[0] = g.astype(BF16)
    bonus_out[0] = (_head_sum(r * k * rk_ref[...], e_ref, exact=False) * v).astype(BF16)


def _rwkv_pre(x, g0, mu, w_rkv_bf, w0, w1_bf, w2_bf, a0, a1_bf, a2_bf, g1_bf, g2_bf, k_k, k_a, r_k, e):
    B, S, D = x.shape
    ts = min(256, S)
    tile = pl.BlockSpec((1, ts, D), lambda b, s: (b, s, 0))
    consts = [g0, mu, w_rkv_bf, w0, w1_bf, w2_bf, a0, a1_bf, a2_bf, g1_bf, g2_bf, k_k, k_a, r_k, e]
    return pl.pallas_call(
        functools.partial(_rwkv_pre_kernel, ts=ts),
        grid=(B, S // ts),
        in_specs=[tile] + [_const_spec(c.shape) for c in consts],
        out_specs=[tile] * 8,
        out_shape=[jax.ShapeDtypeStruct((B, S, D), F32 if i == 1 else BF16) for i in range(8)],
        scratch_shapes=[pltpu.VMEM((SUBLANES, D), F32)],
        compiler_params=_cparams("arbitrary", "arbitrary"),
        name="rwkv_pre",
    )(x, *consts)


_BNN = (((2,), (1,)), ((0,), (0,)))
_BNT = (((2,), (2,)), ((0,), (0,)))
_BTN = (((1,), (1,)), ((0,), (0,)))


def _qdot(a, b, dims=_BNN):
    return lax.dot_general(a.astype(SCAN_DTYPE), b.astype(SCAN_DTYPE), dims, preferred_element_type=F32)


def _rwkv_scan_kernel(r_ref, c_ref, k_ref, v_ref, a_ref, b_ref, y_ref, state, *, L):
    @pl.when(pl.program_id(1) == 0)
    def _():
        state[...] = jnp.zeros(state.shape, F32)

    G = c_ref.shape[0]
    c = c_ref[...]
    first_row = lax.broadcasted_iota(jnp.int32, c.shape, 1) == 0
    c_prev = jnp.where(first_row, 0.0, pltpu.roll(c, shift=1, axis=1))
    c_end = c[:, L - 1:L, :]
    gam = jnp.exp(c)
    gam_inv = jnp.exp(-c)
    gam_end = jnp.exp(c_end - c)
    b_in = b_ref[...].astype(F32)
    k_in = k_ref[...].astype(F32)
    r_t = r_ref[...].astype(F32) * gam
    a_t = a_ref[...].astype(F32) * jnp.exp(c_prev)
    b_t = b_in * gam_inv
    k_t = k_in * gam_inv
    b_h = b_in * gam_end
    k_h = k_in * gam_end
    gam_last = jnp.exp(c_end)

    hq = QUAD // RWKV_HEAD
    t_idx = lax.broadcasted_iota(jnp.int32, (L, hq * L), 0)
    s_idx = lax.broadcasted_iota(jnp.int32, (L, hq * L), 1) % L
    m_strict = t_idx > s_idx
    m_incl = t_idx >= s_idx
    eye = (t_idx == s_idx).astype(F32)
    m_off = []
    m = 1
    while m < L:
        m_off.append(((t_idx // (2 * m)) == (s_idx // (2 * m))) & ((t_idx % (2 * m)) >= m)
                     & ((s_idx % (2 * m)) < m))
        m *= 2
    qrow = lax.broadcasted_iota(jnp.int32, (QUAD, QUAD), 0)
    qcol = lax.broadcasted_iota(jnp.int32, (QUAD, QUAD), 1)
    m_state = (qrow // RWKV_HEAD) == (qcol // RWKV_HEAD)
    lane_head = lax.broadcasted_iota(jnp.int32, (L, QUAD), 1) // RWKV_HEAD
    lane_blk = lax.broadcasted_iota(jnp.int32, (L, hq * L), 1) // L

    def quads(z):
        return jnp.stack([z[g][:, q * QUAD:(q + 1) * QUAD] for g in range(G) for q in range(N_QUADS)], axis=0)

    def bdiag(z, blk):
        z = z.astype(SCAN_DTYPE)
        zero = jnp.zeros_like(z)
        return jnp.concatenate([jnp.where(blk == h, z, zero) for h in range(hq)], axis=1)

    a_q, r_q, v_q, b_q, k_q = quads(a_t), quads(r_t), quads(v_ref[...]), quads(b_t), quads(k_t)
    ar = jnp.concatenate([a_q, r_q], axis=1)
    g_b = _qdot(ar, bdiag(b_q, lane_head), _BNT)
    g_k = _qdot(ar, bdiag(k_q, lane_head), _BNT)
    n_ab = g_b[:, :L]
    a_rb = jnp.where(m_incl, g_b[:, L:], 0.0)
    a_ak = jnp.where(m_strict, g_k[:, :L], 0.0)
    a_rk = jnp.where(m_incl, g_k[:, L:], 0.0)
    tinv = eye + jnp.where(m_off[0], n_ab, 0.0)
    levels = [jnp.where(m_lvl, n_ab, 0.0) for m_lvl in m_off[1:]]
    for start in range(0, len(levels), LEVEL_GROUP):
        group = levels[start:start + LEVEL_GROUP]
        z = _qdot(jnp.concatenate(group, axis=1), bdiag(tinv, lane_blk))
        pend = [z[:, i * L:(i + 1) * L] for i in range(len(group))]
        while pend:
            p_lvl = pend.pop(0)
            z = _qdot(jnp.concatenate([tinv] + pend, axis=1), bdiag(p_lvl, lane_blk))
            tinv = tinv + z[:, :L]
            pend = [q + z[:, (i + 1) * L:(i + 2) * L] for i, q in enumerate(pend)]
    s0 = state[...]
    xs = _qdot(ar, s0, _BNT)
    vs = _qdot(jnp.concatenate([a_ak, a_rk], axis=1), bdiag(v_q, lane_head))
    u = _qdot(tinv, bdiag(xs[:, :L] + vs[:, :L], lane_head))
    y = xs[:, L:] + vs[:, L:] + _qdot(a_rb, bdiag(u, lane_head))
    upd = _qdot(jnp.concatenate([u, v_q], axis=1), jnp.concatenate([quads(b_h), quads(k_h)], axis=1), _BTN)
    state[...] = s0 * quads(gam_last) + jnp.where(m_state, upd, 0.0)
    for g in range(G):
        y_ref[g] = jnp.concatenate([y[g * N_QUADS + q] for q in range(N_QUADS)], axis=1)


def _rwkv_scan(r, c, k, v, a, b):
    B, S, D = r.shape
    L = min(SCAN_CHUNK, S)
    G = SCAN_ROWS if B % SCAN_ROWS == 0 else 1
    tile = pl.BlockSpec((G, L, D), lambda bb, c: (bb, c, 0))
    return pl.pallas_call(
        functools.partial(_rwkv_scan_kernel, L=L),
        grid=(B // G, S // L),
        in_specs=[tile] * 6,
        out_specs=tile,
        out_shape=jax.ShapeDtypeStruct((B, S, D), F32),
        scratch_shapes=[pltpu.VMEM((G * N_QUADS, QUAD, QUAD), F32)],
        compiler_params=_cparams("arbitrary", "arbitrary"),
        name="rwkv_scan",
    )(r, c, k, v, a, b)


def _rwkv_tail_kernel(x_ref, y_ref, bonus_ref, gate_ref, k_ref, v_ref, gains_ref, gng_ref, gnb_ref, e_ref,
                      wmix_ref, wq_ref, wo_ref, wup_ref, wdn_ref, o_ref):
    y = y_ref[0]
    mean = _head_sum(y, e_ref)
    d = y - mean
    var = _head_sum(d * d, e_ref, exact=False)
    yn = d * lax.rsqrt(var + RWKV_GN_EPS) * gng_ref[...] + gnb_ref[...]
    t = _bdot((yn + bonus_ref[0]) * gate_ref[0], wmix_ref[...])
    x1 = x_ref[0] + _rms(t, gains_ref[1:2, :])
    o_ref[0] = _run(_xattn_mlp_block(x1, k_ref, v_ref, gains_ref, wq_ref, wo_ref, wup_ref, wdn_ref))


def _rwkv_tail(x, y, bonus, gate, kv, layer, gains, gn_g, gn_b, e_mean, w_mix_bf, w_q_bf, w_o_bf, w_up_bf,
               w_down_bf):
    B, S, D = x.shape
    M = kv.shape[2]
    ts = min(TAIL_TILE, S)
    tile = pl.BlockSpec((1, ts, D), lambda b, s: (b, s, 0))
    weights = [w_mix_bf, w_q_bf, w_o_bf, w_up_bf, w_down_bf]
    return pl.pallas_call(
        _rwkv_tail_kernel,
        grid=(B, S // ts),
        in_specs=[tile] * 4 + [
            pl.BlockSpec((None, None, M, D), lambda b, s: (layer, b, 0, 0)),
            pl.BlockSpec((None, None, M, D), lambda b, s: (layer, b, 0, 1)),
            _const_spec(gains.shape), _const_spec(gn_g.shape), _const_spec(gn_b.shape),
            _const_spec(e_mean.shape),
        ] + [_const_spec(w.shape, single=True) for w in weights],
        out_specs=tile,
        out_shape=jax.ShapeDtypeStruct((B, S, D), F32),
        compiler_params=_cparams("arbitrary", "arbitrary"),
        name="rwkv_tail",
    )(x, y, bonus, gate, kv, kv, gains, gn_g, gn_b, e_mean, *weights)


def _head_selector(value):
    head = jnp.arange(QUAD) // RWKV_HEAD
    return jnp.where(head[:, None] == head[None, :], value, 0.0).astype(BF16)


def kernel(x, mem, ln_gains, mem_norm, a_conv_w, a_conv_b, a_w_in, a_b_in, a_gate_w, a_gate_b, a_lambda, a_w_out, a_b_out, b_mu, b_w_rkv, b_w0, b_w1, b_w2, b_a0, b_a1, b_a2, b_g1, b_g2, b_k_k, b_k_a, b_r_k, b_gn_g, b_gn_b, b_w_o, c_w_q, c_w_kv, c_w_o, m_w_up, m_w_down):
    B, S, D = x.shape
    M = mem.shape[1]
    depth = ln_gains.shape[0]
    assert D == D_MODEL and depth == 2
    bf = lambda w: w.astype(BF16)
    row = lambda p: p.reshape(1, -1)

    kv = _mem_kv(mem.reshape(B * M, D), row(mem_norm), bf(c_w_kv)).reshape(depth, B, M, 2 * D)
    w_q_scaled = bf(c_w_q * (MEM_HEAD_DIM ** -0.5))
    e = _head_selector(1.0)
    e_mean = _head_selector(1.0 / RWKV_HEAD)

    for i in range(depth):
        g = ln_gains[i]
        j = i // 2
        if i % 2 == 0:
            x = _layer0(x, kv, i, g, a_conv_w[j], row(a_conv_b[j]), bf(a_w_in[j]), row(a_b_in[j]),
                        bf(a_gate_w[j]), a_gate_b[j].reshape(2, D), row(a_lambda[j]), bf(a_w_out[j]),
                        row(a_b_out[j]), w_q_scaled[i], bf(c_w_o[i]), bf(m_w_up[i]), bf(m_w_down[i]))
        else:
            r, c, k, v, a, b, gate, bonus = _rwkv_pre(
                x, g[0:1], b_mu[j], bf(b_w_rkv[j]), row(b_w0[j]), bf(b_w1[j]), bf(b_w2[j]), row(b_a0[j]),
                bf(b_a1[j]), bf(b_a2[j]), bf(b_g1[j]), bf(b_g2[j]), row(b_k_k[j]), row(b_k_a[j]),
                row(b_r_k[j]), e)
            y = _rwkv_scan(r, c, k, v, a, b)
            x = _rwkv_tail(x, y, bonus, gate, kv, i, g, row(b_gn_g[j]), row(b_gn_b[j]), e_mean, bf(b_w_o[j]),
                           w_q_scaled[i], bf(c_w_o[i]), bf(m_w_up[i]), bf(m_w_down[i]))
    return x
```

```python
import functools

import jax
import jax.numpy as jnp
from jax import lax
from jax.experimental import pallas as pl
from jax.experimental.pallas import tpu as pltpu

F32 = jnp.float32
BF16 = jnp.bfloat16

D_MODEL = 1024
RMS_EPS = 1e-6
LRU_HEADS = 4
LRU_BLOCK = D_MODEL // LRU_HEADS
CONV_WIDTH = 4
LRU_C = 8.0
RWKV_HEAD = 64
RWKV_HEADS = D_MODEL // RWKV_HEAD
RWKV_GN_EPS = 64e-5
DECAY_LOG_SCALE = -0.6065306597126334
MEM_HEADS = 4
MEM_HEAD_DIM = D_MODEL // MEM_HEADS
D_FF = 4 * D_MODEL

LANES = 128
SUBLANES = 8
QUAD = 256
N_QUADS = D_MODEL // QUAD
SEQ_TILE = 256
TAIL_TILE = 512
MLP_PARTS = 4
SCAN_CHUNK = 64
LEVEL_GROUP = 3
SCAN_ROWS = 4
VMEM_LIMIT = 56 * 1024 * 1024

SCAN_DTYPE = BF16


def _cparams(*sem):
    return pltpu.CompilerParams(dimension_semantics=sem, vmem_limit_bytes=VMEM_LIMIT)


def _rms(x, g):
    return x * lax.rsqrt(jnp.mean(x * x, axis=-1, keepdims=True) + RMS_EPS) * g


def _sigmoid(x):
    return 1.0 / (1.0 + jnp.exp(-x))


def _softplus(x):
    return jnp.maximum(x, 0.0) + jnp.log(1.0 + jnp.exp(-jnp.abs(x)))


def _gelu_tanh(x):
    return 0.5 * x * (1.0 + jnp.tanh(0.7978845608028654 * (x + 0.044715 * (x * x * x))))


def _bdot(a, b):
    return jnp.dot(a.astype(BF16), b.astype(BF16), preferred_element_type=F32)


def _shift_rows(x, tail, k):
    rolled = pltpu.roll(x, shift=k, axis=0)
    row = lax.broadcasted_iota(jnp.int32, (SUBLANES, x.shape[1]), 0)
    head = jnp.where(row < k, pltpu.roll(tail, shift=k, axis=0), rolled[:SUBLANES])
    return jnp.concatenate([head, rolled[SUBLANES:]], axis=0)


def _const_spec(shape, single=False):
    nd = len(shape)
    if single:
        return pl.BlockSpec(shape, lambda *_: (0,) * nd, pipeline_mode=pl.Buffered(1))
    return pl.BlockSpec(shape, lambda *_: (0,) * nd)


def _memkv_kernel(mem_ref, gn_ref, wkv_ref, out_ref):
    mn = _rms(mem_ref[...], gn_ref[...])
    out_ref[...] = _bdot(mn, wkv_ref[...]).astype(BF16)


def _mem_kv(mem2d, mem_norm, w_kv_bf):
    rows = mem2d.shape[0]
    depth = w_kv_bf.shape[0]
    tm = min(512, rows)
    return pl.pallas_call(
        _memkv_kernel,
        grid=(depth, rows // tm),
        in_specs=[
            pl.BlockSpec((tm, D_MODEL), lambda i, r: (r, 0)),
            _const_spec((1, D_MODEL)),
            pl.BlockSpec((None, D_MODEL, 2 * D_MODEL), lambda i, r: (i, 0, 0)),
        ],
        out_specs=pl.BlockSpec((None, tm, 2 * D_MODEL), lambda i, r: (i, r, 0)),
        out_shape=jax.ShapeDtypeStruct((depth, rows, 2 * D_MODEL), BF16),
        compiler_params=_cparams("arbitrary", "arbitrary"),
        name="mem_kv",
    )(mem2d, mem_norm, w_kv_bf)


def _run_interleaved(order, gens):
    done = {}
    for name in order:
        try:
            next(gens[name])
        except StopIteration as stop:
            done[name] = stop.value
    return done


def _rglru_block(x, g_ref, convw_ref, convb_ref, win_ref, bin_ref, gw_ref, gb_ref, lam_ref,
                 wout_ref, bout_ref, ubuf, a_s, b_s, h_s, hcar):
    ts = x.shape[0]
    hn = _rms(x, g_ref[0:1, :])
    proj = _bdot(hn, win_ref[...]) + bin_ref[...]
    yield
    yb = _gelu_tanh(proj[:, :D_MODEL])
    u = proj[:, D_MODEL:]
    tail = ubuf[...]
    conv = convb_ref[...] + u * convw_ref[CONV_WIDTH - 1:CONV_WIDTH, :]
    for tap in range(CONV_WIDTH - 1):
        conv = conv + _shift_rows(u, tail, CONV_WIDTH - 1 - tap) * convw_ref[tap:tap + 1, :]
    ubuf[...] = u[ts - SUBLANES:, :]
    yield
    conv_bf = conv.astype(BF16)
    r_parts, i_parts = [], []
    for h in range(LRU_HEADS):
        cb = conv_bf[:, h * LRU_BLOCK:(h + 1) * LRU_BLOCK]
        r_parts.append(jnp.dot(cb, gw_ref[0, h], preferred_element_type=F32))
        i_parts.append(jnp.dot(cb, gw_ref[1, h], preferred_element_type=F32))
    r_gate = _sigmoid(jnp.concatenate(r_parts, axis=1) + gb_ref[0:1, :])
    i_gate = _sigmoid(jnp.concatenate(i_parts, axis=1) + gb_ref[1:2, :])
    a = jnp.exp((-LRU_C) * r_gate * _softplus(-lam_ref[...]))
    a_s[...] = a
    b_s[...] = jnp.sqrt(1.0 - a * a) * i_gate * conv
    yield

    def body(t, h):
        h = a_s[pl.ds(t, 1), :] * h + b_s[pl.ds(t, 1), :]
        h_s[pl.ds(t, 1), :] = h
        return h

    hcar[...] = lax.fori_loop(0, ts, body, hcar[...], unroll=True)
    yield
    out =_bdot(h_s[...] * yb, wout_ref[...]) + bout_ref[...]
    return x + _rms(out, g_ref[1:2, :])


def _xattn_block(x, k_ref, v_ref, g_ref, wq_ref, wo_ref):
    hn = _rms(x, g_ref[0:1, :])
    q = _bdot(hn, wq_ref[...]).astype(BF16)
    yield
    outs = []
    for h in range(MEM_HEADS):
        sl = slice(h * MEM_HEAD_DIM, (h + 1) * MEM_HEAD_DIM)
        sc = lax.dot_general(q[:, sl], k_ref[:, sl], (((1,), (1,)), ((), ())),
                             preferred_element_type=F32)
        p = jnp.exp(sc - jnp.max(sc, axis=-1, keepdims=True))
        l = jnp.sum(p, axis=-1, keepdims=True)
        outs.append(jnp.dot(p.astype(BF16), v_ref[:, sl], preferred_element_type=F32) / l)
    yield
    c = _bdot(jnp.concatenate(outs, axis=1), wo_ref[...])
    return x + _rms(c, g_ref[1:2, :])


def _mlp_block(x, g_ref, wup_ref, wdn_ref):
    hn = _rms(x, g_ref[0:1, :]).astype(BF16)
    acc = None
    for c in range(MLP_PARTS):
        sl = slice(c * (D_FF // MLP_PARTS), (c + 1) * (D_FF // MLP_PARTS))
        h = jnp.maximum(jnp.dot(hn, wup_ref[:, sl], preferred_element_type=F32), 0.0)
        part = jnp.dot((h * h).astype(BF16), wdn_ref[sl, :], preferred_element_type=F32)
        acc = part if acc is None else acc + part
        if c + 1 < MLP_PARTS:
            yield
    return x + _rms(acc, g_ref[1:2, :])


def _xattn_mlp_block(x, k_ref, v_ref, gains_ref, wq_ref, wo_ref, wup_ref, wdn_ref):
    x1 = yield from _xattn_block(x, k_ref, v_ref, gains_ref.at[2:4], wq_ref, wo_ref)
    yield
    return (yield from _mlp_block(x1, gains_ref.at[4:6], wup_ref, wdn_ref))


_LAYER0_ORDER = ("b", "a", "b", "b") + ("a", "b") * MLP_PARTS


def _layer0_kernel(x_ref, k_ref, v_ref, gains_ref, convw_ref, convb_ref, win_ref, bin_ref, gw_ref, gb_ref,
                   lam_ref, wout_ref, bout_ref, wq_ref, wo_ref, wup_ref, wdn_ref, o_ref,
                   ubuf, a_s, b_s, h_s, hcar, mid, *, tiles_per_seq):
    i = pl.program_id(0)

    @pl.when(i == 0)
    def _():
        mid[...] = jnp.zeros(mid.shape, F32)

    @pl.when(i % tiles_per_seq == 0)
    def _():
        ubuf[...] = jnp.zeros(ubuf.shape, F32)
        hcar[...] = jnp.zeros(hcar.shape, F32)

    done = _run_interleaved(_LAYER0_ORDER, {
        "a": _rglru_block(x_ref[0], gains_ref.at[0:2], convw_ref, convb_ref, win_ref, bin_ref, gw_ref,
                          gb_ref, lam_ref, wout_ref, bout_ref, ubuf, a_s, b_s, h_s, hcar),
        "b": _xattn_mlp_block(mid[...], k_ref, v_ref, gains_ref, wq_ref, wo_ref, wup_ref, wdn_ref),
    })
    o_ref[0] = done["b"]
    mid[...] = done["a"]


def _layer0(x, kv, layer, gains, conv_w, conv_b, w_in_bf, b_in, gate_w_bf, gate_b, lam, w_out_bf, b_out,
            w_q_bf, w_o_bf, w_up_bf, w_down_bf):
    B, S, D = x.shape
    M = kv.shape[2]
    ts = min(SEQ_TILE, S)
    n = S // ts
    last = B * n - 1

    def cur(i):
        j = jnp.minimum(i, last)
        return j // n, j % n

    def prev(i):
        j = jnp.maximum(i - 1, 0)
        return j // n, j % n

    weights = [conv_w, conv_b, w_in_bf, b_in, gate_w_bf, gate_b, lam, w_out_bf, b_out,
               w_q_bf, w_o_bf, w_up_bf, w_down_bf]
    return pl.pallas_call(
        functools.partial(_layer0_kernel, tiles_per_seq=n),
        grid=(B * n + 1,),
        in_specs=[
            pl.BlockSpec((1, ts, D), lambda i: (*cur(i), 0)),
            pl.BlockSpec((None, None, M, D), lambda i: (layer, prev(i)[0], 0, 0)),
            pl.BlockSpec((None, None, M, D), lambda i: (layer, prev(i)[0], 0, 1)),
            _const_spec(gains.shape),
        ] + [_const_spec(w.shape, single=True) for w in weights],
        out_specs=pl.BlockSpec((1, ts, D), lambda i: (*prev(i), 0)),
        out_shape=jax.ShapeDtypeStruct((B, S, D), F32),
        scratch_shapes=[
            pltpu.VMEM((SUBLANES, D), F32),
            pltpu.VMEM((ts, D), F32),
            pltpu.VMEM((ts, D), F32),
            pltpu.VMEM((ts, D), F32),
            pltpu.VMEM((1, D), F32),
            pltpu.VMEM((ts, D), F32),
        ],
        compiler_params=_cparams("arbitrary"),
        name="layer0",
    )(x, kv, kv, gains, *weights)


def _head_sum(x, e_ref, exact=True):
    hi = x.astype(BF16)
    lo = (x - hi.astype(F32)).astype(BF16) if exact else None
    e = e_ref[...]
    parts = []
    for q in range(N_QUADS):
        sl = slice(q * QUAD, (q + 1) * QUAD)
        part = jnp.dot(hi[:, sl], e, preferred_element_type=F32)
        if exact:
            part = part + jnp.dot(lo[:, sl], e, preferred_element_type=F32)
        parts.append(part)
    return jnp.concatenate(parts, axis=1)


def _rwkv_pre_kernel(x_ref, g_ref, mu_ref, wrkv_ref, w0_ref, w1_ref, w2_ref, a0_ref, a1_ref, a2_ref,
                     g1_ref, g2_ref, kk_ref, ka_ref, rk_ref, e_ref,
                     r_out, c_out, k_out, v_out, a_out, b_out, g_out, bonus_out, hbuf, *, ts):
    s = pl.program_id(1)

    @pl.when(s == 0)
    def _():
        hbuf[...] = jnp.zeros((SUBLANES, D_MODEL), F32)

    hn = _rms(x_ref[0], g_ref[...])
    xx = (_shift_rows(hn, hbuf[...], 1) - hn).astype(BF16)
    hbuf[...] = hn[ts - SUBLANES:, :]
    hn_bf = hn.astype(BF16)
    mu_bf = mu_ref[...].astype(BF16)

    def mix(i):
        return hn_bf + xx * mu_bf[i:i + 1, :]

    r = _bdot(mix(0), wrkv_ref[0])
    k = _bdot(mix(2), wrkv_ref[1])
    v = _bdot(mix(3), wrkv_ref[2])
    z = w0_ref[...] + _bdot(jnp.tanh(_bdot(mix(1), w1_ref[...])), w2_ref[...])
    lw = DECAY_LOG_SCALE * _sigmoid(z)
    alr = _sigmoid(a0_ref[...] + _bdot(_bdot(mix(4), a1_ref[...]), a2_ref[...]))
    g = _bdot(_sigmoid(_bdot(mix(5), g1_ref[...])), g2_ref[...])

    kk = k * kk_ref[...]
    kk = kk * lax.rsqrt(jnp.maximum(_head_sum(kk * kk, e_ref, exact=False), 1e-24))
    ka = ka_ref[...]
    k = k * ((1.0 - ka) + alr * ka)
    L = min(SCAN_CHUNK, ts)
    ti = lax.broadcasted_iota(jnp.int32, (ts, ts), 0)
    si = lax.broadcasted_iota(jnp.int32, (ts, ts), 1)
    tri = ((ti // L == si // L) & (ti >= si)).astype(BF16)
    lw_hi = lw.astype(BF16)
    lw_lo = (lw - lw_hi.astype(F32)).astype(BF16)
    r_out[0] = r.astype(BF16)
    c_out[0] = (jnp.dot(tri, lw_hi, preferred_element_type=F32)
                + jnp.dot(tri, lw_lo, preferred_element_type=F32))
    k_out[0] = k.astype(BF16)
    v_out[0] = v.astype(BF16)
    a_out[0] = (-kk).astype(BF16)
    b_out[0] = (kk * alr).astype(BF16)
    g_out[0] = g.astype(BF16)
    bonus_out[0] = (_head_sum(r * k * rk_ref[...], e_ref, exact=False) * v).astype(BF16)


def _rwkv_pre(x, g0, mu, w_rkv_bf, w0, w1_bf, w2_bf, a0, a1_bf, a2_bf, g1_bf, g2_bf, k_k, k_a, r_k, e):
    B, S, D = x.shape
    ts = min(256, S)
    tile = pl.BlockSpec((1, ts, D), lambda b, s: (b, s, 0))
    consts = [g0, mu, w_rkv_bf, w0, w1_bf, w2_bf, a0, a1_bf, a2_bf, g1_bf, g2_bf, k_k, k_a, r_k, e]
    return pl.pallas_call(
        functools.partial(_rwkv_pre_kernel, ts=ts),
        grid=(B, S // ts),
        in_specs=[tile] + [_const_spec(c.shape) for c in consts],
        out_specs=[tile] * 8,
        out_shape=[jax.ShapeDtypeStruct((B, S, D), F32 if i == 1 else BF16) for i in range(8)],
        scratch_shapes=[pltpu.VMEM((SUBLANES, D), F32)],
        compiler_params=_cparams("arbitrary", "arbitrary"),
        name="rwkv_pre",
    )(x, *consts)


_BNN = (((2,), (1,)), ((0,), (0,)))
_BNT = (((2,), (2,)), ((0,), (0,)))
_BTN = (((1,), (1,)), ((0,), (0,)))


def _qdot(a, b, dims=_BNN):
    return lax.dot_general(a.astype(SCAN_DTYPE), b.astype(SCAN_DTYPE), dims, preferred_element_type=F32)


def _rwkv_scan_kernel(r_ref, c_ref, k_ref, v_ref, a_ref, b_ref, y_ref, state, *, L):
    @pl.when(pl.program_id(1) == 0)
    def _():
        state[...] = jnp.zeros(state.shape, F32)

    G = c_ref.shape[0]
    c = c_ref[...]
    first_row = lax.broadcasted_iota(jnp.int32, c.shape, 1) == 0
    c_prev = jnp.where(first_row, 0.0, pltpu.roll(c, shift=1, axis=1))
    c_end = c[:, L - 1:L, :]
    gam = jnp.exp(c)
    gam_inv = jnp.exp(-c)
    gam_end = jnp.exp(c_end - c)
    b_in = b_ref[...].astype(F32)
    k_in = k_ref[...].astype(F32)
    r_t = r_ref[...].astype(F32) * gam
    a_t = a_ref[...].astype(F32) * jnp.exp(c_prev)
    b_t = b_in * gam_inv
    k_t = k_in * gam_inv
    b_h = b_in * gam_end
    k_h = k_in * gam_end
    gam_last = jnp.exp(c_end)

    hq = QUAD // RWKV_HEAD
    t_idx = lax.broadcasted_iota(jnp.int32, (L, hq * L), 0)
    s_idx = lax.broadcasted_iota(jnp.int32, (L, hq * L), 1) % L
    m_strict = t_idx > s_idx
    m_incl = t_idx >= s_idx
    eye = (t_idx == s_idx).astype(F32)
    m_off = []
    m = 1
    while m < L:
        m_off.append(((t_idx // (2 * m)) == (s_idx // (2 * m))) & ((t_idx % (2 * m)) >= m)
                     & ((s_idx % (2 * m)) < m))
        m *= 2
    qrow = lax.broadcasted_iota(jnp.int32, (QUAD, QUAD), 0)
    qcol = lax.broadcasted_iota(jnp.int32, (QUAD, QUAD), 1)
    m_state = (qrow // RWKV_HEAD) == (qcol // RWKV_HEAD)
    lane_head = lax.broadcasted_iota(jnp.int32, (L, QUAD), 1) // RWKV_HEAD
    lane_blk = lax.broadcasted_iota(jnp.int32, (L, hq * L), 1) // L

    def quads(z):
        return jnp.stack([z[g][:, q * QUAD:(q + 1) * QUAD] for g in range(G) for q in range(N_QUADS)], axis=0)

    def bdiag(z, blk):
        z = z.astype(SCAN_DTYPE)
        zero = jnp.zeros_like(z)
        return jnp.concatenate([jnp.where(blk == h, z, zero) for h in range(hq)], axis=1)

    a_q, r_q, v_q, b_q, k_q = quads(a_t), quads(r_t), quads(v_ref[...]), quads(b_t), quads(k_t)
    ar = jnp.concatenate([a_q, r_q], axis=1)
    g_b = _qdot(ar, bdiag(b_q, lane_head), _BNT)
    g_k = _qdot(ar, bdiag(k_q, lane_head), _BNT)
    n_ab = g_b[:, :L]
    a_rb = jnp.where(m_incl, g_b[:, L:], 0.0)
    a_ak = jnp.where(m_strict, g_k[:, :L], 0.0)
    a_rk = jnp.where(m_incl, g_k[:, L:], 0.0)
    tinv = eye + jnp.where(m_off[0], n_ab, 0.0)
    levels = [jnp.where(m_lvl, n_ab, 0.0) for m_lvl in m_off[1:]]
    for start in range(0, len(levels), LEVEL_GROUP):
        group = levels[start:start + LEVEL_GROUP]
        z = _qdot(jnp.concatenate(group, axis=1), bdiag(tinv, lane_blk))
        pend = [z[:, i * L:(i + 1) * L] for i in range(len(group))]
        while pend:
            p_lvl = pend.pop(0)
            z = _qdot(jnp.concatenate([tinv] + pend, axis=1), bdiag(p_lvl, lane_blk))
            tinv = tinv + z[:, :L]
            pend = [q + z[:, (i + 1) * L:(i + 2) * L] for i, q in enumerate(pend)]
    s0 = state[...]
    xs = _qdot(ar, s0, _BNT)
    vs = _qdot(jnp.concatenate([a_ak, a_rk], axis=1), bdiag(v_q, lane_head))
    u = _qdot(tinv, bdiag(xs[:, :L] + vs[:, :L], lane_head))
    y = xs[:, L:] + vs[:, L:] + _qdot(a_rb, bdiag(u, lane_head))
    upd = _qdot(jnp.concatenate([u, v_q], axis=1), jnp.concatenate([quads(b_h), quads(k_h)], axis=1), _BTN)
    state[...] = s0 * quads(gam_last) + jnp.where(m_state, upd, 0.0)
    for g in range(G):
        y_ref[g] = jnp.concatenate([y[g * N_QUADS + q] for q in range(N_QUADS)], axis=1)


def _rwkv_scan(r, c, k, v, a, b):
    B, S, D = r.shape
    L = min(SCAN_CHUNK, S)
    G = SCAN_ROWS if B % SCAN_ROWS == 0 else 1
    tile = pl.BlockSpec((G, L, D), lambda bb, c: (bb, c, 0))
    return pl.pallas_call(
        functools.partial(_rwkv_scan_kernel, L=L),
        grid=(B // G, S // L),
        in_specs=[tile] * 6,
        out_specs=tile,
        out_shape=jax.ShapeDtypeStruct((B, S, D), F32),
        scratch_shapes=[pltpu.VMEM((G * N_QUADS, QUAD, QUAD), F32)],
        compiler_params=_cparams("arbitrary", "arbitrary"),
        name="rwkv_scan",
    )(r, c, k, v, a, b)


def _rwkv_post_block(x, y, bonus, gate, g_ref, gng_ref, gnb_ref, e_ref, wmix_ref):
    mean = _head_sum(y, e_ref)
    d = y - mean
    yield
    var = _head_sum(d * d, e_ref, exact=False)
    yn = d * lax.rsqrt(var + RWKV_GN_EPS) * gng_ref[...] + gnb_ref[...]
    yield
    t = _bdot((yn + bonus) * gate, wmix_ref[...])
    return x + _rms(t, g_ref[...])


_TAIL_ORDER = ("b", "b", "b") + ("a", "b") * 3 + ("b",) * (MLP_PARTS - 3)


def _rwkv_tail_kernel(x_ref, y_ref, bonus_ref, gate_ref, k_ref, v_ref, gains_ref, gng_ref, gnb_ref, e_ref,
                      wmix_ref, wq_ref, wo_ref, wup_ref, wdn_ref, o_ref, mid):
    @pl.when(pl.program_id(0) == 0)
    def _():
        mid[...] = jnp.zeros(mid.shape, F32)

    done = _run_interleaved(_TAIL_ORDER, {
        "a": _rwkv_post_block(x_ref[0], y_ref[0], bonus_ref[0], gate_ref[0], gains_ref.at[1:2], gng_ref,
                              gnb_ref, e_ref, wmix_ref),
        "b": _xattn_mlp_block(mid[...], k_ref, v_ref, gains_ref, wq_ref, wo_ref, wup_ref, wdn_ref),
    })
    o_ref[0] = done["b"]
    mid[...] = done["a"]


def _rwkv_tail(x, y, bonus, gate, kv, layer, gains, gn_g, gn_b, e_mean, w_mix_bf, w_q_bf, w_o_bf, w_up_bf,
               w_down_bf):
    B, S, D = x.shape
    M = kv.shape[2]
    ts = min(TAIL_TILE, S)
    n = S // ts
    last = B * n - 1

    def cur(i):
        j = jnp.minimum(i, last)
        return j // n, j % n

    def prev(i):
        j = jnp.maximum(i - 1, 0)
        return j // n, j % n

    tile_in = pl.BlockSpec((1, ts, D), lambda i: (*cur(i), 0))
    weights = [w_mix_bf, w_q_bf, w_o_bf, w_up_bf, w_down_bf]
    return pl.pallas_call(
        _rwkv_tail_kernel,
        grid=(B * n + 1,),
        in_specs=[tile_in] * 4 + [
            pl.BlockSpec((None, None, M, D), lambda i: (layer, prev(i)[0], 0, 0)),
            pl.BlockSpec((None, None, M, D), lambda i: (layer, prev(i)[0], 0, 1)),
            _const_spec(gains.shape), _const_spec(gn_g.shape), _const_spec(gn_b.shape),
            _const_spec(e_mean.shape),
        ] + [_const_spec(w.shape, single=True) for w in weights],
        out_specs=pl.BlockSpec((1, ts, D), lambda i: (*prev(i), 0)),
        out_shape=jax.ShapeDtypeStruct((B, S, D), F32),
        scratch_shapes=[pltpu.VMEM((ts, D), F32)],
        compiler_params=_cparams("arbitrary"),
        name="rwkv_tail",
    )(x, y, bonus, gate, kv, kv, gains, gn_g, gn_b, e_mean, *weights)


def _head_selector(value):
    head = jnp.arange(QUAD) // RWKV_HEAD
    return jnp.where(head[:, None] == head[None, :], value, 0.0).astype(BF16)


def kernel(x, mem, ln_gains, mem_norm, a_conv_w, a_conv_b, a_w_in, a_b_in, a_gate_w, a_gate_b, a_lambda, a_w_out, a_b_out, b_mu, b_w_rkv, b_w0, b_w1, b_w2, b_a0, b_a1, b_a2, b_g1, b_g2, b_k_k, b_k_a, b_r_k, b_gn_g, b_gn_b, b_w_o, c_w_q, c_w_kv, c_w_o, m_w_up, m_w_down):
    B, S, D = x.shape
    M = mem.shape[1]
    depth = ln_gains.shape[0]
    assert D == D_MODEL and depth == 2
    bf = lambda w: w.astype(BF16)
    row = lambda p: p.reshape(1, -1)

    kv = _mem_kv(mem.reshape(B * M, D), row(mem_norm), bf(c_w_kv)).reshape(depth, B, M, 2 * D)
    w_q_scaled = bf(c_w_q * (MEM_HEAD_DIM ** -0.5))
    e = _head_selector(1.0)
    e_mean = _head_selector(1.0 / RWKV_HEAD)

    for i in range(depth):
        g = ln_gains[i]
        j = i // 2
        if i % 2 == 0:
            x = _layer0(x, kv, i, g, a_conv_w[j], row(a_conv_b[j]), bf(a_w_in[j]), row(a_b_in[j]),
                        bf(a_gate_w[j]), a_gate_b[j].reshape(2, D), row(a_lambda[j]), bf(a_w_out[j]),
                        row(a_b_out[j]), w_q_scaled[i], bf(c_w_o[i]), bf(m_w_up[i]), bf(m_w_down[i]))
        else:
            r, c, k, v, a, b, gate, bonus = _rwkv_pre(
                x, g[0:1], b_mu[j], bf(b_w_rkv[j]), row(b_w0[j]), bf(b_w1[j]), bf(b_w2[j]), row(b_a0[j]),
                bf(b_a1[j]), bf(b_a2[j]), bf(b_g1[j]), bf(b_g2[j]), row(b_k_k[j]), row(b_k_a[j]),
                row(b_r_k[j]), e)
            y = _rwkv_scan(r, c, k, v, a, b)
            x = _rwkv_tail(x, y, bonus, gate, kv, i, g, row(b_gn_g[j]), row(b_gn_b[j]), e_mean, bf(b_w_o[j]),
                           w_q_scaled[i], bf(c_w_o[i]), bf(m_w_up[i]), bf(m_w_down[i]))
    return x
```

```python
import functools

import jax
import jax.numpy as jnp
from jax import lax
from jax.experimental import pallas as pl
from jax.experimental.pallas import tpu as pltpu

F32 = jnp.float32
BF16 = jnp.bfloat16

D_MODEL = 1024
RMS_EPS = 1e-6
LRU_HEADS = 4
LRU_BLOCK = D_MODEL // LRU_HEADS
CONV_WIDTH = 4
LRU_C = 8.0
RWKV_HEAD = 64
RWKV_HEADS = D_MODEL // RWKV_HEAD
RWKV_GN_EPS = 64e-5
DECAY_LOG_SCALE = -0.6065306597126334
MEM_HEADS = 4
MEM_HEAD_DIM = D_MODEL // MEM_HEADS
D_FF = 4 * D_MODEL

LANES = 128
SUBLANES = 8
QUAD = 256
N_QUADS = D_MODEL // QUAD
SEQ_TILE = 512
PRE_TILE = 512
TAIL_TILE = 512
MLP_PARTS = 4
SCAN_CHUNK = 64
LEVEL_GROUP = 3
SCAN_ROWS = 4
VMEM_LIMIT = 56 * 1024 * 1024

SCAN_DTYPE = BF16


def _cparams(*sem):
    return pltpu.CompilerParams(dimension_semantics=sem, vmem_limit_bytes=VMEM_LIMIT)


def _rms(x, g):
    return x * lax.rsqrt(jnp.mean(x * x, axis=-1, keepdims=True) + RMS_EPS) * g


def _sigmoid(x):
    return 1.0 / (1.0 + jnp.exp(-x))


def _softplus(x):
    return jnp.maximum(x, 0.0) + jnp.log(1.0 + jnp.exp(-jnp.abs(x)))


def _gelu_tanh(x):
    return 0.5 * x * (1.0 + jnp.tanh(0.7978845608028654 * (x + 0.044715 * (x * x * x))))


def _bdot(a, b):
    return jnp.dot(a.astype(BF16), b.astype(BF16), preferred_element_type=F32)


def _shift_rows(x, tail, k):
    rolled = pltpu.roll(x, shift=k, axis=0)
    row = lax.broadcasted_iota(jnp.int32, (SUBLANES, x.shape[1]), 0)
    head = jnp.where(row < k, pltpu.roll(tail, shift=k, axis=0), rolled[:SUBLANES])
    return jnp.concatenate([head, rolled[SUBLANES:]], axis=0)


def _const_spec(shape, single=False):
    nd = len(shape)
    if single:
        return pl.BlockSpec(shape, lambda *_: (0,) * nd, pipeline_mode=pl.Buffered(1))
    return pl.BlockSpec(shape, lambda *_: (0,) * nd)


def _memkv_kernel(mem_ref, gn_ref, wkv_ref, out_ref):
    mn = _rms(mem_ref[...], gn_ref[...])
    out_ref[...] = _bdot(mn, wkv_ref[...]).astype(BF16)


def _mem_kv(mem2d, mem_norm, w_kv_bf):
    rows = mem2d.shape[0]
    depth = w_kv_bf.shape[0]
    tm = min(512, rows)
    return pl.pallas_call(
        _memkv_kernel,
        grid=(depth, rows // tm),
        in_specs=[
            pl.BlockSpec((tm, D_MODEL), lambda i, r: (r, 0)),
            _const_spec((1, D_MODEL)),
            pl.BlockSpec((None, D_MODEL, 2 * D_MODEL), lambda i, r: (i, 0, 0)),
        ],
        out_specs=pl.BlockSpec((None, tm, 2 * D_MODEL), lambda i, r: (i, r, 0)),
        out_shape=jax.ShapeDtypeStruct((depth, rows, 2 * D_MODEL), BF16),
        compiler_params=_cparams("arbitrary", "arbitrary"),
        name="mem_kv",
    )(mem2d, mem_norm, w_kv_bf)


def _run_interleaved(order, gens):
    done = {}
    for name in order:
        try:
            next(gens[name])
        except StopIteration as stop:
            done[name] = stop.value
    return done


def _rglru_block(x, g_ref, convw_ref, convb_ref, win_ref, bin_ref, gw_ref, gb_ref, lam_ref,
                 wout_ref, bout_ref, ubuf, a_s, b_s, h_s, hcar):
    ts = x.shape[0]
    hn = _rms(x, g_ref[0:1, :])
    proj = _bdot(hn, win_ref[...]) + bin_ref[...]
    yield
    yb = _gelu_tanh(proj[:, :D_MODEL])
    u = proj[:, D_MODEL:]
    tail = ubuf[...]
    conv = convb_ref[...] + u * convw_ref[CONV_WIDTH - 1:CONV_WIDTH, :]
    for tap in range(CONV_WIDTH - 1):
        conv = conv + _shift_rows(u, tail, CONV_WIDTH - 1 - tap) * convw_ref[tap:tap + 1, :]
    ubuf[...] = u[ts - SUBLANES:, :]
    yield
    conv_bf = conv.astype(BF16)
    r_parts, i_parts = [], []
    for h in range(LRU_HEADS):
        cb = conv_bf[:, h * LRU_BLOCK:(h + 1) * LRU_BLOCK]
        r_parts.append(jnp.dot(cb, gw_ref[0, h], preferred_element_type=F32))
        i_parts.append(jnp.dot(cb, gw_ref[1, h], preferred_element_type=F32))
    r_gate = _sigmoid(jnp.concatenate(r_parts, axis=1) + gb_ref[0:1, :])
    i_gate = _sigmoid(jnp.concatenate(i_parts, axis=1) + gb_ref[1:2, :])
    a = jnp.exp((-LRU_C) * r_gate * _softplus(-lam_ref[...]))
    a_s[...] = a
    b_s[...] = jnp.sqrt(1.0 - a * a) * i_gate * conv
    yield

    def body(t, h):
        h = a_s[pl.ds(t, 1), :] * h + b_s[pl.ds(t, 1), :]
        h_s[pl.ds(t, 1), :] = h
        return h

    hcar[...] = lax.fori_loop(0, ts, body, hcar[...], unroll=True)
    yield
    out =_bdot(h_s[...] * yb, wout_ref[...]) + bout_ref[...]
    return x + _rms(out, g_ref[1:2, :])


def _xattn_block(x, k_ref, v_ref, g_ref, wq_ref, wo_ref):
    hn = _rms(x, g_ref[0:1, :])
    q = _bdot(hn, wq_ref[...]).astype(BF16)
    yield
    outs = []
    for h in range(MEM_HEADS):
        sl = slice(h * MEM_HEAD_DIM, (h + 1) * MEM_HEAD_DIM)
        sc = lax.dot_general(q[:, sl], k_ref[:, sl], (((1,), (1,)), ((), ())),
                             preferred_element_type=F32)
        p = jnp.exp(sc - jnp.max(sc, axis=-1, keepdims=True))
        l = jnp.sum(p, axis=-1, keepdims=True)
        outs.append(jnp.dot(p.astype(BF16), v_ref[:, sl], preferred_element_type=F32) / l)
    yield
    c = _bdot(jnp.concatenate(outs, axis=1), wo_ref[...])
    return x + _rms(c, g_ref[1:2, :])


def _mlp_block(x, g_ref, wup_ref, wdn_ref):
    hn = _rms(x, g_ref[0:1, :]).astype(BF16)
    acc = None
    for c in range(MLP_PARTS):
        sl = slice(c * (D_FF // MLP_PARTS), (c + 1) * (D_FF // MLP_PARTS))
        h = jnp.maximum(jnp.dot(hn, wup_ref[:, sl], preferred_element_type=F32), 0.0)
        part = jnp.dot((h * h).astype(BF16), wdn_ref[sl, :], preferred_element_type=F32)
        acc = part if acc is None else acc + part
        if c + 1 < MLP_PARTS:
            yield
    return x + _rms(acc, g_ref[1:2, :])


def _xattn_mlp_block(x, k_ref, v_ref, gains_ref, wq_ref, wo_ref, wup_ref, wdn_ref):
    x1 = yield from _xattn_block(x, k_ref, v_ref, gains_ref.at[2:4], wq_ref, wo_ref)
    yield
    return (yield from _mlp_block(x1, gains_ref.at[4:6], wup_ref, wdn_ref))


_LAYER0_ORDER = ("b", "a", "b", "b") + ("a", "b") * MLP_PARTS


def _layer0_kernel(x_ref, k_ref, v_ref, gains_ref, convw_ref, convb_ref, win_ref, bin_ref, gw_ref, gb_ref,
                   lam_ref, wout_ref, bout_ref, wq_ref, wo_ref, wup_ref, wdn_ref, o_ref,
                   ubuf, a_s, b_s, h_s, hcar, mid, *, tiles_per_seq):
    i = pl.program_id(0)

    @pl.when(i == 0)
    def _():
        mid[...] = jnp.zeros(mid.shape, F32)

    @pl.when(i % tiles_per_seq == 0)
    def _():
        ubuf[...] = jnp.zeros(ubuf.shape, F32)
        hcar[...] = jnp.zeros(hcar.shape, F32)

    done = _run_interleaved(_LAYER0_ORDER, {
        "a": _rglru_block(x_ref[0], gains_ref.at[0:2], convw_ref, convb_ref, win_ref, bin_ref, gw_ref,
                          gb_ref, lam_ref, wout_ref, bout_ref, ubuf, a_s, b_s, h_s, hcar),
        "b": _xattn_mlp_block(mid[...], k_ref, v_ref, gains_ref, wq_ref, wo_ref, wup_ref, wdn_ref),
    })
    o_ref[0] = done["b"]
    mid[...] = done["a"]


def _layer0(x, kv, layer, gains, conv_w, conv_b, w_in_bf, b_in, gate_w_bf, gate_b, lam, w_out_bf, b_out,
            w_q_bf, w_o_bf, w_up_bf, w_down_bf):
    B, S, D = x.shape
    M = kv.shape[2]
    ts = min(SEQ_TILE, S)
    n = S // ts
    last = B * n - 1

    def cur(i):
        j = jnp.minimum(i, last)
        return j // n, j % n

    def prev(i):
        j = jnp.maximum(i - 1, 0)
        return j // n, j % n

    weights = [conv_w, conv_b, w_in_bf, b_in, gate_w_bf, gate_b, lam, w_out_bf, b_out,
               w_q_bf, w_o_bf, w_up_bf, w_down_bf]
    return pl.pallas_call(
        functools.partial(_layer0_kernel, tiles_per_seq=n),
        grid=(B * n + 1,),
        in_specs=[
            pl.BlockSpec((1, ts, D), lambda i: (*cur(i), 0)),
            pl.BlockSpec((None, None, M, D), lambda i: (layer, prev(i)[0], 0, 0)),
            pl.BlockSpec((None, None, M, D), lambda i: (layer, prev(i)[0], 0, 1)),
            _const_spec(gains.shape),
        ] + [_const_spec(w.shape, single=True) for w in weights],
        out_specs=pl.BlockSpec((1, ts, D), lambda i: (*prev(i), 0)),
        out_shape=jax.ShapeDtypeStruct((B, S, D), F32),
        scratch_shapes=[
            pltpu.VMEM((SUBLANES, D), F32),
            pltpu.VMEM((ts, D), F32),
            pltpu.VMEM((ts, D), F32),
            pltpu.VMEM((ts, D), F32),
            pltpu.VMEM((1, D), F32),
            pltpu.VMEM((ts, D), F32),
        ],
        compiler_params=_cparams("arbitrary"),
        name="layer0",
    )(x, kv, kv, gains, *weights)


def _head_sum(x, e_ref, exact=True):
    hi = x.astype(BF16)
    lo = (x - hi.astype(F32)).astype(BF16) if exact else None
    e = e_ref[...]
    parts = []
    for q in range(N_QUADS):
        sl = slice(q * QUAD, (q + 1) * QUAD)
        part = jnp.dot(hi[:, sl], e, preferred_element_type=F32)
        if exact:
            part = part + jnp.dot(lo[:, sl], e, preferred_element_type=F32)
        parts.append(part)
    return jnp.concatenate(parts, axis=1)


def _rwkv_pre_kernel(x_ref, g_ref, mu_ref, wrkv_ref, w0_ref, w1_ref, w2_ref, a0_ref, a1_ref, a2_ref,
                     g1_ref, g2_ref, kk_ref, ka_ref, rk_ref, e_ref,
                     r_out, c_out, k_out, v_out, a_out, b_out, g_out, bonus_out, hbuf, *, ts):
    s = pl.program_id(1)

    @pl.when(s == 0)
    def _():
        hbuf[...] = jnp.zeros((SUBLANES, D_MODEL), F32)

    hn = _rms(x_ref[0], g_ref[...])
    xx = (_shift_rows(hn, hbuf[...], 1) - hn).astype(BF16)
    hbuf[...] = hn[ts - SUBLANES:, :]
    hn_bf = hn.astype(BF16)
    mu_bf = mu_ref[...].astype(BF16)

    def mix(i):
        return hn_bf + xx * mu_bf[i:i + 1, :]

    r = _bdot(mix(0), wrkv_ref[0])
    k = _bdot(mix(2), wrkv_ref[1])
    v = _bdot(mix(3), wrkv_ref[2])
    z = w0_ref[...] + _bdot(jnp.tanh(_bdot(mix(1), w1_ref[...])), w2_ref[...])
    lw = DECAY_LOG_SCALE * _sigmoid(z)
    alr = _sigmoid(a0_ref[...] + _bdot(_bdot(mix(4), a1_ref[...]), a2_ref[...]))
    g = _bdot(_sigmoid(_bdot(mix(5), g1_ref[...])), g2_ref[...])

    kk = k * kk_ref[...]
    kk = kk * lax.rsqrt(jnp.maximum(_head_sum(kk * kk, e_ref, exact=False), 1e-24))
    ka = ka_ref[...]
    k = k * ((1.0 - ka) + alr * ka)
    L = min(SCAN_CHUNK, ts)
    rows = min(QUAD, ts)
    ti = lax.broadcasted_iota(jnp.int32, (rows, rows), 0)
    si = lax.broadcasted_iota(jnp.int32, (rows, rows), 1)
    tri = ((ti // L == si // L) & (ti >= si)).astype(BF16)
    lw_hi = lw.astype(BF16)
    lw_lo = (lw - lw_hi.astype(F32)).astype(BF16)
    r_out[0] = r.astype(BF16)
    c_out[0] = jnp.concatenate(
        [jnp.dot(tri, lw_hi[i:i + rows], preferred_element_type=F32)
         + jnp.dot(tri, lw_lo[i:i + rows], preferred_element_type=F32) for i in range(0, ts, rows)], axis=0)
    k_out[0] = k.astype(BF16)
    v_out[0] = v.astype(BF16)
    a_out[0] = (-kk).astype(BF16)
    b_out[0] = (kk * alr).astype(BF16)
    g_out[0] = g.astype(BF16)
    bonus_out[0] = (_head_sum(r * k * rk_ref[...], e_ref, exact=False) * v).astype(BF16)


def _rwkv_pre(x, g0, mu, w_rkv_bf, w0, w1_bf, w2_bf, a0, a1_bf, a2_bf, g1_bf, g2_bf, k_k, k_a, r_k, e):
    B, S, D = x.shape
    ts = min(PRE_TILE, S)
    tile = pl.BlockSpec((1, ts, D), lambda b, s: (b, s, 0))
    consts = [g0, mu, w_rkv_bf, w0, w1_bf, w2_bf, a0, a1_bf, a2_bf, g1_bf, g2_bf, k_k, k_a, r_k, e]
    return pl.pallas_call(
        functools.partial(_rwkv_pre_kernel, ts=ts),
        grid=(B, S // ts),
        in_specs=[tile] + [_const_spec(c.shape, single=c.size >= D * RWKV_HEAD) for c in consts],
        out_specs=[tile] * 8,
        out_shape=[jax.ShapeDtypeStruct((B, S, D), F32 if i == 1 else BF16) for i in range(8)],
        scratch_shapes=[pltpu.VMEM((SUBLANES, D), F32)],
        compiler_params=_cparams("arbitrary", "arbitrary"),
        name="rwkv_pre",
    )(x, *consts)


_BNN = (((2,), (1,)), ((0,), (0,)))
_BNT = (((2,), (2,)), ((0,), (0,)))
_BTN = (((1,), (1,)), ((0,), (0,)))


def _qdot(a, b, dims=_BNN):
    return lax.dot_general(a.astype(SCAN_DTYPE), b.astype(SCAN_DTYPE), dims, preferred_element_type=F32)


def _rwkv_scan_kernel(r_ref, c_ref, k_ref, v_ref, a_ref, b_ref, y_ref, state, *, L):
    @pl.when(pl.program_id(1) == 0)
    def _():
        state[...] = jnp.zeros(state.shape, F32)

    G = c_ref.shape[0]
    c = c_ref[...]
    first_row = lax.broadcasted_iota(jnp.int32, c.shape, 1) == 0
    c_prev = jnp.where(first_row, 0.0, pltpu.roll(c, shift=1, axis=1))
    c_end = c[:, L - 1:L, :]
    gam = jnp.exp(c)
    gam_inv = jnp.exp(-c)
    gam_end = jnp.exp(c_end - c)
    b_in = b_ref[...].astype(F32)
    k_in = k_ref[...].astype(F32)
    r_t = r_ref[...].astype(F32) * gam
    a_t = a_ref[...].astype(F32) * jnp.exp(c_prev)
    b_t = b_in * gam_inv
    k_t = k_in * gam_inv
    b_h = b_in * gam_end
    k_h = k_in * gam_end
    gam_last = jnp.exp(c_end)

    hq = QUAD // RWKV_HEAD
    t_idx = lax.broadcasted_iota(jnp.int32, (L, hq * L), 0)
    s_idx = lax.broadcasted_iota(jnp.int32, (L, hq * L), 1) % L
    m_strict = t_idx > s_idx
    m_incl = t_idx >= s_idx
    eye = (t_idx == s_idx).astype(F32)
    m_off = []
    m = 1
    while m < L:
        m_off.append(((t_idx // (2 * m)) == (s_idx // (2 * m))) & ((t_idx % (2 * m)) >= m)
                     & ((s_idx % (2 * m)) < m))
        m *= 2
    qrow = lax.broadcasted_iota(jnp.int32, (QUAD, QUAD), 0)
    qcol = lax.broadcasted_iota(jnp.int32, (QUAD, QUAD), 1)
    m_state = (qrow // RWKV_HEAD) == (qcol // RWKV_HEAD)
    lane_head = lax.broadcasted_iota(jnp.int32, (L, QUAD), 1) // RWKV_HEAD
    lane_blk = lax.broadcasted_iota(jnp.int32, (L, hq * L), 1) // L

    def quads(z):
        return jnp.stack([z[g][:, q * QUAD:(q + 1) * QUAD] for g in range(G) for q in range(N_QUADS)], axis=0)

    def bdiag(z, blk):
        z = z.astype(SCAN_DTYPE)
        zero = jnp.zeros_like(z)
        return jnp.concatenate([jnp.where(blk == h, z, zero) for h in range(hq)], axis=1)

    a_q, r_q, v_q, b_q, k_q = quads(a_t), quads(r_t), quads(v_ref[...]), quads(b_t), quads(k_t)
    ar = jnp.concatenate([a_q, r_q], axis=1)
    g_b = _qdot(ar, bdiag(b_q, lane_head), _BNT)
    g_k = _qdot(ar, bdiag(k_q, lane_head), _BNT)
    n_ab = g_b[:, :L]
    a_rb = jnp.where(m_incl, g_b[:, L:], 0.0)
    a_ak = jnp.where(m_strict, g_k[:, :L], 0.0)
    a_rk = jnp.where(m_incl, g_k[:, L:], 0.0)
    tinv = eye + jnp.where(m_off[0], n_ab, 0.0)
    levels = [jnp.where(m_lvl, n_ab, 0.0) for m_lvl in m_off[1:]]
    for start in range(0, len(levels), LEVEL_GROUP):
        group = levels[start:start + LEVEL_GROUP]
        z = _qdot(jnp.concatenate(group, axis=1), bdiag(tinv, lane_blk))
        pend = [z[:, i * L:(i + 1) * L] for i in range(len(group))]
        while pend:
            p_lvl = pend.pop(0)
            z = _qdot(jnp.concatenate([tinv] + pend, axis=1), bdiag(p_lvl, lane_blk))
            tinv = tinv + z[:, :L]
            pend = [q + z[:, (i + 1) * L:(i + 2) * L] for i, q in enumerate(pend)]
    s0 = state[...]
    xs = _qdot(ar, s0, _BNT)
    vs = _qdot(jnp.concatenate([a_ak, a_rk], axis=1), bdiag(v_q, lane_head))
    u = _qdot(tinv, bdiag(xs[:, :L] + vs[:, :L], lane_head))
    y = xs[:, L:] + vs[:, L:] + _qdot(a_rb, bdiag(u, lane_head))
    upd = _qdot(jnp.concatenate([u, v_q], axis=1), jnp.concatenate([quads(b_h), quads(k_h)], axis=1), _BTN)
    state[...] = s0 * quads(gam_last) + jnp.where(m_state, upd, 0.0)
    for g in range(G):
        y_ref[g] = jnp.concatenate([y[g * N_QUADS + q] for q in range(N_QUADS)], axis=1)


def _rwkv_scan(r, c, k, v, a, b):
    B, S, D = r.shape
    L = min(SCAN_CHUNK, S)
    G = SCAN_ROWS if B % SCAN_ROWS == 0 else 1
    tile = pl.BlockSpec((G, L, D), lambda bb, c: (bb, c, 0))
    return pl.pallas_call(
        functools.partial(_rwkv_scan_kernel, L=L),
        grid=(B // G, S // L),
        in_specs=[tile] * 6,
        out_specs=tile,
        out_shape=jax.ShapeDtypeStruct((B, S, D), F32),
        scratch_shapes=[pltpu.VMEM((G * N_QUADS, QUAD, QUAD), F32)],
        compiler_params=_cparams("arbitrary", "arbitrary"),
        name="rwkv_scan",
    )(r, c, k, v, a, b)


def _rwkv_post_block(x, y, bonus, gate, g_ref, gng_ref, gnb_ref, e_ref, wmix_ref):
    mean = _head_sum(y, e_ref)
    d = y - mean
    yield
    var = _head_sum(d * d, e_ref, exact=False)
    yn = d * lax.rsqrt(var + RWKV_GN_EPS) * gng_ref[...] + gnb_ref[...]
    yield
    t = _bdot((yn + bonus) * gate, wmix_ref[...])
    return x + _rms(t, g_ref[...])


_TAIL_ORDER = ("b", "b", "b") + ("a", "b") * 3 + ("b",) * (MLP_PARTS - 3)


def _rwkv_tail_kernel(x_ref, y_ref, bonus_ref, gate_ref, k_ref, v_ref, gains_ref, gng_ref, gnb_ref, e_ref,
                      wmix_ref, wq_ref, wo_ref, wup_ref, wdn_ref, o_ref, mid):
    @pl.when(pl.program_id(0) == 0)
    def _():
        mid[...] = jnp.zeros(mid.shape, F32)

    done = _run_interleaved(_TAIL_ORDER, {
        "a": _rwkv_post_block(x_ref[0], y_ref[0], bonus_ref[0], gate_ref[0], gains_ref.at[1:2], gng_ref,
                              gnb_ref, e_ref, wmix_ref),
        "b": _xattn_mlp_block(mid[...], k_ref, v_ref, gains_ref, wq_ref, wo_ref, wup_ref, wdn_ref),
    })
    o_ref[0] = done["b"]
    mid[...] = done["a"]


def _rwkv_tail(x, y, bonus, gate, kv, layer, gains, gn_g, gn_b, e_mean, w_mix_bf, w_q_bf, w_o_bf, w_up_bf,
               w_down_bf):
    B, S, D = x.shape
    M = kv.shape[2]
    ts = min(TAIL_TILE, S)
    n = S // ts
    last = B * n - 1

    def cur(i):
        j = jnp.minimum(i, last)
        return j // n, j % n

    def prev(i):
        j = jnp.maximum(i - 1, 0)
        return j // n, j % n

    tile_in = pl.BlockSpec((1, ts, D), lambda i: (*cur(i), 0))
    weights = [w_mix_bf, w_q_bf, w_o_bf, w_up_bf, w_down_bf]
    return pl.pallas_call(
        _rwkv_tail_kernel,
        grid=(B * n + 1,),
        in_specs=[tile_in] * 4 + [
            pl.BlockSpec((None, None, M, D), lambda i: (layer, prev(i)[0], 0, 0)),
            pl.BlockSpec((None, None, M, D), lambda i: (layer, prev(i)[0], 0, 1)),
            _const_spec(gains.shape), _const_spec(gn_g.shape), _const_spec(gn_b.shape),
            _const_spec(e_mean.shape),
        ] + [_const_spec(w.shape, single=True) for w in weights],
        out_specs=pl.BlockSpec((1, ts, D), lambda i: (*prev(i), 0)),
        out_shape=jax.ShapeDtypeStruct((B, S, D), F32),
        scratch_shapes=[pltpu.VMEM((ts, D), F32)],
        compiler_params=_cparams("arbitrary"),
        name="rwkv_tail",
    )(x, y, bonus, gate, kv, kv, gains, gn_g, gn_b, e_mean, *weights)


def _head_selector(value):
    head = jnp.arange(QUAD) // RWKV_HEAD
    return jnp.where(head[:, None] == head[None, :], value, 0.0).astype(BF16)


def kernel(x, mem, ln_gains, mem_norm, a_conv_w, a_conv_b, a_w_in, a_b_in, a_gate_w, a_gate_b, a_lambda, a_w_out, a_b_out, b_mu, b_w_rkv, b_w0, b_w1, b_w2, b_a0, b_a1, b_a2, b_g1, b_g2, b_k_k, b_k_a, b_r_k, b_gn_g, b_gn_b, b_w_o, c_w_q, c_w_kv, c_w_o, m_w_up, m_w_down):
    B, S, D = x.shape
    M = mem.shape[1]
    depth = ln_gains.shape[0]
    assert D == D_MODEL and depth == 2
    bf = lambda w: w.astype(BF16)
    row = lambda p: p.reshape(1, -1)

    kv = _mem_kv(mem.reshape(B * M, D), row(mem_norm), bf(c_w_kv)).reshape(depth, B, M, 2 * D)
    w_q_scaled = bf(c_w_q * (MEM_HEAD_DIM ** -0.5))
    e = _head_selector(1.0)
    e_mean = _head_selector(1.0 / RWKV_HEAD)

    for i in range(depth):
        g = ln_gains[i]
        j = i // 2
        if i % 2 == 0:
            x = _layer0(x, kv, i, g, a_conv_w[j], row(a_conv_b[j]), bf(a_w_in[j]), row(a_b_in[j]),
                        bf(a_gate_w[j]), a_gate_b[j].reshape(2, D), row(a_lambda[j]), bf(a_w_out[j]),
                        row(a_b_out[j]), w_q_scaled[i], bf(c_w_o[i]), bf(m_w_up[i]), bf(m_w_down[i]))
        else:
            r, c, k, v, a, b, gate, bonus = _rwkv_pre(
                x, g[0:1], b_mu[j], bf(b_w_rkv[j]), row(b_w0[j]), bf(b_w1[j]), bf(b_w2[j]), row(b_a0[j]),
                bf(b_a1[j]), bf(b_a2[j]), bf(b_g1[j]), bf(b_g2[j]), row(b_k_k[j]), row(b_k_a[j]),
                row(b_r_k[j]), e)
            y = _rwkv_scan(r, c, k, v, a, b)
            x = _rwkv_tail(x, y, bonus, gate, kv, i, g, row(b_gn_g[j]), row(b_gn_b[j]), e_mean, bf(b_w_o[j]),
                           w_q_scaled[i], bf(c_w_o[i]), bf(m_w_up[i]), bf(m_w_down[i]))
    return x
```

```python
import functools

import jax
import jax.numpy as jnp
from jax import lax
from jax.experimental import pallas as pl
from jax.experimental.pallas import tpu as pltpu

F32 = jnp.float32
BF16 = jnp.bfloat16

D_MODEL = 1024
RMS_EPS = 1e-6
LRU_HEADS = 4
LRU_BLOCK = D_MODEL // LRU_HEADS
CONV_WIDTH = 4
LRU_C = 8.0
RWKV_HEAD = 64
RWKV_HEADS = D_MODEL // RWKV_HEAD
RWKV_GN_EPS = 64e-5
DECAY_LOG_SCALE = -0.6065306597126334
MEM_HEADS = 4
MEM_HEAD_DIM = D_MODEL // MEM_HEADS
D_FF = 4 * D_MODEL

LANES = 128
SUBLANES = 8
QUAD = 256
N_QUADS = D_MODEL // QUAD
SEQ_TILE = 512
PRE_TILE = 512
TAIL_TILE = 512
MLP_PARTS = 4
SCAN_CHUNK = 64
LEVEL_GROUP = 3
SCAN_ROWS = 8
VMEM_LIMIT = 56 * 1024 * 1024

SCAN_DTYPE = BF16


def _cparams(*sem):
    return pltpu.CompilerParams(dimension_semantics=sem, vmem_limit_bytes=VMEM_LIMIT)


def _rms(x, g):
    return x * lax.rsqrt(jnp.mean(x * x, axis=-1, keepdims=True) + RMS_EPS) * g


def _sigmoid(x):
    return 1.0 / (1.0 + jnp.exp(-x))


def _softplus(x):
    return jnp.maximum(x, 0.0) + jnp.log(1.0 + jnp.exp(-jnp.abs(x)))


def _gelu_tanh(x):
    return 0.5 * x * (1.0 + jnp.tanh(0.7978845608028654 * (x + 0.044715 * (x * x * x))))


def _bdot(a, b):
    return jnp.dot(a.astype(BF16), b.astype(BF16), preferred_element_type=F32)


def _shift_rows(x, tail, k):
    rolled = pltpu.roll(x, shift=k, axis=0)
    row = lax.broadcasted_iota(jnp.int32, (SUBLANES, x.shape[1]), 0)
    head = jnp.where(row < k, pltpu.roll(tail, shift=k, axis=0), rolled[:SUBLANES])
    return jnp.concatenate([head, rolled[SUBLANES:]], axis=0)


def _const_spec(shape, single=False):
    nd = len(shape)
    if single:
        return pl.BlockSpec(shape, lambda *_: (0,) * nd, pipeline_mode=pl.Buffered(1))
    return pl.BlockSpec(shape, lambda *_: (0,) * nd)


def _memkv_kernel(mem_ref, gn_ref, wkv_ref, out_ref):
    mn = _rms(mem_ref[...], gn_ref[...])
    out_ref[...] = _bdot(mn, wkv_ref[...]).astype(BF16)


def _mem_kv(mem2d, mem_norm, w_kv_bf):
    rows = mem2d.shape[0]
    depth = w_kv_bf.shape[0]
    tm = min(512, rows)
    return pl.pallas_call(
        _memkv_kernel,
        grid=(depth, rows // tm),
        in_specs=[
            pl.BlockSpec((tm, D_MODEL), lambda i, r: (r, 0)),
            _const_spec((1, D_MODEL)),
            pl.BlockSpec((None, D_MODEL, 2 * D_MODEL), lambda i, r: (i, 0, 0)),
        ],
        out_specs=pl.BlockSpec((None, tm, 2 * D_MODEL), lambda i, r: (i, r, 0)),
        out_shape=jax.ShapeDtypeStruct((depth, rows, 2 * D_MODEL), BF16),
        compiler_params=_cparams("arbitrary", "arbitrary"),
        name="mem_kv",
    )(mem2d, mem_norm, w_kv_bf)


def _run_interleaved(order, gens):
    done = {}
    for name in order:
        try:
            next(gens[name])
        except StopIteration as stop:
            done[name] = stop.value
    return done


def _rglru_block(x, g_ref, convw_ref, convb_ref, win_ref, bin_ref, gw_ref, gb_ref, lam_ref,
                 wout_ref, bout_ref, ubuf, a_s, b_s, h_s, hcar):
    ts = x.shape[0]
    hn = _rms(x, g_ref[0:1, :])
    proj = _bdot(hn, win_ref[...]) + bin_ref[...]
    yield
    yb = _gelu_tanh(proj[:, :D_MODEL])
    u = proj[:, D_MODEL:]
    tail = ubuf[...]
    conv = convb_ref[...] + u * convw_ref[CONV_WIDTH - 1:CONV_WIDTH, :]
    for tap in range(CONV_WIDTH - 1):
        conv = conv + _shift_rows(u, tail, CONV_WIDTH - 1 - tap) * convw_ref[tap:tap + 1, :]
    ubuf[...] = u[ts - SUBLANES:, :]
    yield
    conv_bf = conv.astype(BF16)
    r_parts, i_parts = [], []
    for h in range(LRU_HEADS):
        cb = conv_bf[:, h * LRU_BLOCK:(h + 1) * LRU_BLOCK]
        r_parts.append(jnp.dot(cb, gw_ref[0, h], preferred_element_type=F32))
        i_parts.append(jnp.dot(cb, gw_ref[1, h], preferred_element_type=F32))
    r_gate = _sigmoid(jnp.concatenate(r_parts, axis=1) + gb_ref[0:1, :])
    i_gate = _sigmoid(jnp.concatenate(i_parts, axis=1) + gb_ref[1:2, :])
    a = jnp.exp((-LRU_C) * r_gate * _softplus(-lam_ref[...]))
    a_s[...] = a
    b_s[...] = jnp.sqrt(1.0 - a * a) * i_gate * conv
    yield

    def body(t, h):
        h = a_s[pl.ds(t, 1), :] * h + b_s[pl.ds(t, 1), :]
        h_s[pl.ds(t, 1), :] = h
        return h

    hcar[...] = lax.fori_loop(0, ts, body, hcar[...], unroll=True)
    yield
    out =_bdot(h_s[...] * yb, wout_ref[...]) + bout_ref[...]
    return x + _rms(out, g_ref[1:2, :])


def _xattn_block(x, k_ref, v_ref, g_ref, wq_ref, wo_ref):
    hn = _rms(x, g_ref[0:1, :])
    q = _bdot(hn, wq_ref[...]).astype(BF16)
    yield
    outs = []
    for h in range(MEM_HEADS):
        sl = slice(h * MEM_HEAD_DIM, (h + 1) * MEM_HEAD_DIM)
        sc = lax.dot_general(q[:, sl], k_ref[:, sl], (((1,), (1,)), ((), ())),
                             preferred_element_type=F32)
        p = jnp.exp(sc - jnp.max(sc, axis=-1, keepdims=True))
        l = jnp.sum(p, axis=-1, keepdims=True)
        outs.append(jnp.dot(p.astype(BF16), v_ref[:, sl], preferred_element_type=F32) / l)
    yield
    c = _bdot(jnp.concatenate(outs, axis=1), wo_ref[...])
    return x + _rms(c, g_ref[1:2, :])


def _mlp_block(x, g_ref, wup_ref, wdn_ref):
    hn = _rms(x, g_ref[0:1, :]).astype(BF16)
    acc = None
    for c in range(MLP_PARTS):
        sl = slice(c * (D_FF // MLP_PARTS), (c + 1) * (D_FF // MLP_PARTS))
        h = jnp.maximum(jnp.dot(hn, wup_ref[:, sl], preferred_element_type=F32), 0.0)
        part = jnp.dot((h * h).astype(BF16), wdn_ref[sl, :], preferred_element_type=F32)
        acc = part if acc is None else acc + part
        if c + 1 < MLP_PARTS:
            yield
    return x + _rms(acc, g_ref[1:2, :])


def _xattn_mlp_block(x, k_ref, v_ref, gains_ref, wq_ref, wo_ref, wup_ref, wdn_ref):
    x1 = yield from _xattn_block(x, k_ref, v_ref, gains_ref.at[2:4], wq_ref, wo_ref)
    yield
    return (yield from _mlp_block(x1, gains_ref.at[4:6], wup_ref, wdn_ref))


_LAYER0_ORDER = ("b", "a", "b", "b") + ("a", "b") * MLP_PARTS


def _layer0_kernel(x_ref, k_ref, v_ref, gains_ref, convw_ref, convb_ref, win_ref, bin_ref, gw_ref, gb_ref,
                   lam_ref, wout_ref, bout_ref, wq_ref, wo_ref, wup_ref, wdn_ref, o_ref,
                   ubuf, a_s, b_s, h_s, hcar, mid, *, tiles_per_seq):
    i = pl.program_id(0)

    @pl.when(i == 0)
    def _():
        mid[...] = jnp.zeros(mid.shape, F32)

    @pl.when(i % tiles_per_seq == 0)
    def _():
        ubuf[...] = jnp.zeros(ubuf.shape, F32)
        hcar[...] = jnp.zeros(hcar.shape, F32)

    done = _run_interleaved(_LAYER0_ORDER, {
        "a": _rglru_block(x_ref[0], gains_ref.at[0:2], convw_ref, convb_ref, win_ref, bin_ref, gw_ref,
                          gb_ref, lam_ref, wout_ref, bout_ref, ubuf, a_s, b_s, h_s, hcar),
        "b": _xattn_mlp_block(mid[...], k_ref, v_ref, gains_ref, wq_ref, wo_ref, wup_ref, wdn_ref),
    })
    o_ref[0] = done["b"]
    mid[...] = done["a"]


def _layer0(x, kv, layer, gains, conv_w, conv_b, w_in_bf, b_in, gate_w_bf, gate_b, lam, w_out_bf, b_out,
            w_q_bf, w_o_bf, w_up_bf, w_down_bf):
    B, S, D = x.shape
    M = kv.shape[2]
    ts = min(SEQ_TILE, S)
    n = S // ts
    last = B * n - 1

    def cur(i):
        j = jnp.minimum(i, last)
        return j // n, j % n

    def prev(i):
        j = jnp.maximum(i - 1, 0)
        return j // n, j % n

    weights = [conv_w, conv_b, w_in_bf, b_in, gate_w_bf, gate_b, lam, w_out_bf, b_out,
               w_q_bf, w_o_bf, w_up_bf, w_down_bf]
    return pl.pallas_call(
        functools.partial(_layer0_kernel, tiles_per_seq=n),
        grid=(B * n + 1,),
        in_specs=[
            pl.BlockSpec((1, ts, D), lambda i: (*cur(i), 0)),
            pl.BlockSpec((None, None, M, D), lambda i: (layer, prev(i)[0], 0, 0)),
            pl.BlockSpec((None, None, M, D), lambda i: (layer, prev(i)[0], 0, 1)),
            _const_spec(gains.shape),
        ] + [_const_spec(w.shape, single=True) for w in weights],
        out_specs=pl.BlockSpec((1, ts, D), lambda i: (*prev(i), 0)),
        out_shape=jax.ShapeDtypeStruct((B, S, D), F32),
        scratch_shapes=[
            pltpu.VMEM((SUBLANES, D), F32),
            pltpu.VMEM((ts, D), F32),
            pltpu.VMEM((ts, D), F32),
            pltpu.VMEM((ts, D), F32),
            pltpu.VMEM((1, D), F32),
            pltpu.VMEM((ts, D), F32),
        ],
        compiler_params=_cparams("arbitrary"),
        name="layer0",
    )(x, kv, kv, gains, *weights)


def _head_sum(x, e_ref, exact=True):
    hi = x.astype(BF16)
    lo = (x - hi.astype(F32)).astype(BF16) if exact else None
    e = e_ref[...]
    parts = []
    for q in range(N_QUADS):
        sl = slice(q * QUAD, (q + 1) * QUAD)
        part = jnp.dot(hi[:, sl], e, preferred_element_type=F32)
        if exact:
            part = part + jnp.dot(lo[:, sl], e, preferred_element_type=F32)
        parts.append(part)
    return jnp.concatenate(parts, axis=1)


def _rwkv_pre_kernel(x_ref, g_ref, mu_ref, wrkv_ref, w0_ref, w1_ref, w2_ref, a0_ref, a1_ref, a2_ref,
                     g1_ref, g2_ref, kk_ref, ka_ref, rk_ref, e_ref,
                     r_out, c_out, k_out, v_out, a_out, b_out, g_out, bonus_out, hbuf, *, ts):
    s = pl.program_id(1)

    @pl.when(s == 0)
    def _():
        hbuf[...] = jnp.zeros((SUBLANES, D_MODEL), F32)

    hn = _rms(x_ref[0], g_ref[...])
    xx = (_shift_rows(hn, hbuf[...], 1) - hn).astype(BF16)
    hbuf[...] = hn[ts - SUBLANES:, :]
    hn_bf = hn.astype(BF16)
    mu_bf = mu_ref[...].astype(BF16)

    def mix(i):
        return hn_bf + xx * mu_bf[i:i + 1, :]

    r = _bdot(mix(0), wrkv_ref[0])
    k = _bdot(mix(2), wrkv_ref[1])
    v = _bdot(mix(3), wrkv_ref[2])
    z = w0_ref[...] + _bdot(jnp.tanh(_bdot(mix(1), w1_ref[...])), w2_ref[...])
    lw = DECAY_LOG_SCALE * _sigmoid(z)
    alr = _sigmoid(a0_ref[...] + _bdot(_bdot(mix(4), a1_ref[...]), a2_ref[...]))
    g = _bdot(_sigmoid(_bdot(mix(5), g1_ref[...])), g2_ref[...])

    kk = k * kk_ref[...]
    kk = kk * lax.rsqrt(jnp.maximum(_head_sum(kk * kk, e_ref, exact=False), 1e-24))
    ka = ka_ref[...]
    k = k * ((1.0 - ka) + alr * ka)
    L = min(SCAN_CHUNK, ts)
    rows = min(QUAD, ts)
    ti = lax.broadcasted_iota(jnp.int32, (rows, rows), 0)
    si = lax.broadcasted_iota(jnp.int32, (rows, rows), 1)
    tri = ((ti // L == si // L) & (ti >= si)).astype(BF16)
    lw_hi = lw.astype(BF16)
    lw_lo = (lw - lw_hi.astype(F32)).astype(BF16)
    r_out[0] = r.astype(BF16)
    c_out[0] = jnp.concatenate(
        [jnp.dot(tri, lw_hi[i:i + rows], preferred_element_type=F32)
         + jnp.dot(tri, lw_lo[i:i + rows], preferred_element_type=F32) for i in range(0, ts, rows)], axis=0)
    k_out[0] = k.astype(BF16)
    v_out[0] = v.astype(BF16)
    a_out[0] = (-kk).astype(BF16)
    b_out[0] = (kk * alr).astype(BF16)
    g_out[0] = g.astype(BF16)
    bonus_out[0] = (_head_sum(r * k * rk_ref[...], e_ref, exact=False) * v).astype(BF16)


def _rwkv_pre(x, g0, mu, w_rkv_bf, w0, w1_bf, w2_bf, a0, a1_bf, a2_bf, g1_bf, g2_bf, k_k, k_a, r_k, e):
    B, S, D = x.shape
    ts = min(PRE_TILE, S)
    tile = pl.BlockSpec((1, ts, D), lambda b, s: (b, s, 0))
    consts = [g0, mu, w_rkv_bf, w0, w1_bf, w2_bf, a0, a1_bf, a2_bf, g1_bf, g2_bf, k_k, k_a, r_k, e]
    return pl.pallas_call(
        functools.partial(_rwkv_pre_kernel, ts=ts),
        grid=(B, S // ts),
        in_specs=[tile] + [_const_spec(c.shape, single=c.size >= D * RWKV_HEAD) for c in consts],
        out_specs=[tile] * 8,
        out_shape=[jax.ShapeDtypeStruct((B, S, D), F32 if i == 1 else BF16) for i in range(8)],
        scratch_shapes=[pltpu.VMEM((SUBLANES, D), F32)],
        compiler_params=_cparams("arbitrary", "arbitrary"),
        name="rwkv_pre",
    )(x, *consts)


_BNN = (((2,), (1,)), ((0,), (0,)))
_BNT = (((2,), (2,)), ((0,), (0,)))
_BTN = (((1,), (1,)), ((0,), (0,)))


def _qdot(a, b, dims=_BNN):
    return lax.dot_general(a.astype(SCAN_DTYPE), b.astype(SCAN_DTYPE), dims, preferred_element_type=F32)


def _rwkv_scan_kernel(r_ref, c_ref, k_ref, v_ref, a_ref, b_ref, y_ref, state, *, L):
    @pl.when(pl.program_id(1) == 0)
    def _():
        state[...] = jnp.zeros(state.shape, F32)

    G = c_ref.shape[0]
    c = c_ref[...]
    first_row = lax.broadcasted_iota(jnp.int32, c.shape, 1) == 0
    c_prev = jnp.where(first_row, 0.0, pltpu.roll(c, shift=1, axis=1))
    c_end = c[:, L - 1:L, :]
    gam = jnp.exp(c)
    gam_inv = jnp.exp(-c)
    gam_end = jnp.exp(c_end - c)
    b_in = b_ref[...].astype(F32)
    k_in = k_ref[...].astype(F32)
    r_t = r_ref[...].astype(F32) * gam
    a_t = a_ref[...].astype(F32) * jnp.exp(c_prev)
    b_t = b_in * gam_inv
    k_t = k_in * gam_inv
    b_h = b_in * gam_end
    k_h = k_in * gam_end
    gam_last = jnp.exp(c_end)

    hq = QUAD // RWKV_HEAD
    t_idx = lax.broadcasted_iota(jnp.int32, (L, hq * L), 0)
    s_idx = lax.broadcasted_iota(jnp.int32, (L, hq * L), 1) % L
    m_strict = t_idx > s_idx
    m_incl = t_idx >= s_idx
    eye = (t_idx == s_idx).astype(F32)
    m_off = []
    m = 1
    while m < L:
        m_off.append(((t_idx // (2 * m)) == (s_idx // (2 * m))) & ((t_idx % (2 * m)) >= m)
                     & ((s_idx % (2 * m)) < m))
        m *= 2
    qrow = lax.broadcasted_iota(jnp.int32, (QUAD, QUAD), 0)
    qcol = lax.broadcasted_iota(jnp.int32, (QUAD, QUAD), 1)
    m_state = (qrow // RWKV_HEAD) == (qcol // RWKV_HEAD)
    lane_head = lax.broadcasted_iota(jnp.int32, (L, QUAD), 1) // RWKV_HEAD
    lane_blk = lax.broadcasted_iota(jnp.int32, (L, hq * L), 1) // L

    def quads(z):
        return jnp.stack([z[g][:, q * QUAD:(q + 1) * QUAD] for g in range(G) for q in range(N_QUADS)], axis=0)

    def bdiag(z, blk):
        z = z.astype(SCAN_DTYPE)
        zero = jnp.zeros_like(z)
        return jnp.concatenate([jnp.where(blk == h, z, zero) for h in range(hq)], axis=1)

    a_q, r_q, v_q, b_q, k_q = quads(a_t), quads(r_t), quads(v_ref[...]), quads(b_t), quads(k_t)
    ar = jnp.concatenate([a_q, r_q], axis=1)
    g_b = _qdot(ar, bdiag(b_q, lane_head), _BNT)
    g_k = _qdot(ar, bdiag(k_q, lane_head), _BNT)
    n_ab = g_b[:, :L]
    a_rb = jnp.where(m_incl, g_b[:, L:], 0.0)
    a_ak = jnp.where(m_strict, g_k[:, :L], 0.0)
    a_rk = jnp.where(m_incl, g_k[:, L:], 0.0)
    tinv = eye + jnp.where(m_off[0], n_ab, 0.0)
    levels = [jnp.where(m_lvl, n_ab, 0.0) for m_lvl in m_off[1:]]
    for start in range(0, len(levels), LEVEL_GROUP):
        group = levels[start:start + LEVEL_GROUP]
        z = _qdot(jnp.concatenate(group, axis=1), bdiag(tinv, lane_blk))
        pend = [z[:, i * L:(i + 1) * L] for i in range(len(group))]
        while pend:
            p_lvl = pend.pop(0)
            z = _qdot(jnp.concatenate([tinv] + pend, axis=1), bdiag(p_lvl, lane_blk))
            tinv = tinv + z[:, :L]
            pend = [q + z[:, (i + 1) * L:(i + 2) * L] for i, q in enumerate(pend)]
    s0 = state[...]
    xs = _qdot(ar, s0, _BNT)
    vs = _qdot(jnp.concatenate([a_ak, a_rk], axis=1), bdiag(v_q, lane_head))
    u = _qdot(tinv, bdiag(xs[:, :L] + vs[:, :L], lane_head))
    y = xs[:, L:] + vs[:, L:] + _qdot(a_rb, bdiag(u, lane_head))
    upd = _qdot(jnp.concatenate([u, v_q], axis=1), jnp.concatenate([quads(b_h), quads(k_h)], axis=1), _BTN)
    state[...] = s0 * quads(gam_last) + jnp.where(m_state, upd, 0.0)
    for g in range(G):
        y_ref[g] = jnp.concatenate([y[g * N_QUADS + q] for q in range(N_QUADS)], axis=1)


def _rwkv_scan(r, c, k, v, a, b):
    B, S, D = r.shape
    L = min(SCAN_CHUNK, S)
    G = SCAN_ROWS if B % SCAN_ROWS == 0 else 1
    tile = pl.BlockSpec((G, L, D), lambda bb, c: (bb, c, 0))
    return pl.pallas_call(
        functools.partial(_rwkv_scan_kernel, L=L),
        grid=(B // G, S // L),
        in_specs=[tile] * 6,
        out_specs=tile,
        out_shape=jax.ShapeDtypeStruct((B, S, D), F32),
        scratch_shapes=[pltpu.VMEM((G * N_QUADS, QUAD, QUAD), F32)],
        compiler_params=_cparams("arbitrary", "arbitrary"),
        name="rwkv_scan",
    )(r, c, k, v, a, b)


def _rwkv_post_block(x, y, bonus, gate, g_ref, gng_ref, gnb_ref, e_ref, wmix_ref):
    mean = _head_sum(y, e_ref)
    d = y - mean
    yield
    var = _head_sum(d * d, e_ref, exact=False)
    yn = d * lax.rsqrt(var + RWKV_GN_EPS) * gng_ref[...] + gnb_ref[...]
    yield
    t = _bdot((yn + bonus) * gate, wmix_ref[...])
    return x + _rms(t, g_ref[...])


_TAIL_ORDER = ("b", "b", "b") + ("a", "b") * 3 + ("b",) * (MLP_PARTS - 3)


def _rwkv_tail_kernel(x_ref, y_ref, bonus_ref, gate_ref, k_ref, v_ref, gains_ref, gng_ref, gnb_ref, e_ref,
                      wmix_ref, wq_ref, wo_ref, wup_ref, wdn_ref, o_ref, mid):
    @pl.when(pl.program_id(0) == 0)
    def _():
        mid[...] = jnp.zeros(mid.shape, F32)

    done = _run_interleaved(_TAIL_ORDER, {
        "a": _rwkv_post_block(x_ref[0], y_ref[0], bonus_ref[0], gate_ref[0], gains_ref.at[1:2], gng_ref,
                              gnb_ref, e_ref, wmix_ref),
        "b": _xattn_mlp_block(mid[...], k_ref, v_ref, gains_ref, wq_ref, wo_ref, wup_ref, wdn_ref),
    })
    o_ref[0] = done["b"]
    mid[...] = done["a"]


def _rwkv_tail(x, y, bonus, gate, kv, layer, gains, gn_g, gn_b, e_mean, w_mix_bf, w_q_bf, w_o_bf, w_up_bf,
               w_down_bf):
    B, S, D = x.shape
    M = kv.shape[2]
    ts = min(TAIL_TILE, S)
    n = S // ts
    last = B * n - 1

    def cur(i):
        j = jnp.minimum(i, last)
        return j // n, j % n

    def prev(i):
        j = jnp.maximum(i - 1, 0)
        return j // n, j % n

    tile_in = pl.BlockSpec((1, ts, D), lambda i: (*cur(i), 0))
    weights = [w_mix_bf, w_q_bf, w_o_bf, w_up_bf, w_down_bf]
    return pl.pallas_call(
        _rwkv_tail_kernel,
        grid=(B * n + 1,),
        in_specs=[tile_in] * 4 + [
            pl.BlockSpec((None, None, M, D), lambda i: (layer, prev(i)[0], 0, 0)),
            pl.BlockSpec((None, None, M, D), lambda i: (layer, prev(i)[0], 0, 1)),
            _const_spec(gains.shape), _const_spec(gn_g.shape), _const_spec(gn_b.shape),
            _const_spec(e_mean.shape),
        ] + [_const_spec(w.shape, single=True) for w in weights],
        out_specs=pl.BlockSpec((1, ts, D), lambda i: (*prev(i), 0)),
        out_shape=jax.ShapeDtypeStruct((B, S, D), F32),
        scratch_shapes=[pltpu.VMEM((ts, D), F32)],
        compiler_params=_cparams("arbitrary"),
        name="rwkv_tail",
    )(x, y, bonus, gate, kv, kv, gains, gn_g, gn_b, e_mean, *weights)


def _head_selector(value):
    head = jnp.arange(QUAD) // RWKV_HEAD
    return jnp.where(head[:, None] == head[None, :], value, 0.0).astype(BF16)


def kernel(x, mem, ln_gains, mem_norm, a_conv_w, a_conv_b, a_w_in, a_b_in, a_gate_w, a_gate_b, a_lambda, a_w_out, a_b_out, b_mu, b_w_rkv, b_w0, b_w1, b_w2, b_a0, b_a1, b_a2, b_g1, b_g2, b_k_k, b_k_a, b_r_k, b_gn_g, b_gn_b, b_w_o, c_w_q, c_w_kv, c_w_o, m_w_up, m_w_down):
    B, S, D = x.shape
    M = mem.shape[1]
    depth = ln_gains.shape[0]
    assert D == D_MODEL and depth == 2
    bf = lambda w: w.astype(BF16)
    row = lambda p: p.reshape(1, -1)

    kv = _mem_kv(mem.reshape(B * M, D), row(mem_norm), bf(c_w_kv)).reshape(depth, B, M, 2 * D)
    w_q_scaled = bf(c_w_q * (MEM_HEAD_DIM ** -0.5))
    e = _head_selector(1.0)
    e_mean = _head_selector(1.0 / RWKV_HEAD)

    for i in range(depth):
        g = ln_gains[i]
        j = i // 2
        if i % 2 == 0:
            x = _layer0(x, kv, i, g, a_conv_w[j], row(a_conv_b[j]), bf(a_w_in[j]), row(a_b_in[j]),
                        bf(a_gate_w[j]), a_gate_b[j].reshape(2, D), row(a_lambda[j]), bf(a_w_out[j]),
                        row(a_b_out[j]), w_q_scaled[i], bf(c_w_o[i]), bf(m_w_up[i]), bf(m_w_down[i]))
        else:
            r, c, k, v, a, b, gate, bonus = _rwkv_pre(
                x, g[0:1], b_mu[j], bf(b_w_rkv[j]), row(b_w0[j]), bf(b_w1[j]), bf(b_w2[j]), row(b_a0[j]),
                bf(b_a1[j]), bf(b_a2[j]), bf(b_g1[j]), bf(b_g2[j]), row(b_k_k[j]), row(b_k_a[j]),
                row(b_r_k[j]), e)
            y = _rwkv_scan(r, c, k, v, a, b)
            x = _rwkv_tail(x, y, bonus, gate, kv, i, g, row(b_gn_g[j]), row(b_gn_b[j]), e_mean, bf(b_w_o[j]),
                           w_q_scaled[i], bf(c_w_o[i]), bf(m_w_up[i]), bf(m_w_down[i]))
    return x
```

```python
import functools

import jax
import jax.numpy as jnp
from jax import lax
from jax.experimental import pallas as pl
from jax.experimental.pallas import tpu as pltpu

F32 = jnp.float32
BF16 = jnp.bfloat16

D_MODEL = 1024
RMS_EPS = 1e-6
LRU_HEADS = 4
LRU_BLOCK = D_MODEL // LRU_HEADS
CONV_WIDTH = 4
LRU_C = 8.0
RWKV_HEAD = 64
RWKV_HEADS = D_MODEL // RWKV_HEAD
RWKV_GN_EPS = 64e-5
DECAY_LOG_SCALE = -0.6065306597126334
GELU_C = 0.7978845608028654
MEM_HEADS = 4
MEM_HEAD_DIM = D_MODEL // MEM_HEADS
D_FF = 4 * D_MODEL

LANES = 128
SUBLANES = 8
QUAD = 256
N_QUADS = D_MODEL // QUAD
SEQ_TILE = 512
PRE_TILE = 512
TAIL_TILE = 512
MLP_PARTS = 8
SCAN_CHUNK = 64
LEVEL_GROUP = 3
SCAN_ROWS = 8
VMEM_LIMIT = 56 * 1024 * 1024

SCAN_DTYPE = BF16


def _cparams(*sem):
    return pltpu.CompilerParams(dimension_semantics=sem, vmem_limit_bytes=VMEM_LIMIT)


def _rms(x, g):
    return x * lax.rsqrt(jnp.mean(x * x, axis=-1, keepdims=True) + RMS_EPS) * g


def _sigmoid(x):
    return 1.0 / (1.0 + jnp.exp(-x))


def _softplus(x):
    return jnp.maximum(x, 0.0) + jnp.log(1.0 + jnp.exp(-jnp.abs(x)))


def _gelu_tanh(x):
    half = 0.5 * x
    return half + half * jnp.tanh(x * (GELU_C + (GELU_C * 0.044715) * (x * x)))


def _bdot(a, b):
    return jnp.dot(a.astype(BF16), b.astype(BF16), preferred_element_type=F32)


def _shift_rows(x, tail, k):
    rolled = pltpu.roll(x, shift=k, axis=0)
    row = lax.broadcasted_iota(jnp.int32, (SUBLANES, x.shape[1]), 0)
    head = jnp.where(row < k, pltpu.roll(tail, shift=k, axis=0), rolled[:SUBLANES])
    return jnp.concatenate([head, rolled[SUBLANES:]], axis=0)


def _const_spec(shape, single=False):
    nd = len(shape)
    if single:
        return pl.BlockSpec(shape, lambda *_: (0,) * nd, pipeline_mode=pl.Buffered(1))
    return pl.BlockSpec(shape, lambda *_: (0,) * nd)


def _memkv_kernel(mem_ref, gn_ref, wkv_ref, out_ref):
    mn = _rms(mem_ref[...], gn_ref[...])
    out_ref[...] = _bdot(mn, wkv_ref[...]).astype(BF16)


def _mem_kv(mem2d, mem_norm, w_kv_bf):
    rows = mem2d.shape[0]
    depth = w_kv_bf.shape[0]
    tm = min(1024, rows)
    return pl.pallas_call(
        _memkv_kernel,
        grid=(depth, rows // tm),
        in_specs=[
            pl.BlockSpec((tm, D_MODEL), lambda i, r: (r, 0)),
            _const_spec((1, D_MODEL)),
            pl.BlockSpec((None, D_MODEL, 2 * D_MODEL), lambda i, r: (i, 0, 0)),
        ],
        out_specs=pl.BlockSpec((None, tm, 2 * D_MODEL), lambda i, r: (i, r, 0)),
        out_shape=jax.ShapeDtypeStruct((depth, rows, 2 * D_MODEL), BF16),
        compiler_params=_cparams("arbitrary", "arbitrary"),
        name="mem_kv",
    )(mem2d, mem_norm, w_kv_bf)


def _run_interleaved(order, gens):
    done = {}
    for name in order:
        try:
            next(gens[name])
        except StopIteration as stop:
            done[name] = stop.value
    return done


def _rglru_block(x, g_ref, convw_ref, convb_ref, win_ref, bin_ref, gw_ref, gb_ref, lam_ref,
                 wout_ref, bout_ref, ubuf, a_s, b_s, h_s, hcar):
    ts = x.shape[0]
    hn = _rms(x, g_ref[0:1, :])
    proj = _bdot(hn, win_ref[...]) + bin_ref[...]
    yield
    yb = _gelu_tanh(proj[:, :D_MODEL])
    u = proj[:, D_MODEL:]
    tail = ubuf[...]
    conv = convb_ref[...] + u * convw_ref[CONV_WIDTH - 1:CONV_WIDTH, :]
    for tap in range(CONV_WIDTH - 1):
        conv = conv + _shift_rows(u, tail, CONV_WIDTH - 1 - tap) * convw_ref[tap:tap + 1, :]
    ubuf[...] = u[ts - SUBLANES:, :]
    yield
    conv_bf = conv.astype(BF16)
    r_parts, i_parts = [], []
    for h in range(LRU_HEADS):
        cb = conv_bf[:, h * LRU_BLOCK:(h + 1) * LRU_BLOCK]
        r_parts.append(jnp.dot(cb, gw_ref[0, h], preferred_element_type=F32))
        i_parts.append(jnp.dot(cb, gw_ref[1, h], preferred_element_type=F32))
    r_gate = _sigmoid(jnp.concatenate(r_parts, axis=1) + gb_ref[0:1, :])
    i_gate = _sigmoid(jnp.concatenate(i_parts, axis=1) + gb_ref[1:2, :])
    a = jnp.exp(r_gate * ((-LRU_C) * _softplus(-lam_ref[...])))
    a_s[...] = a
    b_s[...] = jnp.sqrt(1.0 - a * a) * i_gate * conv
    yield

    def body(t, h):
        h = a_s[pl.ds(t, 1), :] * h + b_s[pl.ds(t, 1), :]
        h_s[pl.ds(t, 1), :] = h
        return h

    hcar[...] = lax.fori_loop(0, ts, body, hcar[...], unroll=True)
    yield
    out =_bdot(h_s[...] * yb, wout_ref[...]) + bout_ref[...]
    return x + _rms(out, g_ref[1:2, :])


def _xattn_block(x, k_ref, v_ref, g_ref, wq_ref, wo_ref):
    hn = _rms(x, g_ref[0:1, :])
    q = _bdot(hn, wq_ref[...]).astype(BF16)
    yield
    outs = []
    for h in range(MEM_HEADS):
        sl = slice(h * MEM_HEAD_DIM, (h + 1) * MEM_HEAD_DIM)
        sc = lax.dot_general(q[:, sl], k_ref[:, sl], (((1,), (1,)), ((), ())),
                             preferred_element_type=F32)
        p = jnp.exp(sc - jnp.max(sc, axis=-1, keepdims=True))
        l = jnp.sum(p, axis=-1, keepdims=True)
        outs.append(jnp.dot(p.astype(BF16), v_ref[:, sl], preferred_element_type=F32) / l)
    yield
    c = _bdot(jnp.concatenate(outs, axis=1), wo_ref[...])
    return x + _rms(c, g_ref[1:2, :])


def _mlp_block(x, g_ref, wup_ref, wdn_ref):
    hn = _rms(x, g_ref[0:1, :]).astype(BF16)
    acc = None
    for c in range(MLP_PARTS):
        sl = slice(c * (D_FF // MLP_PARTS), (c + 1) * (D_FF // MLP_PARTS))
        h = jnp.maximum(jnp.dot(hn, wup_ref[:, sl], preferred_element_type=F32), 0.0)
        part = jnp.dot((h * h).astype(BF16), wdn_ref[sl, :], preferred_element_type=F32)
        acc = part if acc is None else acc + part
        if c + 1 < MLP_PARTS:
            yield
    return x + _rms(acc, g_ref[1:2, :])


def _xattn_mlp_block(x, k_ref, v_ref, gains_ref, wq_ref, wo_ref, wup_ref, wdn_ref):
    x1 = yield from _xattn_block(x, k_ref, v_ref, gains_ref.at[2:4], wq_ref, wo_ref)
    yield
    return (yield from _mlp_block(x1, gains_ref.at[4:6], wup_ref, wdn_ref))


_LAYER0_ORDER = ("b", "a", "b", "b") + ("a", "b") * 4 + ("b",) * (MLP_PARTS - 4)


def _layer0_kernel(x_ref, k_ref, v_ref, gains_ref, convw_ref, convb_ref, win_ref, bin_ref, gw_ref, gb_ref,
                   lam_ref, wout_ref, bout_ref, wq_ref, wo_ref, wup_ref, wdn_ref, o_ref,
                   ubuf, a_s, b_s, h_s, hcar, mid, *, tiles_per_seq):
    i = pl.program_id(0)

    @pl.when(i == 0)
    def _():
        mid[...] = jnp.zeros(mid.shape, F32)

    @pl.when(i % tiles_per_seq == 0)
    def _():
        ubuf[...] = jnp.zeros(ubuf.shape, F32)
        hcar[...] = jnp.zeros(hcar.shape, F32)

    done = _run_interleaved(_LAYER0_ORDER, {
        "a": _rglru_block(x_ref[0], gains_ref.at[0:2], convw_ref, convb_ref, win_ref, bin_ref, gw_ref,
                          gb_ref, lam_ref, wout_ref, bout_ref, ubuf, a_s, b_s, h_s, hcar),
        "b": _xattn_mlp_block(mid[...], k_ref, v_ref, gains_ref, wq_ref, wo_ref, wup_ref, wdn_ref),
    })
    o_ref[0] = done["b"]
    mid[...] = done["a"]


def _layer0(x, kv, layer, gains, conv_w, conv_b, w_in_bf, b_in, gate_w_bf, gate_b, lam, w_out_bf, b_out,
            w_q_bf, w_o_bf, w_up_bf, w_down_bf):
    B, S, D = x.shape
    M = kv.shape[2]
    ts = min(SEQ_TILE, S)
    n = S // ts
    last = B * n - 1

    def cur(i):
        j = jnp.minimum(i, last)
        return j // n, j % n

    def prev(i):
        j = jnp.maximum(i - 1, 0)
        return j // n, j % n

    weights = [conv_w, conv_b, w_in_bf, b_in, gate_w_bf, gate_b, lam, w_out_bf, b_out,
               w_q_bf, w_o_bf, w_up_bf, w_down_bf]
    return pl.pallas_call(
        functools.partial(_layer0_kernel, tiles_per_seq=n),
        grid=(B * n + 1,),
        in_specs=[
            pl.BlockSpec((1, ts, D), lambda i: (*cur(i), 0)),
            pl.BlockSpec((None, None, M, D), lambda i: (layer, prev(i)[0], 0, 0)),
            pl.BlockSpec((None, None, M, D), lambda i: (layer, prev(i)[0], 0, 1)),
            _const_spec(gains.shape),
        ] + [_const_spec(w.shape, single=True) for w in weights],
        out_specs=pl.BlockSpec((1, ts, D), lambda i: (*prev(i), 0)),
        out_shape=jax.ShapeDtypeStruct((B, S, D), F32),
        scratch_shapes=[
            pltpu.VMEM((SUBLANES, D), F32),
            pltpu.VMEM((ts, D), F32),
            pltpu.VMEM((ts, D), F32),
            pltpu.VMEM((ts, D), F32),
            pltpu.VMEM((1, D), F32),
            pltpu.VMEM((ts, D), F32),
        ],
        compiler_params=_cparams("arbitrary"),
        name="layer0",
    )(x, kv, kv, gains, *weights)


def _head_sum(x, e_ref, exact=True):
    hi = x.astype(BF16)
    lo = (x - hi.astype(F32)).astype(BF16) if exact else None
    e = e_ref[...]
    parts = []
    for q in range(N_QUADS):
        sl = slice(q * QUAD, (q + 1) * QUAD)
        part = jnp.dot(hi[:, sl], e, preferred_element_type=F32)
        if exact:
            part = part + jnp.dot(lo[:, sl], e, preferred_element_type=F32)
        parts.append(part)
    return jnp.concatenate(parts, axis=1)


def _rwkv_pre_kernel(x_ref, g_ref, mu_ref, wrkv_ref, w0_ref, w1_ref, w2_ref, a0_ref, a1_ref, a2_ref,
                     g1_ref, g2_ref, kk_ref, ka_ref, rk_ref, e_ref,
                     r_out, c_out, k_out, v_out, a_out, b_out, g_out, bonus_out, hbuf, *, ts):
    s = pl.program_id(1)

    @pl.when(s == 0)
    def _():
        hbuf[...] = jnp.zeros((SUBLANES, D_MODEL), F32)

    hn = _rms(x_ref[0], g_ref[...])
    xx = (_shift_rows(hn, hbuf[...], 1) - hn).astype(BF16)
    hbuf[...] = hn[ts - SUBLANES:, :]
    hn_bf = hn.astype(BF16)
    mu_bf = mu_ref[...].astype(BF16)

    def mix(i):
        return hn_bf + xx * mu_bf[i:i + 1, :]

    r = _bdot(mix(0), wrkv_ref[0])
    k = _bdot(mix(2), wrkv_ref[1])
    v = _bdot(mix(3), wrkv_ref[2])
    z = w0_ref[...] + _bdot(jnp.tanh(_bdot(mix(1), w1_ref[...])), w2_ref[...])
    lw = DECAY_LOG_SCALE * _sigmoid(z)
    alr = _sigmoid(a0_ref[...] + _bdot(_bdot(mix(4), a1_ref[...]), a2_ref[...]))
    g = _bdot(_sigmoid(_bdot(mix(5), g1_ref[...])), g2_ref[...])

    kk = k * kk_ref[...]
    kk = kk * lax.rsqrt(jnp.maximum(_head_sum(kk * kk, e_ref, exact=False), 1e-24))
    ka = ka_ref[...]
    k = k * ((1.0 - ka) + alr * ka)
    L = min(SCAN_CHUNK, ts)
    rows = min(QUAD, ts)
    ti = lax.broadcasted_iota(jnp.int32, (rows, rows), 0)
    si = lax.broadcasted_iota(jnp.int32, (rows, rows), 1)
    tri = ((ti // L == si // L) & (ti >= si)).astype(BF16)
    lw_hi = lw.astype(BF16)
    lw_lo = (lw - lw_hi.astype(F32)).astype(BF16)
    r_out[0] = r.astype(BF16)
    c_out[0] = jnp.concatenate(
        [jnp.dot(tri, lw_hi[i:i + rows], preferred_element_type=F32)
         + jnp.dot(tri, lw_lo[i:i + rows], preferred_element_type=F32) for i in range(0, ts, rows)], axis=0)
    k_out[0] = k.astype(BF16)
    v_out[0] = v.astype(BF16)
    a_out[0] = (-kk).astype(BF16)
    b_out[0] = (kk * alr).astype(BF16)
    g_out[0] = g.astype(BF16)
    bonus_out[0] = (_head_sum(r * k * rk_ref[...], e_ref, exact=False) * v).astype(BF16)


def _rwkv_pre(x, g0, mu, w_rkv_bf, w0, w1_bf, w2_bf, a0, a1_bf, a2_bf, g1_bf, g2_bf, k_k, k_a, r_k, e):
    B, S, D = x.shape
    ts = min(PRE_TILE, S)
    tile = pl.BlockSpec((1, ts, D), lambda b, s: (b, s, 0))
    consts = [g0, mu, w_rkv_bf, w0, w1_bf, w2_bf, a0, a1_bf, a2_bf, g1_bf, g2_bf, k_k, k_a, r_k, e]
    return pl.pallas_call(
        functools.partial(_rwkv_pre_kernel, ts=ts),
        grid=(B, S // ts),
        in_specs=[tile] + [_const_spec(c.shape, single=c.size >= D * RWKV_HEAD) for c in consts],
        out_specs=[tile] * 8,
        out_shape=[jax.ShapeDtypeStruct((B, S, D), F32 if i == 1 else BF16) for i in range(8)],
        scratch_shapes=[pltpu.VMEM((SUBLANES, D), F32)],
        compiler_params=_cparams("arbitrary", "arbitrary"),
        name="rwkv_pre",
    )(x, *consts)


_BNN = (((2,), (1,)), ((0,), (0,)))
_BNT = (((2,), (2,)), ((0,), (0,)))
_BTN = (((1,), (1,)), ((0,), (0,)))


def _qdot(a, b, dims=_BNN):
    return lax.dot_general(a.astype(SCAN_DTYPE), b.astype(SCAN_DTYPE), dims, preferred_element_type=F32)


def _rwkv_scan_kernel(r_ref, c_ref, k_ref, v_ref, a_ref, b_ref, y_ref, state, *, L):
    @pl.when(pl.program_id(1) == 0)
    def _():
        state[...] = jnp.zeros(state.shape, F32)

    G = c_ref.shape[0]
    c = c_ref[...]
    first_row = lax.broadcasted_iota(jnp.int32, c.shape, 1) == 0
    c_prev = jnp.where(first_row, 0.0, pltpu.roll(c, shift=1, axis=1))
    c_end = c[:, L - 1:L, :]
    gam = jnp.exp(c)
    gam_inv = jnp.exp(-c)
    gam_end = jnp.exp(c_end - c)
    b_in = b_ref[...].astype(F32)
    k_in = k_ref[...].astype(F32)
    r_t = r_ref[...].astype(F32) * gam
    a_t = a_ref[...].astype(F32) * jnp.exp(c_prev)
    b_t = b_in * gam_inv
    k_t = k_in * gam_inv
    b_h = b_in * gam_end
    k_h = k_in * gam_end
    gam_last = jnp.exp(c_end)

    hq = QUAD // RWKV_HEAD
    t_idx = lax.broadcasted_iota(jnp.int32, (L, hq * L), 0)
    s_idx = lax.broadcasted_iota(jnp.int32, (L, hq * L), 1) % L
    m_strict = t_idx > s_idx
    m_incl = t_idx >= s_idx
    eye = (t_idx == s_idx).astype(F32)
    m_off = []
    m = 1
    while m < L:
        m_off.append(((t_idx // (2 * m)) == (s_idx // (2 * m))) & ((t_idx % (2 * m)) >= m)
                     & ((s_idx % (2 * m)) < m))
        m *= 2
    qrow = lax.broadcasted_iota(jnp.int32, (QUAD, QUAD), 0)
    qcol = lax.broadcasted_iota(jnp.int32, (QUAD, QUAD), 1)
    m_state = (qrow // RWKV_HEAD) == (qcol // RWKV_HEAD)
    lane_head = lax.broadcasted_iota(jnp.int32, (L, QUAD), 1) // RWKV_HEAD
    lane_blk = lax.broadcasted_iota(jnp.int32, (L, hq * L), 1) // L

    def quads(z):
        return jnp.stack([z[g][:, q * QUAD:(q + 1) * QUAD] for g in range(G) for q in range(N_QUADS)], axis=0)

    def bdiag(z, blk):
        z = z.astype(SCAN_DTYPE)
        zero = jnp.zeros_like(z)
        return jnp.concatenate([jnp.where(blk == h, z, zero) for h in range(hq)], axis=1)

    a_q, r_q, v_q, b_q, k_q = quads(a_t), quads(r_t), quads(v_ref[...]), quads(b_t), quads(k_t)
    ar = jnp.concatenate([a_q, r_q], axis=1)
    g_b = _qdot(ar, bdiag(b_q, lane_head), _BNT)
    g_k = _qdot(ar, bdiag(k_q, lane_head), _BNT)
    n_ab = g_b[:, :L]
    a_rb = jnp.where(m_incl, g_b[:, L:], 0.0)
    a_ak = jnp.where(m_strict, g_k[:, :L], 0.0)
    a_rk = jnp.where(m_incl, g_k[:, L:], 0.0)
    tinv = eye + jnp.where(m_off[0], n_ab, 0.0)
    levels = [jnp.where(m_lvl, n_ab, 0.0) for m_lvl in m_off[1:]]
    for start in range(0, len(levels), LEVEL_GROUP):
        group = levels[start:start + LEVEL_GROUP]
        z = _qdot(jnp.concatenate(group, axis=1), bdiag(tinv, lane_blk))
        pend = [z[:, i * L:(i + 1) * L] for i in range(len(group))]
        while pend:
            p_lvl = pend.pop(0)
            z = _qdot(jnp.concatenate([tinv] + pend, axis=1), bdiag(p_lvl, lane_blk))
            tinv = tinv + z[:, :L]
            pend = [q + z[:, (i + 1) * L:(i + 2) * L] for i, q in enumerate(pend)]
    s0 = state[...]
    xs = _qdot(ar, s0, _BNT)
    vs = _qdot(jnp.concatenate([a_ak, a_rk], axis=1), bdiag(v_q, lane_head))
    u = _qdot(tinv, bdiag(xs[:, :L] + vs[:, :L], lane_head))
    y = xs[:, L:] + vs[:, L:] + _qdot(a_rb, bdiag(u, lane_head))
    upd = _qdot(jnp.concatenate([u, v_q], axis=1), jnp.concatenate([quads(b_h), quads(k_h)], axis=1), _BTN)
    state[...] = s0 * quads(gam_last) + jnp.where(m_state, upd, 0.0)
    for g in range(G):
        y_ref[g] = jnp.concatenate([y[g * N_QUADS + q] for q in range(N_QUADS)], axis=1)


def _rwkv_scan(r, c, k, v, a, b):
    B, S, D = r.shape
    L = min(SCAN_CHUNK, S)
    G = SCAN_ROWS if B % SCAN_ROWS == 0 else 1
    tile = pl.BlockSpec((G, L, D), lambda bb, c: (bb, c, 0))
    return pl.pallas_call(
        functools.partial(_rwkv_scan_kernel, L=L),
        grid=(B // G, S // L),
        in_specs=[tile] * 6,
        out_specs=tile,
        out_shape=jax.ShapeDtypeStruct((B, S, D), F32),
        scratch_shapes=[pltpu.VMEM((G * N_QUADS, QUAD, QUAD), F32)],
        compiler_params=_cparams("arbitrary", "arbitrary"),
        name="rwkv_scan",
    )(r, c, k, v, a, b)


def _rwkv_post_block(x, y, bonus, gate, g_ref, gng_ref, gnb_ref, e_ref, wmix_ref):
    mean = _head_sum(y, e_ref)
    d = y - mean
    yield
    var = _head_sum(d * d, e_ref, exact=False)
    yn = d * lax.rsqrt(var + RWKV_GN_EPS) * gng_ref[...] + gnb_ref[...]
    yield
    t = _bdot((yn + bonus) * gate, wmix_ref[...])
    return x + _rms(t, g_ref[...])


_TAIL_ORDER = ("b", "b", "b") + ("a", "b") * 3 + ("b",) * (MLP_PARTS - 3)


def _rwkv_tail_kernel(x_ref, y_ref, bonus_ref, gate_ref, k_ref, v_ref, gains_ref, gng_ref, gnb_ref, e_ref,
                      wmix_ref, wq_ref, wo_ref, wup_ref, wdn_ref, o_ref, mid):
    @pl.when(pl.program_id(0) == 0)
    def _():
        mid[...] = jnp.zeros(mid.shape, F32)

    done = _run_interleaved(_TAIL_ORDER, {
        "a": _rwkv_post_block(x_ref[0], y_ref[0], bonus_ref[0], gate_ref[0], gains_ref.at[1:2], gng_ref,
                              gnb_ref, e_ref, wmix_ref),
        "b": _xattn_mlp_block(mid[...], k_ref, v_ref, gains_ref, wq_ref, wo_ref, wup_ref, wdn_ref),
    })
    o_ref[0] = done["b"]
    mid[...] = done["a"]


def _rwkv_tail(x, y, bonus, gate, kv, layer, gains, gn_g, gn_b, e_mean, w_mix_bf, w_q_bf, w_o_bf, w_up_bf,
               w_down_bf):
    B, S, D = x.shape
    M = kv.shape[2]
    ts = min(TAIL_TILE, S)
    n = S // ts
    last = B * n - 1

    def cur(i):
        j = jnp.minimum(i, last)
        return j // n, j % n

    def prev(i):
        j = jnp.maximum(i - 1, 0)
        return j // n, j % n

    tile_in = pl.BlockSpec((1, ts, D), lambda i: (*cur(i), 0))
    weights = [w_mix_bf, w_q_bf, w_o_bf, w_up_bf, w_down_bf]
    return pl.pallas_call(
        _rwkv_tail_kernel,
        grid=(B * n + 1,),
        in_specs=[tile_in] * 4 + [
            pl.BlockSpec((None, None, M, D), lambda i: (layer, prev(i)[0], 0, 0)),
            pl.BlockSpec((None, None, M, D), lambda i: (layer, prev(i)[0], 0, 1)),
            _const_spec(gains.shape), _const_spec(gn_g.shape), _const_spec(gn_b.shape),
            _const_spec(e_mean.shape),
        ] + [_const_spec(w.shape, single=True) for w in weights],
        out_specs=pl.BlockSpec((1, ts, D), lambda i: (*prev(i), 0)),
        out_shape=jax.ShapeDtypeStruct((B, S, D), F32),
        scratch_shapes=[pltpu.VMEM((ts, D), F32)],
        compiler_params=_cparams("arbitrary"),
        name="rwkv_tail",
    )(x, y, bonus, gate, kv, kv, gains, gn_g, gn_b, e_mean, *weights)


def _head_selector(value):
    head = jnp.arange(QUAD) // RWKV_HEAD
    return jnp.where(head[:, None] == head[None, :], value, 0.0).astype(BF16)


def kernel(x, mem, ln_gains, mem_norm, a_conv_w, a_conv_b, a_w_in, a_b_in, a_gate_w, a_gate_b, a_lambda, a_w_out, a_b_out, b_mu, b_w_rkv, b_w0, b_w1, b_w2, b_a0, b_a1, b_a2, b_g1, b_g2, b_k_k, b_k_a, b_r_k, b_gn_g, b_gn_b, b_w_o, c_w_q, c_w_kv, c_w_o, m_w_up, m_w_down):
    B, S, D = x.shape
    M = mem.shape[1]
    depth = ln_gains.shape[0]
    assert D == D_MODEL and depth == 2
    bf = lambda w: w.astype(BF16)
    row = lambda p: p.reshape(1, -1)

    kv = _mem_kv(mem.reshape(B * M, D), row(mem_norm), bf(c_w_kv)).reshape(depth, B, M, 2 * D)
    w_q_scaled = bf(c_w_q * (MEM_HEAD_DIM ** -0.5))
    e = _head_selector(1.0)
    e_mean = _head_selector(1.0 / RWKV_HEAD)

    for i in range(depth):
        g = ln_gains[i]
        j = i // 2
        if i % 2 == 0:
            x = _layer0(x, kv, i, g, a_conv_w[j], row(a_conv_b[j]), bf(a_w_in[j]), row(a_b_in[j]),
                        bf(a_gate_w[j]), a_gate_b[j].reshape(2, D), row(a_lambda[j]), bf(a_w_out[j]),
                        row(a_b_out[j]), w_q_scaled[i], bf(c_w_o[i]), bf(m_w_up[i]), bf(m_w_down[i]))
        else:
            r, c, k, v, a, b, gate, bonus = _rwkv_pre(
                x, g[0:1], b_mu[j], bf(b_w_rkv[j]), row(b_w0[j]), bf(b_w1[j]), bf(b_w2[j]), row(b_a0[j]),
                bf(b_a1[j]), bf(b_a2[j]), bf(b_g1[j]), bf(b_g2[j]), row(b_k_k[j]), row(b_k_a[j]),
                row(b_r_k[j]), e)
            y = _rwkv_scan(r, c, k, v, a, b)
            x = _rwkv_tail(x, y, bonus, gate, kv, i, g, row(b_gn_g[j]), row(b_gn_b[j]), e_mean, bf(b_w_o[j]),
                           w_q_scaled[i], bf(c_w_o[i]), bf(m_w_up[i]), bf(m_w_down[i]))
    return x
```

```python
import functools

import jax
import jax.numpy as jnp
from jax import lax
from jax.experimental import pallas as pl
from jax.experimental.pallas import tpu as pltpu

F32 = jnp.float32
BF16 = jnp.bfloat16

D_MODEL = 1024
RMS_EPS = 1e-6
LRU_HEADS = 4
LRU_BLOCK = D_MODEL // LRU_HEADS
CONV_WIDTH = 4
LRU_C = 8.0
RWKV_HEAD = 64
RWKV_KK_NORM_FLOOR = 1e-12
RWKV_GN_EPS = 64e-5
DECAY_LOG_SCALE = -0.6065306597126334
GELU_C = 0.7978845608028654
MEM_HEADS = 4
MEM_HEAD_DIM = D_MODEL // MEM_HEADS
D_FF = 4 * D_MODEL

SUBLANES = 8
QUAD = 256
N_QUADS = D_MODEL // QUAD
SEQ_TILE = 512
MEMKV_TILE = 1024
PRE_TILE = 512
TAIL_TILE = 512
MLP_PARTS = 8
SCAN_CHUNK = 64
LEVEL_GROUP = 3
SCAN_ROWS = 8
VMEM_LIMIT = 56 * 1024 * 1024

SCAN_DTYPE = BF16


def _cparams(*sem):
    return pltpu.CompilerParams(dimension_semantics=sem, vmem_limit_bytes=VMEM_LIMIT)


def _rms(x, g):
    return x * lax.rsqrt(jnp.mean(x * x, axis=-1, keepdims=True) + RMS_EPS) * g


def _sigmoid(x):
    return 1.0 / (1.0 + jnp.exp(-x))


def _softplus(x):
    return jnp.maximum(x, 0.0) + jnp.log(1.0 + jnp.exp(-jnp.abs(x)))


def _gelu_tanh(x):
    half = 0.5 * x
    return half + half * jnp.tanh(x * (GELU_C + (GELU_C * 0.044715) * (x * x)))


def _bdot(a, b):
    return jnp.dot(a.astype(BF16), b.astype(BF16), preferred_element_type=F32)


def _shift_rows(x, tail, k):
    rolled = pltpu.roll(x, shift=k, axis=0)
    row = lax.broadcasted_iota(jnp.int32, (SUBLANES, x.shape[1]), 0)
    head = jnp.where(row < k, pltpu.roll(tail, shift=k, axis=0), rolled[:SUBLANES])
    return jnp.concatenate([head, rolled[SUBLANES:]], axis=0)


def _const_spec(shape, single=False):
    nd = len(shape)
    if single:
        return pl.BlockSpec(shape, lambda *_: (0,) * nd, pipeline_mode=pl.Buffered(1))
    return pl.BlockSpec(shape, lambda *_: (0,) * nd)


def _memkv_kernel(mem_ref, gn_ref, wkv_ref, out_ref):
    mn = _rms(mem_ref[...], gn_ref[...])
    out_ref[...] = _bdot(mn, wkv_ref[...]).astype(BF16)


def _mem_kv(mem2d, mem_norm, w_kv_bf):
    rows = mem2d.shape[0]
    depth = w_kv_bf.shape[0]
    tm = min(MEMKV_TILE, rows)
    return pl.pallas_call(
        _memkv_kernel,
        grid=(depth, rows // tm),
        in_specs=[
            pl.BlockSpec((tm, D_MODEL), lambda i, r: (r, 0)),
            _const_spec((1, D_MODEL)),
            pl.BlockSpec((None, D_MODEL, 2 * D_MODEL), lambda i, r: (i, 0, 0)),
        ],
        out_specs=pl.BlockSpec((None, tm, 2 * D_MODEL), lambda i, r: (i, r, 0)),
        out_shape=jax.ShapeDtypeStruct((depth, rows, 2 * D_MODEL), BF16),
        compiler_params=_cparams("arbitrary", "arbitrary"),
        name="mem_kv",
    )(mem2d, mem_norm, w_kv_bf)


def _run_interleaved(order, gens):
    done = {}
    for name in order:
        try:
            next(gens[name])
        except StopIteration as stop:
            done[name] = stop.value
    return done


def _rglru_block(x, g_ref, convw_ref, convb_ref, win_ref, bin_ref, gw_ref, gb_ref, lam_ref,
                 wout_ref, bout_ref, ubuf, a_s, b_s, h_s, hcar):
    ts = x.shape[0]
    hn = _rms(x, g_ref[0:1, :])
    proj = _bdot(hn, win_ref[...]) + bin_ref[...]
    yield
    yb = _gelu_tanh(proj[:, :D_MODEL])
    u = proj[:, D_MODEL:]
    tail = ubuf[...]
    conv = convb_ref[...] + u * convw_ref[CONV_WIDTH - 1:CONV_WIDTH, :]
    for tap in range(CONV_WIDTH - 1):
        conv = conv + _shift_rows(u, tail, CONV_WIDTH - 1 - tap) * convw_ref[tap:tap + 1, :]
    ubuf[...] = u[ts - SUBLANES:, :]
    yield
    conv_bf = conv.astype(BF16)
    r_parts, i_parts = [], []
    for h in range(LRU_HEADS):
        cb = conv_bf[:, h * LRU_BLOCK:(h + 1) * LRU_BLOCK]
        r_parts.append(jnp.dot(cb, gw_ref[0, h], preferred_element_type=F32))
        i_parts.append(jnp.dot(cb, gw_ref[1, h], preferred_element_type=F32))
    r_gate = _sigmoid(jnp.concatenate(r_parts, axis=1) + gb_ref[0:1, :])
    i_gate = _sigmoid(jnp.concatenate(i_parts, axis=1) + gb_ref[1:2, :])
    a = jnp.exp(r_gate * ((-LRU_C) * _softplus(-lam_ref[...])))
    a_s[...] = a
    b_s[...] = jnp.sqrt(1.0 - a * a) * i_gate * conv
    yield

    def body(t, h):
        h = a_s[pl.ds(t, 1), :] * h + b_s[pl.ds(t, 1), :]
        h_s[pl.ds(t, 1), :] = h
        return h

    hcar[...] = lax.fori_loop(0, ts, body, hcar[...], unroll=True)
    yield
    out =_bdot(h_s[...] * yb, wout_ref[...]) + bout_ref[...]
    return x + _rms(out, g_ref[1:2, :])


def _xattn_block(x, k_ref, v_ref, g_ref, wq_ref, wo_ref):
    hn = _rms(x, g_ref[0:1, :])
    q = _bdot(hn, wq_ref[...]).astype(BF16)
    yield
    outs = []
    for h in range(MEM_HEADS):
        sl = slice(h * MEM_HEAD_DIM, (h + 1) * MEM_HEAD_DIM)
        sc = lax.dot_general(q[:, sl], k_ref[:, sl], (((1,), (1,)), ((), ())),
                             preferred_element_type=F32)
        p = jnp.exp(sc - jnp.max(sc, axis=-1, keepdims=True))
        l = jnp.sum(p, axis=-1, keepdims=True)
        outs.append(jnp.dot(p.astype(BF16), v_ref[:, sl], preferred_element_type=F32) / l)
    yield
    c = _bdot(jnp.concatenate(outs, axis=1), wo_ref[...])
    return x + _rms(c, g_ref[1:2, :])


def _mlp_block(x, g_ref, wup_ref, wdn_ref):
    hn = _rms(x, g_ref[0:1, :]).astype(BF16)
    acc = None
    for c in range(MLP_PARTS):
        sl = slice(c * (D_FF // MLP_PARTS), (c + 1) * (D_FF // MLP_PARTS))
        h = jnp.maximum(jnp.dot(hn, wup_ref[:, sl], preferred_element_type=F32), 0.0)
        part = jnp.dot((h * h).astype(BF16), wdn_ref[sl, :], preferred_element_type=F32)
        acc = part if acc is None else acc + part
        if c + 1 < MLP_PARTS:
            yield
    return x + _rms(acc, g_ref[1:2, :])


def _xattn_mlp_block(x, k_ref, v_ref, gains_ref, wq_ref, wo_ref, wup_ref, wdn_ref):
    x1 = yield from _xattn_block(x, k_ref, v_ref, gains_ref.at[2:4], wq_ref, wo_ref)
    yield
    return (yield from _mlp_block(x1, gains_ref.at[4:6], wup_ref, wdn_ref))


_LAYER0_ORDER = ("b", "a", "b", "b") + ("a", "b") * 4 + ("b",) * (MLP_PARTS - 4)


def _layer0_kernel(x_ref, k_ref, v_ref, gains_ref, convw_ref, convb_ref, win_ref, bin_ref, gw_ref, gb_ref,
                   lam_ref, wout_ref, bout_ref, wq_ref, wo_ref, wup_ref, wdn_ref, o_ref,
                   ubuf, a_s, b_s, h_s, hcar, mid, *, tiles_per_seq):
    i = pl.program_id(0)

    @pl.when(i == 0)
    def _():
        mid[...] = jnp.zeros(mid.shape, F32)

    @pl.when(i % tiles_per_seq == 0)
    def _():
        ubuf[...] = jnp.zeros(ubuf.shape, F32)
        hcar[...] = jnp.zeros(hcar.shape, F32)

    done = _run_interleaved(_LAYER0_ORDER, {
        "a": _rglru_block(x_ref[0], gains_ref.at[0:2], convw_ref, convb_ref, win_ref, bin_ref, gw_ref,
                          gb_ref, lam_ref, wout_ref, bout_ref, ubuf, a_s, b_s, h_s, hcar),
        "b": _xattn_mlp_block(mid[...], k_ref, v_ref, gains_ref, wq_ref, wo_ref, wup_ref, wdn_ref),
    })
    o_ref[0] = done["b"]
    mid[...] = done["a"]


def _layer0(x, kv, layer, gains, conv_w, conv_b, w_in_bf, b_in, gate_w_bf, gate_b, lam, w_out_bf, b_out,
            w_q_bf, w_o_bf, w_up_bf, w_down_bf):
    B, S, D = x.shape
    M = kv.shape[2]
    ts = min(SEQ_TILE, S)
    n = S // ts
    last = B * n - 1

    def cur(i):
        j = jnp.minimum(i, last)
        return j // n, j % n

    def prev(i):
        j = jnp.maximum(i - 1, 0)
        return j // n, j % n

    weights = [conv_w, conv_b, w_in_bf, b_in, gate_w_bf, gate_b, lam, w_out_bf, b_out,
               w_q_bf, w_o_bf, w_up_bf, w_down_bf]
    return pl.pallas_call(
        functools.partial(_layer0_kernel, tiles_per_seq=n),
        grid=(B * n + 1,),
        in_specs=[
            pl.BlockSpec((1, ts, D), lambda i: (*cur(i), 0)),
            pl.BlockSpec((None, None, M, D), lambda i: (layer, prev(i)[0], 0, 0)),
            pl.BlockSpec((None, None, M, D), lambda i: (layer, prev(i)[0], 0, 1)),
            _const_spec(gains.shape),
        ] + [_const_spec(w.shape, single=True) for w in weights],
        out_specs=pl.BlockSpec((1, ts, D), lambda i: (*prev(i), 0)),
        out_shape=jax.ShapeDtypeStruct((B, S, D), F32),
        scratch_shapes=[
            pltpu.VMEM((SUBLANES, D), F32),
            pltpu.VMEM((ts, D), F32),
            pltpu.VMEM((ts, D), F32),
            pltpu.VMEM((ts, D), F32),
            pltpu.VMEM((1, D), F32),
            pltpu.VMEM((ts, D), F32),
        ],
        compiler_params=_cparams("arbitrary"),
        name="layer0",
    )(x, kv, kv, gains, *weights)


def _head_sum(x, e_ref, exact=True):
    hi = x.astype(BF16)
    lo = (x - hi.astype(F32)).astype(BF16) if exact else None
    e = e_ref[...]
    parts = []
    for q in range(N_QUADS):
        sl = slice(q * QUAD, (q + 1) * QUAD)
        part = jnp.dot(hi[:, sl], e, preferred_element_type=F32)
        if exact:
            part = part + jnp.dot(lo[:, sl], e, preferred_element_type=F32)
        parts.append(part)
    return jnp.concatenate(parts, axis=1)


def _rwkv_pre_kernel(x_ref, g_ref, mu_ref, wrkv_ref, w0_ref, w1_ref, w2_ref, a0_ref, a1_ref, a2_ref,
                     g1_ref, g2_ref, kk_ref, ka_ref, rk_ref, e_ref,
                     r_out, c_out, k_out, v_out, a_out, b_out, g_out, bonus_out, hbuf, *, ts):
    s = pl.program_id(1)

    @pl.when(s == 0)
    def _():
        hbuf[...] = jnp.zeros((SUBLANES, D_MODEL), F32)

    hn = _rms(x_ref[0], g_ref[...])
    xx = (_shift_rows(hn, hbuf[...], 1) - hn).astype(BF16)
    hbuf[...] = hn[ts - SUBLANES:, :]
    hn_bf = hn.astype(BF16)
    mu_bf = mu_ref[...].astype(BF16)

    def mix(i):
        return hn_bf + xx * mu_bf[i:i + 1, :]

    r = _bdot(mix(0), wrkv_ref[0])
    k = _bdot(mix(2), wrkv_ref[1])
    v = _bdot(mix(3), wrkv_ref[2])
    z = w0_ref[...] + _bdot(jnp.tanh(_bdot(mix(1), w1_ref[...])), w2_ref[...])
    lw = DECAY_LOG_SCALE * _sigmoid(z)
    alr = _sigmoid(a0_ref[...] + _bdot(_bdot(mix(4), a1_ref[...]), a2_ref[...]))
    g = _bdot(_sigmoid(_bdot(mix(5), g1_ref[...])), g2_ref[...])

    kk = k * kk_ref[...]
    kk = kk * lax.rsqrt(jnp.maximum(_head_sum(kk * kk, e_ref, exact=False), RWKV_KK_NORM_FLOOR ** 2))
    ka = ka_ref[...]
    k = k * ((1.0 - ka) + alr * ka)
    L = min(SCAN_CHUNK, ts)
    rows = min(QUAD, ts)
    ti = lax.broadcasted_iota(jnp.int32, (rows, rows), 0)
    si = lax.broadcasted_iota(jnp.int32, (rows, rows), 1)
    tri = ((ti // L == si // L) & (ti >= si)).astype(BF16)
    lw_hi = lw.astype(BF16)
    lw_lo = (lw - lw_hi.astype(F32)).astype(BF16)
    r_out[0] = r.astype(BF16)
    c_out[0] = jnp.concatenate(
        [jnp.dot(tri, lw_hi[i:i + rows], preferred_element_type=F32)
         + jnp.dot(tri, lw_lo[i:i + rows], preferred_element_type=F32) for i in range(0, ts, rows)], axis=0)
    k_out[0] = k.astype(BF16)
    v_out[0] = v.astype(BF16)
    a_out[0] = (-kk).astype(BF16)
    b_out[0] = (kk * alr).astype(BF16)
    g_out[0] = g.astype(BF16)
    bonus_out[0] = (_head_sum(r * k * rk_ref[...], e_ref, exact=False) * v).astype(BF16)


def _rwkv_pre(x, g0, mu, w_rkv_bf, w0, w1_bf, w2_bf, a0, a1_bf, a2_bf, g1_bf, g2_bf, k_k, k_a, r_k, e):
    B, S, D = x.shape
    ts = min(PRE_TILE, S)
    tile = pl.BlockSpec((1, ts, D), lambda b, s: (b, s, 0))
    consts = [g0, mu, w_rkv_bf, w0, w1_bf, w2_bf, a0, a1_bf, a2_bf, g1_bf, g2_bf, k_k, k_a, r_k, e]
    return pl.pallas_call(
        functools.partial(_rwkv_pre_kernel, ts=ts),
        grid=(B, S // ts),
        in_specs=[tile] + [_const_spec(c.shape, single=c.size >= D * RWKV_HEAD) for c in consts],
        out_specs=[tile] * 8,
        out_shape=[jax.ShapeDtypeStruct((B, S, D), F32 if i == 1 else BF16) for i in range(8)],
        scratch_shapes=[pltpu.VMEM((SUBLANES, D), F32)],
        compiler_params=_cparams("arbitrary", "arbitrary"),
        name="rwkv_pre",
    )(x, *consts)


_BNN = (((2,), (1,)), ((0,), (0,)))
_BNT = (((2,), (2,)), ((0,), (0,)))
_BTN = (((1,), (1,)), ((0,), (0,)))


def _qdot(a, b, dims=_BNN):
    return lax.dot_general(a.astype(SCAN_DTYPE), b.astype(SCAN_DTYPE), dims, preferred_element_type=F32)


def _rwkv_scan_kernel(r_ref, c_ref, k_ref, v_ref, a_ref, b_ref, y_ref, state, *, L):
    @pl.when(pl.program_id(1) == 0)
    def _():
        state[...] = jnp.zeros(state.shape, F32)

    G = c_ref.shape[0]
    c = c_ref[...]
    first_row = lax.broadcasted_iota(jnp.int32, c.shape, 1) == 0
    c_prev = jnp.where(first_row, 0.0, pltpu.roll(c, shift=1, axis=1))
    c_end = c[:, L - 1:L, :]
    gam = jnp.exp(c)
    gam_inv = jnp.exp(-c)
    gam_end = jnp.exp(c_end - c)
    b_in = b_ref[...].astype(F32)
    k_in = k_ref[...].astype(F32)
    r_t = r_ref[...].astype(F32) * gam
    a_t = a_ref[...].astype(F32) * jnp.exp(c_prev)
    b_t = b_in * gam_inv
    k_t = k_in * gam_inv
    b_h = b_in * gam_end
    k_h = k_in * gam_end
    gam_last = jnp.exp(c_end)

    hq = QUAD // RWKV_HEAD
    t_idx = lax.broadcasted_iota(jnp.int32, (L, hq * L), 0)
    s_idx = lax.broadcasted_iota(jnp.int32, (L, hq * L), 1) % L
    m_strict = t_idx > s_idx
    m_incl = t_idx >= s_idx
    eye = (t_idx == s_idx).astype(F32)
    m_off = []
    m = 1
    while m < L:
        m_off.append(((t_idx // (2 * m)) == (s_idx // (2 * m))) & ((t_idx % (2 * m)) >= m)
                     & ((s_idx % (2 * m)) < m))
        m *= 2
    qrow = lax.broadcasted_iota(jnp.int32, (QUAD, QUAD), 0)
    qcol = lax.broadcasted_iota(jnp.int32, (QUAD, QUAD), 1)
    m_state = (qrow // RWKV_HEAD) == (qcol // RWKV_HEAD)
    lane_head = lax.broadcasted_iota(jnp.int32, (L, QUAD), 1) // RWKV_HEAD
    lane_blk = lax.broadcasted_iota(jnp.int32, (L, hq * L), 1) // L

    def quads(z):
        return jnp.stack([z[g][:, q * QUAD:(q + 1) * QUAD] for g in range(G) for q in range(N_QUADS)], axis=0)

    def bdiag(z, blk):
        z = z.astype(SCAN_DTYPE)
        zero = jnp.zeros_like(z)
        return jnp.concatenate([jnp.where(blk == h, z, zero) for h in range(hq)], axis=1)

    a_q, r_q, v_q, b_q, k_q = quads(a_t), quads(r_t), quads(v_ref[...]), quads(b_t), quads(k_t)
    ar = jnp.concatenate([a_q, r_q], axis=1)
    g_b = _qdot(ar, bdiag(b_q, lane_head), _BNT)
    g_k = _qdot(ar, bdiag(k_q, lane_head), _BNT)
    n_ab = g_b[:, :L]
    a_rb = jnp.where(m_incl, g_b[:, L:], 0.0)
    a_ak = jnp.where(m_strict, g_k[:, :L], 0.0)
    a_rk = jnp.where(m_incl, g_k[:, L:], 0.0)
    tinv = eye + jnp.where(m_off[0], n_ab, 0.0)
    levels = [jnp.where(m_lvl, n_ab, 0.0) for m_lvl in m_off[1:]]
    for start in range(0, len(levels), LEVEL_GROUP):
        group = levels[start:start + LEVEL_GROUP]
        z = _qdot(jnp.concatenate(group, axis=1), bdiag(tinv, lane_blk))
        pend = [z[:, i * L:(i + 1) * L] for i in range(len(group))]
        while pend:
            p_lvl = pend.pop(0)
            z = _qdot(jnp.concatenate([tinv] + pend, axis=1), bdiag(p_lvl, lane_blk))
            tinv = tinv + z[:, :L]
            pend = [q + z[:, (i + 1) * L:(i + 2) * L] for i, q in enumerate(pend)]
    s0 = state[...]
    xs = _qdot(ar, s0, _BNT)
    vs = _qdot(jnp.concatenate([a_ak, a_rk], axis=1), bdiag(v_q, lane_head))
    u = _qdot(tinv, bdiag(xs[:, :L] + vs[:, :L], lane_head))
    y = xs[:, L:] + vs[:, L:] + _qdot(a_rb, bdiag(u, lane_head))
    upd = _qdot(jnp.concatenate([u, v_q], axis=1), jnp.concatenate([quads(b_h), quads(k_h)], axis=1), _BTN)
    state[...] = s0 * quads(gam_last) + jnp.where(m_state, upd, 0.0)
    for g in range(G):
        y_ref[g] = jnp.concatenate([y[g * N_QUADS + q] for q in range(N_QUADS)], axis=1)


def _rwkv_scan(r, c, k, v, a, b):
    B, S, D = r.shape
    L = min(SCAN_CHUNK, S)
    G = SCAN_ROWS if B % SCAN_ROWS == 0 else 1
    tile = pl.BlockSpec((G, L, D), lambda bb, c: (bb, c, 0))
    return pl.pallas_call(
        functools.partial(_rwkv_scan_kernel, L=L),
        grid=(B // G, S // L),
        in_specs=[tile] * 6,
        out_specs=tile,
        out_shape=jax.ShapeDtypeStruct((B, S, D), F32),
        scratch_shapes=[pltpu.VMEM((G * N_QUADS, QUAD, QUAD), F32)],
        compiler_params=_cparams("arbitrary", "arbitrary"),
        name="rwkv_scan",
    )(r, c, k, v, a, b)


def _rwkv_post_block(x, y, bonus, gate, g_ref, gng_ref, gnb_ref, e_ref, wmix_ref):
    mean = _head_sum(y, e_ref)
    d = y - mean
    yield
    var = _head_sum(d * d, e_ref, exact=False)
    yn = d * lax.rsqrt(var + RWKV_GN_EPS) * gng_ref[...] + gnb_ref[...]
    yield
    t = _bdot((yn + bonus) * gate, wmix_ref[...])
    return x + _rms(t, g_ref[...])


_TAIL_ORDER = ("b", "b", "b") + ("a", "b") * 3 + ("b",) * (MLP_PARTS - 3)


def _rwkv_tail_kernel(x_ref, y_ref, bonus_ref, gate_ref, k_ref, v_ref, gains_ref, gng_ref, gnb_ref, e_ref,
                      wmix_ref, wq_ref, wo_ref, wup_ref, wdn_ref, o_ref, mid):
    @pl.when(pl.program_id(0) == 0)
    def _():
        mid[...] = jnp.zeros(mid.shape, F32)

    done = _run_interleaved(_TAIL_ORDER, {
        "a": _rwkv_post_block(x_ref[0], y_ref[0], bonus_ref[0], gate_ref[0], gains_ref.at[1:2], gng_ref,
                              gnb_ref, e_ref, wmix_ref),
        "b": _xattn_mlp_block(mid[...], k_ref, v_ref, gains_ref, wq_ref, wo_ref, wup_ref, wdn_ref),
    })
    o_ref[0] = done["b"]
    mid[...] = done["a"]


def _rwkv_tail(x, y, bonus, gate, kv, layer, gains, gn_g, gn_b, e_mean, w_mix_bf, w_q_bf, w_o_bf, w_up_bf,
               w_down_bf):
    B, S, D = x.shape
    M = kv.shape[2]
    ts = min(TAIL_TILE, S)
    n = S // ts
    last = B * n - 1

    def cur(i):
        j = jnp.minimum(i, last)
        return j // n, j % n

    def prev(i):
        j = jnp.maximum(i - 1, 0)
        return j // n, j % n

    tile_in = pl.BlockSpec((1, ts, D), lambda i: (*cur(i), 0))
    weights = [w_mix_bf, w_q_bf, w_o_bf, w_up_bf, w_down_bf]
    return pl.pallas_call(
        _rwkv_tail_kernel,
        grid=(B * n + 1,),
        in_specs=[tile_in] * 4 + [
            pl.BlockSpec((None, None, M, D), lambda i: (layer, prev(i)[0], 0, 0)),
            pl.BlockSpec((None, None, M, D), lambda i: (layer, prev(i)[0], 0, 1)),
            _const_spec(gains.shape), _const_spec(gn_g.shape), _const_spec(gn_b.shape),
            _const_spec(e_mean.shape),
        ] + [_const_spec(w.shape, single=True) for w in weights],
        out_specs=pl.BlockSpec((1, ts, D), lambda i: (*prev(i), 0)),
        out_shape=jax.ShapeDtypeStruct((B, S, D), F32),
        scratch_shapes=[pltpu.VMEM((ts, D), F32)],
        compiler_params=_cparams("arbitrary"),
        name="rwkv_tail",
    )(x, y, bonus, gate, kv, kv, gains, gn_g, gn_b, e_mean, *weights)


def _head_selector(value):
    head = jnp.arange(QUAD) // RWKV_HEAD
    return jnp.where(head[:, None] == head[None, :], value, 0.0).astype(BF16)


def kernel(x, mem, ln_gains, mem_norm, a_conv_w, a_conv_b, a_w_in, a_b_in, a_gate_w, a_gate_b, a_lambda, a_w_out, a_b_out, b_mu, b_w_rkv, b_w0, b_w1, b_w2, b_a0, b_a1, b_a2, b_g1, b_g2, b_k_k, b_k_a, b_r_k, b_gn_g, b_gn_b, b_w_o, c_w_q, c_w_kv, c_w_o, m_w_up, m_w_down):
    B, S, D = x.shape
    M = mem.shape[1]
    depth = ln_gains.shape[0]
    assert D == D_MODEL and depth == 2
    bf = lambda w: w.astype(BF16)
    row = lambda p: p.reshape(1, -1)

    kv = _mem_kv(mem.reshape(B * M, D), row(mem_norm), bf(c_w_kv)).reshape(depth, B, M, 2 * D)
    w_q_scaled = bf(c_w_q * (MEM_HEAD_DIM ** -0.5))
    e = _head_selector(1.0)
    e_mean = _head_selector(1.0 / RWKV_HEAD)

    for i in range(depth):
        g = ln_gains[i]
        j = i // 2
        if i % 2 == 0:
            x = _layer0(x, kv, i, g, a_conv_w[j], row(a_conv_b[j]), bf(a_w_in[j]), row(a_b_in[j]),
                        bf(a_gate_w[j]), a_gate_b[j].reshape(2, D), row(a_lambda[j]), bf(a_w_out[j]),
                        row(a_b_out[j]), w_q_scaled[i], bf(c_w_o[i]), bf(m_w_up[i]), bf(m_w_down[i]))
        else:
            r, c, k, v, a, b, gate, bonus = _rwkv_pre(
                x, g[0:1], b_mu[j], bf(b_w_rkv[j]), row(b_w0[j]), bf(b_w1[j]), bf(b_w2[j]), row(b_a0[j]),
                bf(b_a1[j]), bf(b_a2[j]), bf(b_g1[j]), bf(b_g2[j]), row(b_k_k[j]), row(b_k_a[j]),
                row(b_r_k[j]), e)
            y = _rwkv_scan(r, c, k, v, a, b)
            x = _rwkv_tail(x, y, bonus, gate, kv, i, g, row(b_gn_g[j]), row(b_gn_b[j]), e_mean, bf(b_w_o[j]),
                           w_q_scaled[i], bf(c_w_o[i]), bf(m_w_up[i]), bf(m_w_down[i]))
    return x
```

```python
import functools

import jax
import jax.numpy as jnp
from jax import lax
from jax.experimental import pallas as pl
from jax.experimental.pallas import tpu as pltpu

F32 = jnp.float32
BF16 = jnp.bfloat16

D_MODEL = 1024
RMS_EPS = 1e-6
LRU_HEADS = 4
LRU_BLOCK = D_MODEL // LRU_HEADS
CONV_WIDTH = 4
LRU_C = 8.0
RWKV_HEAD = 64
RWKV_KK_NORM_FLOOR = 1e-12
RWKV_GN_EPS = 64e-5
DECAY_LOG_SCALE = -0.6065306597126334
GELU_C = 0.7978845608028654
MEM_HEADS = 4
MEM_HEAD_DIM = D_MODEL // MEM_HEADS
D_FF = 4 * D_MODEL

SUBLANES = 8
QUAD = 256
N_QUADS = D_MODEL // QUAD
SEQ_TILE = 512
MEMKV_TILE = 1024
PRE_TILE = 512
TAIL_TILE = 512
MLP_PARTS = 8
SCAN_CHUNK = 64
LEVEL_GROUP = 3
SCAN_ROWS = 8
VMEM_LIMIT = 56 * 1024 * 1024

SCAN_DTYPE = BF16


def _cparams(*sem):
    return pltpu.CompilerParams(dimension_semantics=sem, vmem_limit_bytes=VMEM_LIMIT)


def _rms(x, g):
    return x * lax.rsqrt(jnp.mean(x * x, axis=-1, keepdims=True) + RMS_EPS) * g


def _sigmoid(x):
    return 1.0 / (1.0 + jnp.exp(-x))


def _softplus(x):
    return jnp.maximum(x, 0.0) + jnp.log(1.0 + jnp.exp(-jnp.abs(x)))


def _gelu_tanh(x):
    half = 0.5 * x
    return half + half * jnp.tanh(x * (GELU_C + (GELU_C * 0.044715) * (x * x)))


def _bdot(a, b):
    return jnp.dot(a.astype(BF16), b.astype(BF16), preferred_element_type=F32)


def _shift_rows(x, tail, k):
    rolled = pltpu.roll(x, shift=k, axis=0)
    row = lax.broadcasted_iota(jnp.int32, (SUBLANES, x.shape[1]), 0)
    head = jnp.where(row < k, pltpu.roll(tail, shift=k, axis=0), rolled[:SUBLANES])
    return jnp.concatenate([head, rolled[SUBLANES:]], axis=0)


def _const_spec(shape, single=False):
    nd = len(shape)
    if single:
        return pl.BlockSpec(shape, lambda *_: (0,) * nd, pipeline_mode=pl.Buffered(1))
    return pl.BlockSpec(shape, lambda *_: (0,) * nd)


def _layer_spec(shape, layer):
    nd = len(shape) - 1
    return pl.BlockSpec((None,) + tuple(shape[1:]), lambda *_: (layer,) + (0,) * nd,
                        pipeline_mode=pl.Buffered(1))


def _memkv_kernel(mem_ref, gn_ref, wkv_ref, out_ref):
    mn = _rms(mem_ref[...], gn_ref[...])
    out_ref[...] = _bdot(mn, wkv_ref[...]).astype(BF16)


def _mem_kv(mem2d, mem_norm, w_kv_bf):
    rows = mem2d.shape[0]
    depth = w_kv_bf.shape[0]
    tm = min(MEMKV_TILE, rows)
    return pl.pallas_call(
        _memkv_kernel,
        grid=(depth, rows // tm),
        in_specs=[
            pl.BlockSpec((tm, D_MODEL), lambda i, r: (r, 0)),
            _const_spec((1, D_MODEL)),
            pl.BlockSpec((None, D_MODEL, 2 * D_MODEL), lambda i, r: (i, 0, 0)),
        ],
        out_specs=pl.BlockSpec((None, tm, 2 * D_MODEL), lambda i, r: (i, r, 0)),
        out_shape=jax.ShapeDtypeStruct((depth, rows, 2 * D_MODEL), BF16),
        compiler_params=_cparams("arbitrary", "arbitrary"),
        name="mem_kv",
    )(mem2d, mem_norm, w_kv_bf)


def _run_interleaved(order, gens):
    done = {}
    for name in order:
        try:
            next(gens[name])
        except StopIteration as stop:
            done[name] = stop.value
    return done


def _rglru_block(x, g_ref, convw_ref, convb_ref, win_ref, bin_ref, gw_ref, gb_ref, lam_ref,
                 wout_ref, bout_ref, ubuf, a_s, b_s, h_s, hcar):
    ts = x.shape[0]
    hn = _rms(x, g_ref[0:1, :])
    proj = _bdot(hn, win_ref[...]) + bin_ref[...]
    yield
    yb = _gelu_tanh(proj[:, :D_MODEL])
    u = proj[:, D_MODEL:]
    tail = ubuf[...]
    conv = convb_ref[...] + u * convw_ref[CONV_WIDTH - 1:CONV_WIDTH, :]
    for tap in range(CONV_WIDTH - 1):
        conv = conv + _shift_rows(u, tail, CONV_WIDTH - 1 - tap) * convw_ref[tap:tap + 1, :]
    ubuf[...] = u[ts - SUBLANES:, :]
    yield
    conv_bf = conv.astype(BF16)
    r_parts, i_parts = [], []
    for h in range(LRU_HEADS):
        cb = conv_bf[:, h * LRU_BLOCK:(h + 1) * LRU_BLOCK]
        r_parts.append(jnp.dot(cb, gw_ref[0, h], preferred_element_type=F32))
        i_parts.append(jnp.dot(cb, gw_ref[1, h], preferred_element_type=F32))
    r_gate = _sigmoid(jnp.concatenate(r_parts, axis=1) + gb_ref[0:1, :])
    i_gate = _sigmoid(jnp.concatenate(i_parts, axis=1) + gb_ref[1:2, :])
    a = jnp.exp(r_gate * ((-LRU_C) * _softplus(-lam_ref[...])))
    a_s[...] = a
    b_s[...] = jnp.sqrt(1.0 - a * a) * i_gate * conv
    yield

    def body(t, h):
        h = a_s[pl.ds(t, 1), :] * h + b_s[pl.ds(t, 1), :]
        h_s[pl.ds(t, 1), :] = h
        return h

    hcar[...] = lax.fori_loop(0, ts, body, hcar[...], unroll=True)
    yield
    out =_bdot(h_s[...] * yb, wout_ref[...]) + bout_ref[...]
    return x + _rms(out, g_ref[1:2, :])


def _xattn_block(x, k_ref, v_ref, g_ref, wq_ref, wo_ref):
    hn = _rms(x, g_ref[0:1, :])
    q = _bdot(hn, wq_ref[...]).astype(BF16)
    yield
    outs = []
    for h in range(MEM_HEADS):
        sl = slice(h * MEM_HEAD_DIM, (h + 1) * MEM_HEAD_DIM)
        sc = lax.dot_general(q[:, sl], k_ref[:, sl], (((1,), (1,)), ((), ())),
                             preferred_element_type=F32)
        p = jnp.exp(sc - jnp.max(sc, axis=-1, keepdims=True))
        l = jnp.sum(p, axis=-1, keepdims=True)
        outs.append(jnp.dot(p.astype(BF16), v_ref[:, sl], preferred_element_type=F32) / l)
    yield
    c = _bdot(jnp.concatenate(outs, axis=1), wo_ref[...])
    return x + _rms(c, g_ref[1:2, :])


def _mlp_block(x, g_ref, wup_ref, wdn_ref):
    hn = _rms(x, g_ref[0:1, :]).astype(BF16)
    acc = None
    for c in range(MLP_PARTS):
        sl = slice(c * (D_FF // MLP_PARTS), (c + 1) * (D_FF // MLP_PARTS))
        h = jnp.maximum(jnp.dot(hn, wup_ref[:, sl], preferred_element_type=F32), 0.0)
        part = jnp.dot((h * h).astype(BF16), wdn_ref[sl, :], preferred_element_type=F32)
        acc = part if acc is None else acc + part
        if c + 1 < MLP_PARTS:
            yield
    return x + _rms(acc, g_ref[1:2, :])


def _xattn_mlp_block(x, k_ref, v_ref, gains_ref, wq_ref, wo_ref, wup_ref, wdn_ref):
    x1 = yield from _xattn_block(x, k_ref, v_ref, gains_ref.at[2:4], wq_ref, wo_ref)
    yield
    return (yield from _mlp_block(x1, gains_ref.at[4:6], wup_ref, wdn_ref))


_LAYER0_ORDER = ("b", "a", "b", "b") + ("a", "b") * 4 + ("b",) * (MLP_PARTS - 4)


def _layer0_kernel(x_ref, k_ref, v_ref, gains_ref, convw_ref, convb_ref, win_ref, bin_ref, gw_ref, gb_ref,
                   lam_ref, wout_ref, bout_ref, wq_ref, wo_ref, wup_ref, wdn_ref, o_ref,
                   ubuf, a_s, b_s, h_s, hcar, mid, *, tiles_per_seq):
    i = pl.program_id(0)

    @pl.when(i == 0)
    def _():
        mid[...] = jnp.zeros(mid.shape, F32)

    @pl.when(i % tiles_per_seq == 0)
    def _():
        ubuf[...] = jnp.zeros(ubuf.shape, F32)
        hcar[...] = jnp.zeros(hcar.shape, F32)

    done = _run_interleaved(_LAYER0_ORDER, {
        "a": _rglru_block(x_ref[0], gains_ref.at[0:2], convw_ref, convb_ref, win_ref, bin_ref, gw_ref,
                          gb_ref, lam_ref, wout_ref, bout_ref, ubuf, a_s, b_s, h_s, hcar),
        "b": _xattn_mlp_block(mid[...], k_ref, v_ref, gains_ref, wq_ref, wo_ref, wup_ref, wdn_ref),
    })
    o_ref[0] = done["b"]
    mid[...] = done["a"]


def _layer0(x, kv, layer, gains, conv_w, conv_b, w_in_bf, b_in, gate_w_bf, gate_b, lam, w_out_bf, b_out,
            stacked):
    B, S, D = x.shape
    M = kv.shape[2]
    ts = min(SEQ_TILE, S)
    n = S // ts
    last = B * n - 1

    def cur(i):
        j = jnp.minimum(i, last)
        return j // n, j % n

    def prev(i):
        j = jnp.maximum(i - 1, 0)
        return j // n, j % n

    weights = [conv_w, conv_b, w_in_bf, b_in, gate_w_bf, gate_b, lam, w_out_bf, b_out]
    return pl.pallas_call(
        functools.partial(_layer0_kernel, tiles_per_seq=n),
        grid=(B * n + 1,),
        in_specs=[
            pl.BlockSpec((1, ts, D), lambda i: (*cur(i), 0)),
            pl.BlockSpec((None, None, M, D), lambda i: (layer, prev(i)[0], 0, 0)),
            pl.BlockSpec((None, None, M, D), lambda i: (layer, prev(i)[0], 0, 1)),
            _const_spec(gains.shape),
        ] + [_const_spec(w.shape, single=True) for w in weights] + [_layer_spec(w.shape, layer) for w in stacked],
        out_specs=pl.BlockSpec((1, ts, D), lambda i: (*prev(i), 0)),
        out_shape=jax.ShapeDtypeStruct((B, S, D), F32),
        scratch_shapes=[
            pltpu.VMEM((SUBLANES, D), F32),
            pltpu.VMEM((ts, D), F32),
            pltpu.VMEM((ts, D), F32),
            pltpu.VMEM((ts, D), F32),
            pltpu.VMEM((1, D), F32),
            pltpu.VMEM((ts, D), F32),
        ],
        compiler_params=_cparams("arbitrary"),
        name="layer0",
    )(x, kv, kv, gains, *weights, *stacked)


def _head_sum(x, e_ref, exact=True):
    hi = x.astype(BF16)
    lo = (x - hi.astype(F32)).astype(BF16) if exact else None
    e = e_ref[...]
    parts = []
    for q in range(N_QUADS):
        sl = slice(q * QUAD, (q + 1) * QUAD)
        part = jnp.dot(hi[:, sl], e, preferred_element_type=F32)
        if exact:
            part = part + jnp.dot(lo[:, sl], e, preferred_element_type=F32)
        parts.append(part)
    return jnp.concatenate(parts, axis=1)


def _rwkv_pre_kernel(x_ref, g_ref, mu_ref, wrkv_ref, w0_ref, w1_ref, w2_ref, a0_ref, a1_ref, a2_ref,
                     g1_ref, g2_ref, kk_ref, ka_ref, rk_ref, e_ref,
                     r_out, c_out, k_out, v_out, a_out, b_out, g_out, bonus_out, hbuf, *, ts):
    s = pl.program_id(1)

    @pl.when(s == 0)
    def _():
        hbuf[...] = jnp.zeros((SUBLANES, D_MODEL), F32)

    hn = _rms(x_ref[0], g_ref[...])
    xx = (_shift_rows(hn, hbuf[...], 1) - hn).astype(BF16)
    hbuf[...] = hn[ts - SUBLANES:, :]
    hn_bf = hn.astype(BF16)
    mu_bf = mu_ref[...].astype(BF16)

    def mix(i):
        return hn_bf + xx * mu_bf[i:i + 1, :]

    r = _bdot(mix(0), wrkv_ref[0])
    k = _bdot(mix(2), wrkv_ref[1])
    v = _bdot(mix(3), wrkv_ref[2])
    z = w0_ref[...] + _bdot(jnp.tanh(_bdot(mix(1), w1_ref[...])), w2_ref[...])
    lw = DECAY_LOG_SCALE * _sigmoid(z)
    alr = _sigmoid(a0_ref[...] + _bdot(_bdot(mix(4), a1_ref[...]), a2_ref[...]))
    g = _bdot(_sigmoid(_bdot(mix(5), g1_ref[...])), g2_ref[...])

    kk = k * kk_ref[...]
    kk = kk * lax.rsqrt(jnp.maximum(_head_sum(kk * kk, e_ref, exact=False), RWKV_KK_NORM_FLOOR ** 2))
    ka = ka_ref[...]
    k = k * ((1.0 - ka) + alr * ka)
    L = min(SCAN_CHUNK, ts)
    rows = min(QUAD, ts)
    ti = lax.broadcasted_iota(jnp.int32, (rows, rows), 0)
    si = lax.broadcasted_iota(jnp.int32, (rows, rows), 1)
    tri = ((ti // L == si // L) & (ti >= si)).astype(BF16)
    lw_hi = lw.astype(BF16)
    lw_lo = (lw - lw_hi.astype(F32)).astype(BF16)
    r_out[0] = r.astype(BF16)
    c_out[0] = jnp.concatenate(
        [jnp.dot(tri, lw_hi[i:i + rows], preferred_element_type=F32)
         + jnp.dot(tri, lw_lo[i:i + rows], preferred_element_type=F32) for i in range(0, ts, rows)], axis=0)
    k_out[0] = k.astype(BF16)
    v_out[0] = v.astype(BF16)
    a_out[0] = (-kk).astype(BF16)
    b_out[0] = (kk * alr).astype(BF16)
    g_out[0] = g.astype(BF16)
    bonus_out[0] = (_head_sum(r * k * rk_ref[...], e_ref, exact=False) * v).astype(BF16)


def _rwkv_pre(x, g0, mu, w_rkv_bf, w0, w1_bf, w2_bf, a0, a1_bf, a2_bf, g1_bf, g2_bf, k_k, k_a, r_k, e):
    B, S, D = x.shape
    ts = min(PRE_TILE, S)
    tile = pl.BlockSpec((1, ts, D), lambda b, s: (b, s, 0))
    consts = [g0, mu, w_rkv_bf, w0, w1_bf, w2_bf, a0, a1_bf, a2_bf, g1_bf, g2_bf, k_k, k_a, r_k, e]
    return pl.pallas_call(
        functools.partial(_rwkv_pre_kernel, ts=ts),
        grid=(B, S // ts),
        in_specs=[tile] + [_const_spec(c.shape, single=c.size >= D * RWKV_HEAD) for c in consts],
        out_specs=[tile] * 8,
        out_shape=[jax.ShapeDtypeStruct((B, S, D), F32 if i == 1 else BF16) for i in range(8)],
        scratch_shapes=[pltpu.VMEM((SUBLANES, D), F32)],
        compiler_params=_cparams("arbitrary", "arbitrary"),
        name="rwkv_pre",
    )(x, *consts)


_BNN = (((2,), (1,)), ((0,), (0,)))
_BNT = (((2,), (2,)), ((0,), (0,)))
_BTN = (((1,), (1,)), ((0,), (0,)))


def _qdot(a, b, dims=_BNN):
    return lax.dot_general(a.astype(SCAN_DTYPE), b.astype(SCAN_DTYPE), dims, preferred_element_type=F32)


def _rwkv_scan_kernel(r_ref, c_ref, k_ref, v_ref, a_ref, b_ref, y_ref, state, *, L):
    @pl.when(pl.program_id(1) == 0)
    def _():
        state[...] = jnp.zeros(state.shape, F32)

    G = c_ref.shape[0]
    c = c_ref[...]
    first_row = lax.broadcasted_iota(jnp.int32, c.shape, 1) == 0
    c_prev = jnp.where(first_row, 0.0, pltpu.roll(c, shift=1, axis=1))
    c_end = c[:, L - 1:L, :]
    gam = jnp.exp(c)
    gam_inv = jnp.exp(-c)
    gam_end = jnp.exp(c_end - c)
    b_in = b_ref[...].astype(F32)
    k_in = k_ref[...].astype(F32)
    r_t = r_ref[...].astype(F32) * gam
    a_t = a_ref[...].astype(F32) * jnp.exp(c_prev)
    b_t = b_in * gam_inv
    k_t = k_in * gam_inv
    b_h = b_in * gam_end
    k_h = k_in * gam_end
    gam_last = jnp.exp(c_end)

    hq = QUAD // RWKV_HEAD
    t_idx = lax.broadcasted_iota(jnp.int32, (L, hq * L), 0)
    s_idx = lax.broadcasted_iota(jnp.int32, (L, hq * L), 1) % L
    m_strict = t_idx > s_idx
    m_incl = t_idx >= s_idx
    eye = (t_idx == s_idx).astype(F32)
    m_off = []
    m = 1
    while m < L:
        m_off.append(((t_idx // (2 * m)) == (s_idx // (2 * m))) & ((t_idx % (2 * m)) >= m)
                     & ((s_idx % (2 * m)) < m))
        m *= 2
    qrow = lax.broadcasted_iota(jnp.int32, (QUAD, QUAD), 0)
    qcol = lax.broadcasted_iota(jnp.int32, (QUAD, QUAD), 1)
    m_state = (qrow // RWKV_HEAD) == (qcol // RWKV_HEAD)
    lane_head = lax.broadcasted_iota(jnp.int32, (L, QUAD), 1) // RWKV_HEAD
    lane_blk = lax.broadcasted_iota(jnp.int32, (L, hq * L), 1) // L

    def quads(z):
        return jnp.stack([z[g][:, q * QUAD:(q + 1) * QUAD] for g in range(G) for q in range(N_QUADS)], axis=0)

    def bdiag(z, blk):
        z = z.astype(SCAN_DTYPE)
        zero = jnp.zeros_like(z)
        return jnp.concatenate([jnp.where(blk == h, z, zero) for h in range(hq)], axis=1)

    a_q, r_q, v_q, b_q, k_q = quads(a_t), quads(r_t), quads(v_ref[...]), quads(b_t), quads(k_t)
    ar = jnp.concatenate([a_q, r_q], axis=1)
    g_b = _qdot(ar, bdiag(b_q, lane_head), _BNT)
    g_k = _qdot(ar, bdiag(k_q, lane_head), _BNT)
    n_ab = g_b[:, :L]
    a_rb = jnp.where(m_incl, g_b[:, L:], 0.0)
    a_ak = jnp.where(m_strict, g_k[:, :L], 0.0)
    a_rk = jnp.where(m_incl, g_k[:, L:], 0.0)
    tinv = eye + jnp.where(m_off[0], n_ab, 0.0)
    levels = [jnp.where(m_lvl, n_ab, 0.0) for m_lvl in m_off[1:]]
    for start in range(0, len(levels), LEVEL_GROUP):
        group = levels[start:start + LEVEL_GROUP]
        z = _qdot(jnp.concatenate(group, axis=1), bdiag(tinv, lane_blk))
        pend = [z[:, i * L:(i + 1) * L] for i in range(len(group))]
        while pend:
            p_lvl = pend.pop(0)
            z = _qdot(jnp.concatenate([tinv] + pend, axis=1), bdiag(p_lvl, lane_blk))
            tinv = tinv + z[:, :L]
            pend = [q + z[:, (i + 1) * L:(i + 2) * L] for i, q in enumerate(pend)]
    s0 = state[...]
    xs = _qdot(ar, s0, _BNT)
    vs = _qdot(jnp.concatenate([a_ak, a_rk], axis=1), bdiag(v_q, lane_head))
    u = _qdot(tinv, bdiag(xs[:, :L] + vs[:, :L], lane_head))
    y = xs[:, L:] + vs[:, L:] + _qdot(a_rb, bdiag(u, lane_head))
    upd = _qdot(jnp.concatenate([u, v_q], axis=1), jnp.concatenate([quads(b_h), quads(k_h)], axis=1), _BTN)
    state[...] = s0 * quads(gam_last) + jnp.where(m_state, upd, 0.0)
    for g in range(G):
        y_ref[g] = jnp.concatenate([y[g * N_QUADS + q] for q in range(N_QUADS)], axis=1)


def _rwkv_scan(r, c, k, v, a, b):
    B, S, D = r.shape
    L = min(SCAN_CHUNK, S)
    G = SCAN_ROWS if B % SCAN_ROWS == 0 else 1
    tile = pl.BlockSpec((G, L, D), lambda bb, c: (bb, c, 0))
    return pl.pallas_call(
        functools.partial(_rwkv_scan_kernel, L=L),
        grid=(B // G, S // L),
        in_specs=[tile] * 6,
        out_specs=tile,
        out_shape=jax.ShapeDtypeStruct((B, S, D), F32),
        scratch_shapes=[pltpu.VMEM((G * N_QUADS, QUAD, QUAD), F32)],
        compiler_params=_cparams("arbitrary", "arbitrary"),
        name="rwkv_scan",
    )(r, c, k, v, a, b)


def _rwkv_post_block(x, y, bonus, gate, g_ref, gng_ref, gnb_ref, e_ref, wmix_ref):
    mean = _head_sum(y, e_ref)
    d = y - mean
    yield
    var = _head_sum(d * d, e_ref, exact=False)
    yn = d * lax.rsqrt(var + RWKV_GN_EPS) * gng_ref[...] + gnb_ref[...]
    yield
    t = _bdot((yn + bonus) * gate, wmix_ref[...])
    return x + _rms(t, g_ref[...])


_TAIL_ORDER = ("b", "b", "b") + ("a", "b") * 3 + ("b",) * (MLP_PARTS - 3)


def _rwkv_tail_kernel(x_ref, y_ref, bonus_ref, gate_ref, k_ref, v_ref, gains_ref, gng_ref, gnb_ref, e_ref,
                      wmix_ref, wq_ref, wo_ref, wup_ref, wdn_ref, o_ref, mid):
    @pl.when(pl.program_id(0) == 0)
    def _():
        mid[...] = jnp.zeros(mid.shape, F32)

    done = _run_interleaved(_TAIL_ORDER, {
        "a": _rwkv_post_block(x_ref[0], y_ref[0], bonus_ref[0], gate_ref[0], gains_ref.at[1:2], gng_ref,
                              gnb_ref, e_ref, wmix_ref),
        "b": _xattn_mlp_block(mid[...], k_ref, v_ref, gains_ref, wq_ref, wo_ref, wup_ref, wdn_ref),
    })
    o_ref[0] = done["b"]
    mid[...] = done["a"]


def _rwkv_tail(x, y, bonus, gate, kv, layer, gains, gn_g, gn_b, e_mean, w_mix_bf, stacked):
    B, S, D = x.shape
    M = kv.shape[2]
    ts = min(TAIL_TILE, S)
    n = S // ts
    last = B * n - 1

    def cur(i):
        j = jnp.minimum(i, last)
        return j // n, j % n

    def prev(i):
        j = jnp.maximum(i - 1, 0)
        return j // n, j % n

    tile_in = pl.BlockSpec((1, ts, D), lambda i: (*cur(i), 0))
    return pl.pallas_call(
        _rwkv_tail_kernel,
        grid=(B * n + 1,),
        in_specs=[tile_in] * 4 + [
            pl.BlockSpec((None, None, M, D), lambda i: (layer, prev(i)[0], 0, 0)),
            pl.BlockSpec((None, None, M, D), lambda i: (layer, prev(i)[0], 0, 1)),
            _const_spec(gains.shape), _const_spec(gn_g.shape), _const_spec(gn_b.shape),
            _const_spec(e_mean.shape), _const_spec(w_mix_bf.shape, single=True),
        ] + [_layer_spec(w.shape, layer) for w in stacked],
        out_specs=pl.BlockSpec((1, ts, D), lambda i: (*prev(i), 0)),
        out_shape=jax.ShapeDtypeStruct((B, S, D), F32),
        scratch_shapes=[pltpu.VMEM((ts, D), F32)],
        compiler_params=_cparams("arbitrary"),
        name="rwkv_tail",
    )(x, y, bonus, gate, kv, kv, gains, gn_g, gn_b, e_mean, w_mix_bf, *stacked)


def _head_selector(value):
    head = jnp.arange(QUAD) // RWKV_HEAD
    return jnp.where(head[:, None] == head[None, :], value, 0.0).astype(BF16)


def kernel(x, mem, ln_gains, mem_norm, a_conv_w, a_conv_b, a_w_in, a_b_in, a_gate_w, a_gate_b, a_lambda, a_w_out, a_b_out, b_mu, b_w_rkv, b_w0, b_w1, b_w2, b_a0, b_a1, b_a2, b_g1, b_g2, b_k_k, b_k_a, b_r_k, b_gn_g, b_gn_b, b_w_o, c_w_q, c_w_kv, c_w_o, m_w_up, m_w_down):
    B, S, D = x.shape
    M = mem.shape[1]
    depth = ln_gains.shape[0]
    assert D == D_MODEL and depth == 2
    bf = lambda w: w.astype(BF16)
    row = lambda p: p.reshape(1, -1)

    kv = _mem_kv(mem.reshape(B * M, D), row(mem_norm), bf(c_w_kv)).reshape(depth, B, M, 2 * D)
    stacked = [bf(c_w_q * (MEM_HEAD_DIM ** -0.5)), bf(c_w_o), bf(m_w_up), bf(m_w_down)]
    e = _head_selector(1.0)
    e_mean = _head_selector(1.0 / RWKV_HEAD)

    for i in range(depth):
        g = ln_gains[i]
        j = i // 2
        if i % 2 == 0:
            x = _layer0(x, kv, i, g, a_conv_w[j], row(a_conv_b[j]), bf(a_w_in[j]), row(a_b_in[j]),
                        bf(a_gate_w[j]), a_gate_b[j].reshape(2, D), row(a_lambda[j]), bf(a_w_out[j]),
                        row(a_b_out[j]), stacked)
        else:
            r, c, k, v, a, b, gate, bonus = _rwkv_pre(
                x, g[0:1], b_mu[j], bf(b_w_rkv[j]), row(b_w0[j]), bf(b_w1[j]), bf(b_w2[j]), row(b_a0[j]),
                bf(b_a1[j]), bf(b_a2[j]), bf(b_g1[j]), bf(b_g2[j]), row(b_k_k[j]), row(b_k_a[j]),
                row(b_r_k[j]), e)
            y = _rwkv_scan(r, c, k, v, a, b)
            x = _rwkv_tail(x, y, bonus, gate, kv, i, g, row(b_gn_g[j]), row(b_gn_b[j]), e_mean, bf(b_w_o[j]),
                           stacked)
    return x
```

```python
import functools

import jax
import jax.numpy as jnp
from jax import lax
from jax.experimental import pallas as pl
from jax.experimental.pallas import tpu as pltpu

F32 = jnp.float32
BF16 = jnp.bfloat16

D_MODEL = 1024
RMS_EPS = 1e-6
LRU_HEADS = 4
LRU_BLOCK = D_MODEL // LRU_HEADS
CONV_WIDTH = 4
LRU_C = 8.0
RWKV_HEAD = 64
RWKV_KK_NORM_FLOOR = 1e-12
RWKV_GN_EPS = 64e-5
DECAY_LOG_SCALE = -0.6065306597126334
GELU_C = 0.7978845608028654
MEM_HEADS = 4
MEM_HEAD_DIM = D_MODEL // MEM_HEADS
D_FF = 4 * D_MODEL

SUBLANES = 8
QUAD = 256
N_QUADS = D_MODEL // QUAD
SEQ_TILE = 512
MEMKV_TILE = 1024
PRE_TILE = 512
TAIL_TILE = 512
MLP_PARTS = 8
SCAN_CHUNK = 64
LEVEL_GROUP = 3
SCAN_ROWS = 8
VMEM_LIMIT = 56 * 1024 * 1024

SCAN_DTYPE = BF16


def _cparams(*sem, fuse_inputs=None):
    return pltpu.CompilerParams(dimension_semantics=sem, vmem_limit_bytes=VMEM_LIMIT,
                                allow_input_fusion=fuse_inputs)


def _rms(x, g):
    return x * lax.rsqrt(jnp.mean(x * x, axis=-1, keepdims=True) + RMS_EPS) * g


def _sigmoid(x):
    return 1.0 / (1.0 + jnp.exp(-x))


def _softplus(x):
    return jnp.maximum(x, 0.0) + jnp.log(1.0 + jnp.exp(-jnp.abs(x)))


def _gelu_tanh(x):
    half = 0.5 * x
    return half + half * jnp.tanh(x * (GELU_C + (GELU_C * 0.044715) * (x * x)))


def _bdot(a, b):
    return jnp.dot(a.astype(BF16), b.astype(BF16), preferred_element_type=F32)


def _shift_rows(x, tail, k):
    rolled = pltpu.roll(x, shift=k, axis=0)
    row = lax.broadcasted_iota(jnp.int32, (SUBLANES, x.shape[1]), 0)
    head = jnp.where(row < k, pltpu.roll(tail, shift=k, axis=0), rolled[:SUBLANES])
    return jnp.concatenate([head, rolled[SUBLANES:]], axis=0)


def _const_spec(shape, single=False):
    nd = len(shape)
    if single:
        return pl.BlockSpec(shape, lambda *_: (0,) * nd, pipeline_mode=pl.Buffered(1))
    return pl.BlockSpec(shape, lambda *_: (0,) * nd)


def _layer_spec(shape, layer):
    nd = len(shape) - 1
    return pl.BlockSpec((None,) + tuple(shape[1:]), lambda *_: (layer,) + (0,) * nd,
                        pipeline_mode=pl.Buffered(1))


def _memkv_kernel(mem_ref, gn_ref, wkv_ref, out_ref):
    mn = _rms(mem_ref[...], gn_ref[...])
    out_ref[...] = _bdot(mn, wkv_ref[...]).astype(BF16)


def _mem_kv(mem2d, mem_norm, w_kv_bf):
    rows = mem2d.shape[0]
    depth = w_kv_bf.shape[0]
    tm = min(MEMKV_TILE, rows)
    return pl.pallas_call(
        _memkv_kernel,
        grid=(depth, rows // tm),
        in_specs=[
            pl.BlockSpec((tm, D_MODEL), lambda i, r: (r, 0)),
            _const_spec((1, D_MODEL)),
            pl.BlockSpec((None, D_MODEL, 2 * D_MODEL), lambda i, r: (i, 0, 0)),
        ],
        out_specs=pl.BlockSpec((None, tm, 2 * D_MODEL), lambda i, r: (i, r, 0)),
        out_shape=jax.ShapeDtypeStruct((depth, rows, 2 * D_MODEL), BF16),
        compiler_params=_cparams("arbitrary", "arbitrary"),
        name="mem_kv",
    )(mem2d, mem_norm, w_kv_bf)


def _run_interleaved(order, gens):
    done = {}
    for name in order:
        try:
            next(gens[name])
        except StopIteration as stop:
            done[name] = stop.value
    return done


def _rglru_block(x, g_ref, convw_ref, convb_ref, win_ref, bin_ref, gw_ref, gb_ref, lam_ref,
                 wout_ref, bout_ref, ubuf, a_s, b_s, h_s, hcar):
    ts = x.shape[0]
    hn = _rms(x, g_ref[0:1, :])
    proj = _bdot(hn, win_ref[...]) + bin_ref[...]
    yield
    yb = _gelu_tanh(proj[:, :D_MODEL])
    u = proj[:, D_MODEL:]
    tail = ubuf[...]
    conv = convb_ref[...] + u * convw_ref[CONV_WIDTH - 1:CONV_WIDTH, :]
    for tap in range(CONV_WIDTH - 1):
        conv = conv + _shift_rows(u, tail, CONV_WIDTH - 1 - tap) * convw_ref[tap:tap + 1, :]
    ubuf[...] = u[ts - SUBLANES:, :]
    yield
    conv_bf = conv.astype(BF16)
    r_parts, i_parts = [], []
    for h in range(LRU_HEADS):
        cb = conv_bf[:, h * LRU_BLOCK:(h + 1) * LRU_BLOCK]
        r_parts.append(jnp.dot(cb, gw_ref[0, h], preferred_element_type=F32))
        i_parts.append(jnp.dot(cb, gw_ref[1, h], preferred_element_type=F32))
    r_gate = _sigmoid(jnp.concatenate(r_parts, axis=1) + gb_ref[0:1, :])
    i_gate = _sigmoid(jnp.concatenate(i_parts, axis=1) + gb_ref[1:2, :])
    a = jnp.exp(r_gate * ((-LRU_C) * _softplus(-lam_ref[...])))
    a_s[...] = a
    b_s[...] = jnp.sqrt(1.0 - a * a) * i_gate * conv
    yield

    def body(t, h):
        h = a_s[pl.ds(t, 1), :] * h + b_s[pl.ds(t, 1), :]
        h_s[pl.ds(t, 1), :] = h
        return h

    hcar[...] = lax.fori_loop(0, ts, body, hcar[...], unroll=True)
    yield
    out =_bdot(h_s[...] * yb, wout_ref[...]) + bout_ref[...]
    return x + _rms(out, g_ref[1:2, :])


def _xattn_block(x, k_ref, v_ref, g_ref, wq_ref, wo_ref):
    hn = _rms(x, g_ref[0:1, :])
    q = _bdot(hn, wq_ref[...]).astype(BF16)
    yield
    outs = []
    for h in range(MEM_HEADS):
        sl = slice(h * MEM_HEAD_DIM, (h + 1) * MEM_HEAD_DIM)
        sc = lax.dot_general(q[:, sl], k_ref[:, sl], (((1,), (1,)), ((), ())),
                             preferred_element_type=F32)
        p = jnp.exp(sc - jnp.max(sc, axis=-1, keepdims=True))
        l = jnp.sum(p, axis=-1, keepdims=True)
        outs.append(jnp.dot(p.astype(BF16), v_ref[:, sl], preferred_element_type=F32) / l)
    yield
    c = _bdot(jnp.concatenate(outs, axis=1), wo_ref[...])
    return x + _rms(c, g_ref[1:2, :])


def _mlp_block(x, g_ref, wup_ref, wdn_ref):
    hn = _rms(x, g_ref[0:1, :]).astype(BF16)
    acc = None
    for c in range(MLP_PARTS):
        sl = slice(c * (D_FF // MLP_PARTS), (c + 1) * (D_FF // MLP_PARTS))
        h = jnp.maximum(jnp.dot(hn, wup_ref[:, sl], preferred_element_type=F32), 0.0)
        part = jnp.dot((h * h).astype(BF16), wdn_ref[sl, :], preferred_element_type=F32)
        acc = part if acc is None else acc + part
        if c + 1 < MLP_PARTS:
            yield
    return x + _rms(acc, g_ref[1:2, :])


def _xattn_mlp_block(x, k_ref, v_ref, gains_ref, wq_ref, wo_ref, wup_ref, wdn_ref):
    x1 = yield from _xattn_block(x, k_ref, v_ref, gains_ref.at[2:4], wq_ref, wo_ref)
    yield
    return (yield from _mlp_block(x1, gains_ref.at[4:6], wup_ref, wdn_ref))


_LAYER0_ORDER = ("b", "a", "b", "b") + ("a", "b") * 4 + ("b",) * (MLP_PARTS - 4)


def _layer0_kernel(x_ref, k_ref, v_ref, gains_ref, convw_ref, convb_ref, win_ref, bin_ref, gw_ref, gb_ref,
                   lam_ref, wout_ref, bout_ref, wq_ref, wo_ref, wup_ref, wdn_ref, o_ref,
                   ubuf, a_s, b_s, h_s, hcar, mid, *, tiles_per_seq):
    i = pl.program_id(0)

    @pl.when(i == 0)
    def _():
        mid[...] = jnp.zeros(mid.shape, F32)

    @pl.when(i % tiles_per_seq == 0)
    def _():
        ubuf[...] = jnp.zeros(ubuf.shape, F32)
        hcar[...] = jnp.zeros(hcar.shape, F32)

    done = _run_interleaved(_LAYER0_ORDER, {
        "a": _rglru_block(x_ref[0], gains_ref.at[0:2], convw_ref, convb_ref, win_ref, bin_ref, gw_ref,
                          gb_ref, lam_ref, wout_ref, bout_ref, ubuf, a_s, b_s, h_s, hcar),
        "b": _xattn_mlp_block(mid[...], k_ref, v_ref, gains_ref, wq_ref, wo_ref, wup_ref, wdn_ref),
    })
    o_ref[0] = done["b"]
    mid[...] = done["a"]


def _layer0(x, kv, layer, gains, conv_w, conv_b, w_in_bf, b_in, gate_w_bf, gate_b, lam, w_out_bf, b_out,
            stacked):
    B, S, D = x.shape
    M = kv.shape[2]
    ts = min(SEQ_TILE, S)
    n = S // ts
    last = B * n - 1

    def cur(i):
        j = jnp.minimum(i, last)
        return j // n, j % n

    def prev(i):
        j = jnp.maximum(i - 1, 0)
        return j // n, j % n

    weights = [conv_w, conv_b, w_in_bf, b_in, gate_w_bf, gate_b, lam, w_out_bf, b_out]
    return pl.pallas_call(
        functools.partial(_layer0_kernel, tiles_per_seq=n),
        grid=(B * n + 1,),
        in_specs=[
            pl.BlockSpec((1, ts, D), lambda i: (*cur(i), 0)),
            pl.BlockSpec((None, None, M, D), lambda i: (layer, prev(i)[0], 0, 0)),
            pl.BlockSpec((None, None, M, D), lambda i: (layer, prev(i)[0], 0, 1)),
            _const_spec(gains.shape),
        ] + [_const_spec(w.shape, single=True) for w in weights] + [_layer_spec(w.shape, layer) for w in stacked],
        out_specs=pl.BlockSpec((1, ts, D), lambda i: (*prev(i), 0)),
        out_shape=jax.ShapeDtypeStruct((B, S, D), F32),
        scratch_shapes=[
            pltpu.VMEM((SUBLANES, D), F32),
            pltpu.VMEM((ts, D), F32),
            pltpu.VMEM((ts, D), F32),
            pltpu.VMEM((ts, D), F32),
            pltpu.VMEM((1, D), F32),
            pltpu.VMEM((ts, D), F32),
        ],
        compiler_params=_cparams("arbitrary", fuse_inputs=[False] * 4 + [w.dtype == BF16 for w in weights]
                                 + [True] * len(stacked)),
        name="layer0",
    )(x, kv, kv, gains, *weights, *stacked)


def _head_sum(x, e_ref, exact=True):
    hi = x.astype(BF16)
    lo = (x - hi.astype(F32)).astype(BF16) if exact else None
    e = e_ref[...]
    parts = []
    for q in range(N_QUADS):
        sl = slice(q * QUAD, (q + 1) * QUAD)
        part = jnp.dot(hi[:, sl], e, preferred_element_type=F32)
        if exact:
            part = part + jnp.dot(lo[:, sl], e, preferred_element_type=F32)
        parts.append(part)
    return jnp.concatenate(parts, axis=1)


def _rwkv_pre_kernel(x_ref, g_ref, mu_ref, wrkv_ref, w0_ref, w1_ref, w2_ref, a0_ref, a1_ref, a2_ref,
                     g1_ref, g2_ref, kk_ref, ka_ref, rk_ref, e_ref,
                     r_out, c_out, k_out, v_out, a_out, b_out, g_out, bonus_out, hbuf, *, ts):
    s = pl.program_id(1)

    @pl.when(s == 0)
    def _():
        hbuf[...] = jnp.zeros((SUBLANES, D_MODEL), F32)

    hn = _rms(x_ref[0], g_ref[...])
    xx = (_shift_rows(hn, hbuf[...], 1) - hn).astype(BF16)
    hbuf[...] = hn[ts - SUBLANES:, :]
    hn_bf = hn.astype(BF16)
    mu_bf = mu_ref[...].astype(BF16)

    def mix(i):
        return hn_bf + xx * mu_bf[i:i + 1, :]

    r = _bdot(mix(0), wrkv_ref[0])
    k = _bdot(mix(2), wrkv_ref[1])
    v = _bdot(mix(3), wrkv_ref[2])
    z = w0_ref[...] + _bdot(jnp.tanh(_bdot(mix(1), w1_ref[...])), w2_ref[...])
    lw = DECAY_LOG_SCALE * _sigmoid(z)
    alr = _sigmoid(a0_ref[...] + _bdot(_bdot(mix(4), a1_ref[...]), a2_ref[...]))
    g = _bdot(_sigmoid(_bdot(mix(5), g1_ref[...])), g2_ref[...])

    kk = k * kk_ref[...]
    kk = kk * lax.rsqrt(jnp.maximum(_head_sum(kk * kk, e_ref, exact=False), RWKV_KK_NORM_FLOOR ** 2))
    ka = ka_ref[...]
    k = k * ((1.0 - ka) + alr * ka)
    L = min(SCAN_CHUNK, ts)
    rows = min(QUAD, ts)
    ti = lax.broadcasted_iota(jnp.int32, (rows, rows), 0)
    si = lax.broadcasted_iota(jnp.int32, (rows, rows), 1)
    tri = ((ti // L == si // L) & (ti >= si)).astype(BF16)
    lw_hi = lw.astype(BF16)
    lw_lo = (lw - lw_hi.astype(F32)).astype(BF16)
    r_out[0] = r.astype(BF16)
    c_out[0] = jnp.concatenate(
        [jnp.dot(tri, lw_hi[i:i + rows], preferred_element_type=F32)
         + jnp.dot(tri, lw_lo[i:i + rows], preferred_element_type=F32) for i in range(0, ts, rows)], axis=0)
    k_out[0] = k.astype(BF16)
    v_out[0] = v.astype(BF16)
    a_out[0] = (-kk).astype(BF16)
    b_out[0] = (kk * alr).astype(BF16)
    g_out[0] = g.astype(BF16)
    bonus_out[0] = (_head_sum(r * k * rk_ref[...], e_ref, exact=False) * v).astype(BF16)


def _rwkv_pre(x, g0, mu, w_rkv_bf, w0, w1_bf, w2_bf, a0, a1_bf, a2_bf, g1_bf, g2_bf, k_k, k_a, r_k, e):
    B, S, D = x.shape
    ts = min(PRE_TILE, S)
    tile = pl.BlockSpec((1, ts, D), lambda b, s: (b, s, 0))
    consts = [g0, mu, w_rkv_bf, w0, w1_bf, w2_bf, a0, a1_bf, a2_bf, g1_bf, g2_bf, k_k, k_a, r_k, e]
    return pl.pallas_call(
        functools.partial(_rwkv_pre_kernel, ts=ts),
        grid=(B, S // ts),
        in_specs=[tile] + [_const_spec(c.shape, single=c.size >= D * RWKV_HEAD) for c in consts],
        out_specs=[tile] * 8,
        out_shape=[jax.ShapeDtypeStruct((B, S, D), F32 if i == 1 else BF16) for i in range(8)],
        scratch_shapes=[pltpu.VMEM((SUBLANES, D), F32)],
        compiler_params=_cparams("arbitrary", "arbitrary"),
        name="rwkv_pre",
    )(x, *consts)


_BNN = (((2,), (1,)), ((0,), (0,)))
_BNT = (((2,), (2,)), ((0,), (0,)))
_BTN = (((1,), (1,)), ((0,), (0,)))


def _qdot(a, b, dims=_BNN):
    return lax.dot_general(a.astype(SCAN_DTYPE), b.astype(SCAN_DTYPE), dims, preferred_element_type=F32)


def _rwkv_scan_kernel(r_ref, c_ref, k_ref, v_ref, a_ref, b_ref, y_ref, state, *, L):
    @pl.when(pl.program_id(1) == 0)
    def _():
        state[...] = jnp.zeros(state.shape, F32)

    G = c_ref.shape[0]
    c = c_ref[...]
    first_row = lax.broadcasted_iota(jnp.int32, c.shape, 1) == 0
    c_prev = jnp.where(first_row, 0.0, pltpu.roll(c, shift=1, axis=1))
    c_end = c[:, L - 1:L, :]
    gam = jnp.exp(c)
    gam_inv = jnp.exp(-c)
    gam_end = jnp.exp(c_end - c)
    b_in = b_ref[...].astype(F32)
    k_in = k_ref[...].astype(F32)
    r_t = r_ref[...].astype(F32) * gam
    a_t = a_ref[...].astype(F32) * jnp.exp(c_prev)
    b_t = b_in * gam_inv
    k_t = k_in * gam_inv
    b_h = b_in * gam_end
    k_h = k_in * gam_end
    gam_last = jnp.exp(c_end)

    hq = QUAD // RWKV_HEAD
    t_idx = lax.broadcasted_iota(jnp.int32, (L, hq * L), 0)
    s_idx = lax.broadcasted_iota(jnp.int32, (L, hq * L), 1) % L
    m_strict = t_idx > s_idx
    m_incl = t_idx >= s_idx
    eye = (t_idx == s_idx).astype(F32)
    m_off = []
    m = 1
    while m < L:
        m_off.append(((t_idx // (2 * m)) == (s_idx // (2 * m))) & ((t_idx % (2 * m)) >= m)
                     & ((s_idx % (2 * m)) < m))
        m *= 2
    qrow = lax.broadcasted_iota(jnp.int32, (QUAD, QUAD), 0)
    qcol = lax.broadcasted_iota(jnp.int32, (QUAD, QUAD), 1)
    m_state = (qrow // RWKV_HEAD) == (qcol // RWKV_HEAD)
    lane_head = lax.broadcasted_iota(jnp.int32, (L, QUAD), 1) // RWKV_HEAD
    lane_blk = lax.broadcasted_iota(jnp.int32, (L, hq * L), 1) // L

    def quads(z):
        return jnp.stack([z[g][:, q * QUAD:(q + 1) * QUAD] for g in range(G) for q in range(N_QUADS)], axis=0)

    def bdiag(z, blk):
        z = z.astype(SCAN_DTYPE)
        zero = jnp.zeros_like(z)
        return jnp.concatenate([jnp.where(blk == h, z, zero) for h in range(hq)], axis=1)

    a_q, r_q, v_q, b_q, k_q = quads(a_t), quads(r_t), quads(v_ref[...]), quads(b_t), quads(k_t)
    ar = jnp.concatenate([a_q, r_q], axis=1)
    g_b = _qdot(ar, bdiag(b_q, lane_head), _BNT)
    g_k = _qdot(ar, bdiag(k_q, lane_head), _BNT)
    n_ab = g_b[:, :L]
    a_rb = jnp.where(m_incl, g_b[:, L:], 0.0)
    a_ak = jnp.where(m_strict, g_k[:, :L], 0.0)
    a_rk = jnp.where(m_incl, g_k[:, L:], 0.0)
    tinv = eye + jnp.where(m_off[0], n_ab, 0.0)
    levels = [jnp.where(m_lvl, n_ab, 0.0) for m_lvl in m_off[1:]]
    for start in range(0, len(levels), LEVEL_GROUP):
        group = levels[start:start + LEVEL_GROUP]
        z = _qdot(jnp.concatenate(group, axis=1), bdiag(tinv, lane_blk))
        pend = [z[:, i * L:(i + 1) * L] for i in range(len(group))]
        while pend:
            p_lvl = pend.pop(0)
            z = _qdot(jnp.concatenate([tinv] + pend, axis=1), bdiag(p_lvl, lane_blk))
            tinv = tinv + z[:, :L]
            pend = [q + z[:, (i + 1) * L:(i + 2) * L] for i, q in enumerate(pend)]
    s0 = state[...]
    xs = _qdot(ar, s0, _BNT)
    vs = _qdot(jnp.concatenate([a_ak, a_rk], axis=1), bdiag(v_q, lane_head))
    u = _qdot(tinv, bdiag(xs[:, :L] + vs[:, :L], lane_head))
    y = xs[:, L:] + vs[:, L:] + _qdot(a_rb, bdiag(u, lane_head))
    upd = _qdot(jnp.concatenate([u, v_q], axis=1), jnp.concatenate([quads(b_h), quads(k_h)], axis=1), _BTN)
    state[...] = s0 * quads(gam_last) + jnp.where(m_state, upd, 0.0)
    for g in range(G):
        y_ref[g] = jnp.concatenate([y[g * N_QUADS + q] for q in range(N_QUADS)], axis=1)


def _rwkv_scan(r, c, k, v, a, b):
    B, S, D = r.shape
    L = min(SCAN_CHUNK, S)
    G = SCAN_ROWS if B % SCAN_ROWS == 0 else 1
    tile = pl.BlockSpec((G, L, D), lambda bb, c: (bb, c, 0))
    return pl.pallas_call(
        functools.partial(_rwkv_scan_kernel, L=L),
        grid=(B // G, S // L),
        in_specs=[tile] * 6,
        out_specs=tile,
        out_shape=jax.ShapeDtypeStruct((B, S, D), F32),
        scratch_shapes=[pltpu.VMEM((G * N_QUADS, QUAD, QUAD), F32)],
        compiler_params=_cparams("arbitrary", "arbitrary"),
        name="rwkv_scan",
    )(r, c, k, v, a, b)


def _rwkv_post_block(x, y, bonus, gate, g_ref, gng_ref, gnb_ref, e_ref, wmix_ref):
    mean = _head_sum(y, e_ref)
    d = y - mean
    yield
    var = _head_sum(d * d, e_ref, exact=False)
    yn = d * lax.rsqrt(var + RWKV_GN_EPS) * gng_ref[...] + gnb_ref[...]
    yield
    t = _bdot((yn + bonus) * gate, wmix_ref[...])
    return x + _rms(t, g_ref[...])


_TAIL_ORDER = ("b", "b", "b") + ("a", "b") * 3 + ("b",) * (MLP_PARTS - 3)


def _rwkv_tail_kernel(x_ref, y_ref, bonus_ref, gate_ref, k_ref, v_ref, gains_ref, gng_ref, gnb_ref, e_ref,
                      wmix_ref, wq_ref, wo_ref, wup_ref, wdn_ref, o_ref, mid):
    @pl.when(pl.program_id(0) == 0)
    def _():
        mid[...] = jnp.zeros(mid.shape, F32)

    done = _run_interleaved(_TAIL_ORDER, {
        "a": _rwkv_post_block(x_ref[0], y_ref[0], bonus_ref[0], gate_ref[0], gains_ref.at[1:2], gng_ref,
                              gnb_ref, e_ref, wmix_ref),
        "b": _xattn_mlp_block(mid[...], k_ref, v_ref, gains_ref, wq_ref, wo_ref, wup_ref, wdn_ref),
    })
    o_ref[0] = done["b"]
    mid[...] = done["a"]


def _rwkv_tail(x, y, bonus, gate, kv, layer, gains, gn_g, gn_b, e_mean, w_mix_bf, stacked):
    B, S, D = x.shape
    M = kv.shape[2]
    ts = min(TAIL_TILE, S)
    n = S // ts
    last = B * n - 1

    def cur(i):
        j = jnp.minimum(i, last)
        return j // n, j % n

    def prev(i):
        j = jnp.maximum(i - 1, 0)
        return j // n, j % n

    tile_in = pl.BlockSpec((1, ts, D), lambda i: (*cur(i), 0))
    return pl.pallas_call(
        _rwkv_tail_kernel,
        grid=(B * n + 1,),
        in_specs=[tile_in] * 4 + [
            pl.BlockSpec((None, None, M, D), lambda i: (layer, prev(i)[0], 0, 0)),
            pl.BlockSpec((None, None, M, D), lambda i: (layer, prev(i)[0], 0, 1)),
            _const_spec(gains.shape), _const_spec(gn_g.shape), _const_spec(gn_b.shape),
            _const_spec(e_mean.shape), _const_spec(w_mix_bf.shape, single=True),
        ] + [_layer_spec(w.shape, layer) for w in stacked],
        out_specs=pl.BlockSpec((1, ts, D), lambda i: (*prev(i), 0)),
        out_shape=jax.ShapeDtypeStruct((B, S, D), F32),
        scratch_shapes=[pltpu.VMEM((ts, D), F32)],
        compiler_params=_cparams("arbitrary", fuse_inputs=[False] * 10 + [True] * (1 + len(stacked))),
        name="rwkv_tail",
    )(x, y, bonus, gate, kv, kv, gains, gn_g, gn_b, e_mean, w_mix_bf, *stacked)


def _head_selector(value):
    head = jnp.arange(QUAD) // RWKV_HEAD
    return jnp.where(head[:, None] == head[None, :], value, 0.0).astype(BF16)


def kernel(x, mem, ln_gains, mem_norm, a_conv_w, a_conv_b, a_w_in, a_b_in, a_gate_w, a_gate_b, a_lambda, a_w_out, a_b_out, b_mu, b_w_rkv, b_w0, b_w1, b_w2, b_a0, b_a1, b_a2, b_g1, b_g2, b_k_k, b_k_a, b_r_k, b_gn_g, b_gn_b, b_w_o, c_w_q, c_w_kv, c_w_o, m_w_up, m_w_down):
    B, S, D = x.shape
    M = mem.shape[1]
    depth = ln_gains.shape[0]
    assert D == D_MODEL and depth == 2
    bf = lambda w: w.astype(BF16)
    row = lambda p: p.reshape(1, -1)

    kv = _mem_kv(mem.reshape(B * M, D), row(mem_norm), bf(c_w_kv)).reshape(depth, B, M, 2 * D)
    stacked = [bf(c_w_q * (MEM_HEAD_DIM ** -0.5)), bf(c_w_o), bf(m_w_up), bf(m_w_down)]
    e = _head_selector(1.0)
    e_mean = _head_selector(1.0 / RWKV_HEAD)

    for i in range(depth):
        g = ln_gains[i]
        j = i // 2
        if i % 2 == 0:
            x = _layer0(x, kv, i, g, a_conv_w[j], row(a_conv_b[j]), bf(a_w_in[j]), row(a_b_in[j]),
                        bf(a_gate_w[j]), a_gate_b[j].reshape(2, D), row(a_lambda[j]), bf(a_w_out[j]),
                        row(a_b_out[j]), stacked)
        else:
            r, c, k, v, a, b, gate, bonus = _rwkv_pre(
                x, g[0:1], b_mu[j], bf(b_w_rkv[j]), row(b_w0[j]), bf(b_w1[j]), bf(b_w2[j]), row(b_a0[j]),
                bf(b_a1[j]), bf(b_a2[j]), bf(b_g1[j]), bf(b_g2[j]), row(b_k_k[j]), row(b_k_a[j]),
                row(b_r_k[j]), e)
            y = _rwkv_scan(r, c, k, v, a, b)
            x = _rwkv_tail(x, y, bonus, gate, kv, i, g, row(b_gn_g[j]), row(b_gn_b[j]), e_mean, bf(b_w_o[j]),
                           stacked)
    return x
```
